```python
import jax, jax.numpy as jnp
from jax import lax
import numpy as np

D_MODEL = 2048
BATCH = 4
SEQ = 4096
DEPTH = 1

HEAD_DIM = 128
ROPE_DIM = HEAD_DIM // 4
ROPE_THETA = 500000.0
NORM_EPS = 1e-5
NEG_INF = -1e30
SEL_FORCED = 1e9

NSA_HEADS = 8
NSA_GROUPS = 2
NSA_REP = NSA_HEADS // NSA_GROUPS
CMP_BLOCK = 32
CMP_STRIDE = 16
CMP_HIDDEN = 256
SEL_BLOCK = 64
SEL_TOPN = 16
WINDOW = 512
WIN_QBLOCK = 128
NSA_QCHUNK = 64

MOBA_HEADS = 8
MOBA_BLOCK = 256
MOBA_TOPK = 3
MOBA_QCHUNK = 16

N_EXPERTS = 32
TOP_K = 4
D_FF = 2048
SWIGLU_LIMIT = 7.0
SWIGLU_ALPHA = 1.702
EXPERT_ROW_BLOCK = 256

NSA_WIDTH = NSA_HEADS * HEAD_DIM
MOBA_WIDTH = MOBA_HEADS * HEAD_DIM
NSA_KV_WIDTH = NSA_GROUPS * HEAD_DIM
IN_SPLITS = (NSA_WIDTH, 6 * NSA_KV_WIDTH, 3 * NSA_HEADS, 3 * MOBA_WIDTH, 2 * D_MODEL)
IN_WIDTH = sum(IN_SPLITS)

kernel_name = "hybrid_nsa_moba_gptoss_moe"


def rms_norm(x, g):
    xf = x.astype(jnp.float32)
    y = xf * lax.rsqrt(jnp.mean(xf * xf, axis=-1, keepdims=True) + NORM_EPS)
    return (y * g.astype(jnp.float32)).astype(x.dtype)


def rope_partial(x, pos):
    half = ROPE_DIM // 2
    inv_freq = ROPE_THETA ** (-jnp.arange(0, ROPE_DIM, 2, dtype=jnp.float32) / ROPE_DIM)
    ang = pos.astype(jnp.float32)[..., None] * inv_freq
    cos = jnp.cos(ang)[:, :, None, :]
    sin = jnp.sin(ang)[:, :, None, :]
    x1 = x[..., :half].astype(jnp.float32)
    x2 = x[..., half:ROPE_DIM].astype(jnp.float32)
    rot = jnp.concatenate([x1 * cos - x2 * sin, x2 * cos + x1 * sin], axis=-1).astype(x.dtype)
    return jnp.concatenate([rot, x[..., ROPE_DIM:]], axis=-1)


def masked_softmax(scores, mask, axis=-1):
    s = jnp.where(mask, scores.astype(jnp.float32), NEG_INF)
    p = jax.nn.softmax(s, axis=axis)
    return jnp.where(mask, p, 0.0)


def compress_blocks(kv, pe, w1, w2):
    B, S, G, HD = kv.shape
    n_cmp = (S - CMP_BLOCK) // CMP_STRIDE + 1
    idx = np.arange(n_cmp)[:, None] * CMP_STRIDE + np.arange(CMP_BLOCK)[None, :]
    blocks = kv[:, idx] + pe[:, None, :]
    blocks = blocks.transpose(0, 1, 3, 2, 4).reshape(B, n_cmp, G, CMP_BLOCK * HD)
    return jax.nn.gelu(blocks @ w1) @ w2


def nsa_mixer(q, k_cmp, v_cmp, k_slc, v_slc, k_win, v_win, gate_logits, pos,
              pe_k, w_k1, w_k2, pe_v, w_v1, w_v2):
    B, S = q.shape[:2]
    G, R, HD = NSA_GROUPS, NSA_REP, HEAD_DIM
    scale = HD ** -0.5
    t = np.arange(S)
    qg = q.reshape(B, S, G, R, HD).transpose(0, 2, 3, 1, 4)

    n_cmp = (S - CMP_BLOCK) // CMP_STRIDE + 1
    cmp_start = np.arange(n_cmp) * CMP_STRIDE
    cmp_end = cmp_start + CMP_BLOCK - 1
    kc = rope_partial(compress_blocks(k_cmp, pe_k, w_k1, w_k2), pos[:, cmp_end]).transpose(0, 2, 1, 3)
    vc = compress_blocks(v_cmp, pe_v, w_v1, w_v2).transpose(0, 2, 1, 3)
    s_c = jnp.einsum('bgrsd,bgnd->bgrsn', qg, kc) * scale
    p_c = masked_softmax(s_c, cmp_end[None, :] <= t[:, None])
    o_cmp = jnp.einsum('bgrsn,bgnd->bgrsd', p_c.astype(vc.dtype), vc)

    n_slc = S // SEL_BLOCK
    n_sel = min(SEL_TOPN, n_slc)
    slc_start = np.arange(n_slc) * SEL_BLOCK
    overlap = np.clip(np.minimum(cmp_start[:, None] + CMP_BLOCK, slc_start[None, :] + SEL_BLOCK)
                      - np.maximum(cmp_start[:, None], slc_start[None, :]), 0, None)
    cmp_to_slc = jnp.asarray(overlap / CMP_BLOCK, dtype=jnp.float32)
    imp = jnp.einsum('bgrsn,nj->bgsj', p_c, cmp_to_slc)
    cur = t // SEL_BLOCK
    j = np.arange(n_slc)
    valid = j[None, :] <= cur[:, None]
    forced = (j[None, :] == 0) | (j[None, :] == cur[:, None]) | (j[None, :] == cur[:, None] - 1)
    score = jnp.where(forced, SEL_FORCED, jnp.where(valid, imp, -SEL_FORCED))
    top_val, sel_idx = lax.top_k(score, n_sel)
    sel_ok = top_val > -0.5 * SEL_FORCED

    kb = k_slc.transpose(0, 2, 1, 3).reshape(B, G, n_slc, SEL_BLOCK, HD)
    vb = v_slc.transpose(0, 2, 1, 3).reshape(B, G, n_slc, SEL_BLOCK, HD)
    qc_n = S // NSA_QCHUNK
    q_ch = qg.reshape(B, G, R, qc_n, NSA_QCHUNK, HD).transpose(3, 0, 1, 2, 4, 5)
    idx_ch = sel_idx.reshape(B, G, qc_n, NSA_QCHUNK, n_sel).transpose(2, 0, 1, 3, 4)
    ok_ch = sel_ok.reshape(B, G, qc_n, NSA_QCHUNK, n_sel).transpose(2, 0, 1, 3, 4)
    bi = jnp.arange(B)[:, None, None, None]
    gi = jnp.arange(G)[None, :, None, None]

    def sel_step(args):
        c, qc, ic, okc = args
        kg = kb[bi, gi, ic]
        vg = vb[bi, gi, ic]
        s = jnp.einsum('bgrqd,bgqnkd->bgrqnk', qc, kg) * scale
        tq = c * NSA_QCHUNK + jnp.arange(NSA_QCHUNK)
        kpos = ic[..., None] * SEL_BLOCK + jnp.arange(SEL_BLOCK)
        mask = (okc[..., None] & (kpos <= tq[None, None, :, None, None]))[:, :, None]
        p = masked_softmax(s, mask, axis=(-2, -1))
        return jnp.einsum('bgrqnk,bgqnkd->bgrqd', p.astype(vg.dtype), vg)

    o_slc = lax.map(sel_step, (jnp.arange(qc_n), q_ch, idx_ch, ok_ch))
    o_slc = o_slc.transpose(1, 2, 3, 0, 4, 5).reshape(B, G, R, S, HD)

    nwb = S // WIN_QBLOCK
    nb_back = WINDOW // WIN_QBLOCK
    band = (nb_back + 1) * WIN_QBLOCK
    kw = jnp.pad(k_win.transpose(0, 2, 1, 3).reshape(B, G, nwb, WIN_QBLOCK, HD),
                 ((0, 0), (0, 0), (nb_back, 0), (0, 0), (0, 0)))
    vw = jnp.pad(v_win.transpose(0, 2, 1, 3).reshape(B, G, nwb, WIN_QBLOCK, HD),
                 ((0, 0), (0, 0), (nb_back, 0), (0, 0), (0, 0)))
    k_band = jnp.concatenate([kw[:, :, i:i + nwb] for i in range(nb_back + 1)], axis=3)
    v_band = jnp.concatenate([vw[:, :, i:i + nwb] for i in range(nb_back + 1)], axis=3)
    qw = qg.reshape(B, G, R, nwb, WIN_QBLOCK, HD)
    s_w = jnp.einsum('bgrnqd,bgnkd->bgrnqk', qw, k_band) * scale
    qpos = np.arange(S).reshape(nwb, WIN_QBLOCK)
    kpos = (np.arange(nwb)[:, None] - nb_back) * WIN_QBLOCK + np.arange(band)[None, :]
    dpos = qpos[:, :, None] - kpos[:, None, :]
    mask_w = (dpos >= 0) & (dpos < WINDOW) & (kpos[:, None, :] >= 0)
    p_w = masked_softmax(s_w, mask_w)
    o_win = jnp.einsum('bgrnqk,bgnkd->bgrnqd', p_w.astype(v_band.dtype), v_band).reshape(B, G, R, S, HD)

    g = jax.nn.sigmoid(gate_logits.astype(jnp.float32)).astype(q.dtype)
    g = g.reshape(B, S, 3, G, R).transpose(2, 0, 3, 4, 1)[..., None]
    o = g[0] * o_cmp + g[1] * o_slc + g[2] * o_win
    return o.transpose(0, 3, 1, 2, 4).reshape(B, S, NSA_WIDTH)


def moba_mixer(q, k, v):
    B, S, H, HD = q.shape
    scale = HD ** -0.5
    q = q.transpose(0, 2, 1, 3)
    k = k.transpose(0, 2, 1, 3)
    v = v.transpose(0, 2, 1, 3)
    n_blk = -(-S // MOBA_BLOCK)
    pad = n_blk * MOBA_BLOCK - S
    kb = jnp.pad(k, ((0, 0), (0, 0), (0, pad), (0, 0))).reshape(B, H, n_blk, MOBA_BLOCK, HD)
    vb = jnp.pad(v, ((0, 0), (0, 0), (0, pad), (0, 0))).reshape(B, H, n_blk, MOBA_BLOCK, HD)
    k_mean = jnp.mean(kb.astype(jnp.float32), axis=3)
    gate = jnp.einsum('bhsd,bhnd->bhsn', q.astype(jnp.float32), k_mean)
    t = np.arange(S)
    past = np.arange(n_blk)[None, :] < (t // MOBA_BLOCK)[:, None]
    n_top = min(MOBA_TOPK, n_blk)
    top_val, top_idx = lax.top_k(jnp.where(past, gate, -SEL_FORCED), n_top)
    top_ok = top_val > -0.5 * SEL_FORCED
    QC = MOBA_QCHUNK
    n_ch = S // QC
    q_ch = q.reshape(B, H, n_ch, QC, HD).transpose(2, 0, 1, 3, 4)
    idx_ch = top_idx.reshape(B, H, n_ch, QC, n_top).transpose(2, 0, 1, 3, 4)
    ok_ch = top_ok.reshape(B, H, n_ch, QC, n_top).transpose(2, 0, 1, 3, 4)
    bi = jnp.arange(B)[:, None, None, None]
    hi = jnp.arange(H)[None, :, None, None]
    n_sel_keys = n_top * MOBA_BLOCK

    def step(args):
        c, qc, ic, okc = args
        kg = kb[bi, hi, ic]
        vg = vb[bi, hi, ic]
        own = (c * QC) // MOBA_BLOCK
        k_own = kb[:, :, own]
        v_own = vb[:, :, own]
        s_sel = jnp.einsum('bhqd,bhqnkd->bhqnk', qc, kg).reshape(B, H, QC, n_sel_keys)
        s_own = jnp.einsum('bhqd,bhkd->bhqk', qc, k_own)
        tq = c * QC + jnp.arange(QC)
        kpos = own * MOBA_BLOCK + jnp.arange(MOBA_BLOCK)
        m_sel = jnp.broadcast_to(okc[..., None], (B, H, QC, n_top, MOBA_BLOCK)).reshape(B, H, QC, n_sel_keys)
        m_own = jnp.broadcast_to(kpos[None, :] <= tq[:, None], (B, H, QC, MOBA_BLOCK))
        p = masked_softmax(jnp.concatenate([s_sel, s_own], axis=-1) * scale,
                           jnp.concatenate([m_sel, m_own], axis=-1)).astype(v.dtype)
        p_sel = p[..., :n_sel_keys].reshape(B, H, QC, n_top, MOBA_BLOCK)
        return (jnp.einsum('bhqnk,bhqnkd->bhqd', p_sel, vg)
                + jnp.einsum('bhqk,bhkd->bhqd', p[..., n_sel_keys:], v_own))

    o = lax.map(step, (jnp.arange(n_ch), q_ch, idx_ch, ok_ch))
    return o.transpose(1, 0, 3, 2, 4).reshape(B, S, MOBA_WIDTH)


def moe_ffn(h, w_router, b_router, w_gate_up, b_gate_up, w_down, b_down):
    B, S, D = h.shape
    T = B * S
    ht = h.reshape(T, D)
    logits = (ht @ w_router + b_router).astype(jnp.float32)
    top_logit, top_e = lax.top_k(logits, TOP_K)
    top_w = jax.nn.softmax(top_logit, axis=-1).astype(h.dtype)
    A = T * TOP_K
    e_flat = top_e.reshape(A)
    tok_flat = jnp.repeat(jnp.arange(T, dtype=jnp.int32), TOP_K)
    w_flat = top_w.reshape(A)
    order = jnp.argsort(e_flat)
    e_s, tok_s, w_s = e_flat[order], tok_flat[order], w_flat[order]
    counts = jnp.bincount(e_flat, length=N_EXPERTS)
    padded = (counts + EXPERT_ROW_BLOCK - 1) // EXPERT_ROW_BLOCK * EXPERT_ROW_BLOCK
    p_end = jnp.cumsum(padded)
    p_start = p_end - padded
    c_start = jnp.cumsum(counts) - counts
    dest = p_start[e_s] + jnp.arange(A, dtype=jnp.int32) - c_start[e_s]
    P = A + N_EXPERTS * EXPERT_ROW_BLOCK
    n_blk = P // EXPERT_ROW_BLOCK
    x_pad = jnp.zeros((P, D), h.dtype).at[dest].set(ht[tok_s])
    blk_expert = jnp.minimum(jnp.searchsorted(p_end, jnp.arange(n_blk) * EXPERT_ROW_BLOCK, side='right'),
                             N_EXPERTS - 1)

    def expert_block(args):
        xb, e = args
        gu = xb @ w_gate_up[e] + b_gate_up[e]
        gate = jnp.minimum(gu[:, :D_FF], SWIGLU_LIMIT)
        lin = jnp.clip(gu[:, D_FF:], -SWIGLU_LIMIT, SWIGLU_LIMIT)
        act = gate * jax.nn.sigmoid(SWIGLU_ALPHA * gate) * (lin + 1.0)
        return act @ w_down[e] + b_down[e]

    y_pad = lax.map(expert_block, (x_pad.reshape(n_blk, EXPERT_ROW_BLOCK, D), blk_expert)).reshape(P, D)
    out = jnp.zeros((T, D), h.dtype).at[tok_s].add(y_pad[dest] * w_s[:, None])
    return out.reshape(B, S, D)


def setup_inputs(seed: int = 0) -> dict:
    key = jax.random.key(seed)
    ks = jax.random.split(key, 22)
    f32 = jnp.float32
    L = DEPTH

    def nrm(k, shape, scale):
        return jax.random.normal(k, shape, f32) * scale

    return {
        "x": nrm(ks[0], (BATCH, SEQ, D_MODEL), 1.0),
        "positions": jnp.broadcast_to(jnp.arange(SEQ, dtype=jnp.int32), (BATCH, SEQ)),
        "g_attn_norm": 1.0 + nrm(ks[1], (L, D_MODEL), 0.02),
        "w_in": nrm(ks[2], (L, D_MODEL, IN_WIDTH), D_MODEL ** -0.5),
        "pe_cmp_k": nrm(ks[3], (L, CMP_BLOCK, HEAD_DIM), 0.02),
        "w_cmp_k1": nrm(ks[4], (L, CMP_BLOCK * HEAD_DIM, CMP_HIDDEN), (CMP_BLOCK * HEAD_DIM) ** -0.5),
        "w_cmp_k2": nrm(ks[5], (L, CMP_HIDDEN, HEAD_DIM), CMP_HIDDEN ** -0.5),
        "pe_cmp_v": nrm(ks[6], (L, CMP_BLOCK, HEAD_DIM), 0.02),
        "w_cmp_v1": nrm(ks[7], (L, CMP_BLOCK * HEAD_DIM, CMP_HIDDEN), (CMP_BLOCK * HEAD_DIM) ** -0.5),
        "w_cmp_v2": nrm(ks[8], (L, CMP_HIDDEN, HEAD_DIM), CMP_HIDDEN ** -0.5),
        "w_proj_nsa": nrm(ks[9], (L, NSA_WIDTH, D_MODEL), NSA_WIDTH ** -0.5),
        "w_proj_moba": nrm(ks[10], (L, MOBA_WIDTH, D_MODEL), MOBA_WIDTH ** -0.5),
        "w_out": nrm(ks[11], (L, D_MODEL, D_MODEL), D_MODEL ** -0.5),
        "g_ffn_norm": 1.0 + nrm(ks[12], (L, D_MODEL), 0.02),
        "w_router": nrm(ks[13], (L, D_MODEL, N_EXPERTS), D_MODEL ** -0.5),
        "b_router": nrm(ks[14], (L, N_EXPERTS), 0.01),
        "w_gate_up": nrm(ks[15], (L, N_EXPERTS, D_MODEL, 2 * D_FF), D_MODEL ** -0.5),
        "b_gate_up": nrm(ks[16], (L, N_EXPERTS, 2 * D_FF), 0.01),
        "w_down": nrm(ks[17], (L, N_EXPERTS, D_FF, D_MODEL), D_FF ** -0.5),
        "b_down": nrm(ks[18], (L, N_EXPERTS, D_MODEL), 0.01),
        "g_final_norm": 1.0 + nrm(ks[19], (D_MODEL,), 0.02),
    }


def reference(x, positions, g_attn_norm, w_in, pe_cmp_k, w_cmp_k1, w_cmp_k2, pe_cmp_v, w_cmp_v1, w_cmp_v2,
              w_proj_nsa, w_proj_moba, w_out, g_ffn_norm, w_router, b_router, w_gate_up, b_gate_up,
              w_down, b_down, g_final_norm):
    B, S, D = x.shape
    split_at = np.cumsum(IN_SPLITS)[:-1]
    for layer in range(DEPTH):
        h = rms_norm(x, g_attn_norm[layer])
        proj = h @ w_in[layer]
        q_a, kv_a, g_nsa, qkv_b, g_merge = jnp.split(proj, split_at, axis=-1)
        q_a = rope_partial(q_a.reshape(B, S, NSA_HEADS, HEAD_DIM), positions)
        kv_a = kv_a.reshape(B, S, 6, NSA_GROUPS, HEAD_DIM)
        o_a = nsa_mixer(q_a, kv_a[:, :, 0], kv_a[:, :, 1],
                        rope_partial(kv_a[:, :, 2], positions), kv_a[:, :, 3],
                        rope_partial(kv_a[:, :, 4], positions), kv_a[:, :, 5],
                        g_nsa, positions,
                        pe_cmp_k[layer], w_cmp_k1[layer], w_cmp_k2[layer],
                        pe_cmp_v[layer], w_cmp_v1[layer], w_cmp_v2[layer])
        qkv_b = qkv_b.reshape(B, S, 3, MOBA_HEADS, HEAD_DIM)
        o_b = moba_mixer(rope_partial(qkv_b[:, :, 0], positions),
                         rope_partial(qkv_b[:, :, 1], positions), qkv_b[:, :, 2])
        gm = jax.nn.sigmoid(g_merge.astype(jnp.float32)).astype(x.dtype).reshape(B, S, 2, D)
        mix = gm[:, :, 0] * (o_a @ w_proj_nsa[layer]) + gm[:, :, 1] * (o_b @ w_proj_moba[layer])
        x = x + mix @ w_out[layer]
        h = rms_norm(x, g_ffn_norm[layer])
        x = x + moe_ffn(h, w_router[layer], b_router[layer], w_gate_up[layer], b_gate_up[layer],
                        w_down[layer], b_down[layer])
    return rms_norm(x, g_final_norm)
```

```python
import functools

import jax
import jax.numpy as jnp
import numpy as np
from jax import lax
from jax.experimental import pallas as pl
from jax.experimental.pallas import tpu as pltpu

F32 = jnp.float32
BF16 = jnp.bfloat16
I32 = jnp.int32

HEAD_DIM = 128
ROPE_DIM = HEAD_DIM // 4
ROPE_HALF = ROPE_DIM // 2
ROPE_THETA = 500000.0
NORM_EPS = 1e-5
NEG_INF = -1e30
REMOVED = -3e38
SEL_FORCED = 1e9

NSA_HEADS = 8
NSA_GROUPS = 2
NSA_REP = NSA_HEADS // NSA_GROUPS
CMP_BLOCK = 32
CMP_STRIDE = 16
CMP_HIDDEN = 256
SEL_BLOCK = 64
SEL_TOPN = 16
WINDOW = 512
MOBA_HEADS = 8
MOBA_BLOCK = 256
MOBA_TOPK = 3
TOP_K = 4
SWIGLU_LIMIT = 7.0
SWIGLU_ALPHA = 1.702
ROW_BLOCK = 256

NSA_WIDTH = NSA_HEADS * HEAD_DIM
MOBA_WIDTH = MOBA_HEADS * HEAD_DIM
KV_WIDTH = NSA_GROUPS * HEAD_DIM
SCALE = HEAD_DIM ** -0.5

COL_TILE = 512
OFF_QA = 0
OFF_KSLC = NSA_WIDTH
OFF_KWIN = OFF_KSLC + KV_WIDTH
OFF_QB = OFF_KWIN + KV_WIDTH
OFF_KB = OFF_QB + MOBA_WIDTH
ROPE_COLS = OFF_KB + MOBA_WIDTH
OFF_KCMP = ROPE_COLS
OFF_VCMP = OFF_KCMP + KV_WIDTH
OFF_VSLC = OFF_VCMP + KV_WIDTH
OFF_VWIN = OFF_VSLC + KV_WIDTH
OFF_VB = OFF_VWIN + KV_WIDTH
PLAIN_END = OFF_VB + MOBA_WIDTH
N_ROPE_TILES = ROPE_COLS // COL_TILE
N_PLAIN_TILES = (PLAIN_END - ROPE_COLS) // COL_TILE
Q_TILES = tuple(range(OFF_QA // COL_TILE, OFF_KSLC // COL_TILE)) + tuple(range(OFF_QB // COL_TILE, OFF_KB // COL_TILE))

VMEM_LIMIT = 56 * 1024 * 1024
NSA_TQ = 128
SEL_KT = 512
GATE_ROWS = 16


def _cparams(sem):
    return pltpu.CompilerParams(dimension_semantics=sem, vmem_limit_bytes=VMEM_LIMIT)


def _sigmoid(z):
    return 1.0 / (1.0 + jnp.exp(-z))


def _dot(a, b):
    return jnp.dot(a, b, preferred_element_type=F32)


def _dot_nt(a, b):
    return lax.dot_general(a, b, (((1,), (1,)), ((), ())), preferred_element_type=F32)


def _split_hi_lo(v):
    hi = v.astype(BF16)
    lo = (v - hi.astype(F32)).astype(BF16)
    return hi, lo


def _topk_rows(work, row_idx, n_rows, k):
    picks = []
    for _ in range(k):
        m = jnp.max(work, axis=0, keepdims=True)
        first = jnp.min(jnp.where(work == m, row_idx, n_rows), axis=0, keepdims=True)
        pick = row_idx == first
        picks.append((m, first, pick))
        work = jnp.where(pick, REMOVED, work)
    return picks


def _norm_gates_kernel(x_ref, g_ref, wgt_ref, h_ref, gt_ref):
    x = x_ref[...]
    h = x * lax.rsqrt(jnp.mean(x * x, axis=-1, keepdims=True) + NORM_EPS) * g_ref[...]
    hb = h.astype(BF16)
    h_ref[...] = hb
    gt_ref[...] = _sigmoid(_dot_nt(wgt_ref[...], hb))


def _norm_gates(x2, g, wgt, tm):
    T, D = x2.shape
    R = wgt.shape[0]
    return pl.pallas_call(
        _norm_gates_kernel,
        grid=(T // tm,),
        in_specs=[pl.BlockSpec((tm, D), lambda i: (i, 0)),
                  pl.BlockSpec((1, D), lambda i: (0, 0)),
                  pl.BlockSpec((R, D), lambda i: (0, 0))],
        out_specs=[pl.BlockSpec((tm, D), lambda i: (i, 0)),
                   pl.BlockSpec((R, tm), lambda i: (0, i))],
        out_shape=[jax.ShapeDtypeStruct((T, D), BF16), jax.ShapeDtypeStruct((R, T), F32)],
        compiler_params=_cparams(("arbitrary",)),
        name="norm_gates",
    )(x2, g, wgt)


def _rope_tile(xh, c, s, lane):
    rot = jnp.where(lane < ROPE_HALF, pltpu.roll(xh, HEAD_DIM - ROPE_HALF, 1), pltpu.roll(xh, ROPE_HALF, 1))
    return xh * c + rot * s


def _in_proj_kernel(h_ref, w_ref, c_ref, s_ref, o_ref):
    j = pl.program_id(1)
    acc = _dot(h_ref[...], w_ref[...])

    @pl.when(j < N_ROPE_TILES)
    def _():
        is_q = j == Q_TILES[0]
        for qt in Q_TILES[1:]:
            is_q = is_q | (j == qt)
        f = jnp.where(is_q, SCALE, 1.0).astype(F32)
        c = c_ref[...] * f
        s = s_ref[...] * f
        lane = lax.broadcasted_iota(I32, c.shape, 1)
        for hh in range(COL_TILE // HEAD_DIM):
            sl = slice(hh * HEAD_DIM, (hh + 1) * HEAD_DIM)
            o_ref[:, sl] = _rope_tile(acc[:, sl], c, s, lane).astype(BF16)

    @pl.when((j >= N_ROPE_TILES) & (j < N_ROPE_TILES + N_PLAIN_TILES))
    def _():
        o_ref[...] = acc.astype(BF16)

    @pl.when(j >= N_ROPE_TILES + N_PLAIN_TILES)
    def _():
        o_ref[...] = _sigmoid(acc).astype(BF16)


def _in_proj(h, w, cos_t, sin_t, tm):
    T, D = h.shape
    N = w.shape[1]
    return pl.pallas_call(
        _in_proj_kernel,
        grid=(T // tm, N // COL_TILE),
        in_specs=[pl.BlockSpec((tm, D), lambda i, j: (i, 0)),
                  pl.BlockSpec((D, COL_TILE), lambda i, j: (0, j)),
                  pl.BlockSpec((tm, HEAD_DIM), lambda i, j: (i, 0)),
                  pl.BlockSpec((tm, HEAD_DIM), lambda i, j: (i, 0))],
        out_specs=pl.BlockSpec((tm, COL_TILE), lambda i, j: (i, j)),
        out_shape=jax.ShapeDtypeStruct((T, N), BF16),
        compiler_params=_cparams(("arbitrary", "arbitrary")),
        name="in_proj",
    )(h, w, cos_t, sin_t)


def _gelu_tanh(x):
    return 0.5 * x * (1.0 + jnp.tanh(0.7978845608028654 * (x + 0.044715 * x * x * x)))


def _compress_kernel(x_ref, pek_ref, w1k_ref, w2k_ref, pev_ref, w1v_ref, w2v_ref, c_ref, s_ref, kc_ref, vc_ref):
    nc = x_ref.shape[0]
    half = CMP_STRIDE * HEAD_DIM
    tok_w = 2 * KV_WIDTH
    for which, (pe_ref, w1_ref, w2_ref, out_ref) in enumerate(
            ((pek_ref, w1k_ref, w2k_ref, kc_ref), (pev_ref, w1v_ref, w2v_ref, vc_ref))):
        for g in range(NSA_GROUPS):
            acc_a = jnp.zeros((nc, CMP_HIDDEN), F32)
            acc_b = jnp.zeros((nc, CMP_HIDDEN), F32)
            for l in range(CMP_STRIDE):
                off = l * tok_w + which * KV_WIDTH + g * HEAD_DIM
                xl = x_ref[:, off:off + HEAD_DIM].astype(F32)
                xa = (xl + pe_ref[l:l + 1, :]).astype(BF16)
                xb = (xl + pe_ref[CMP_STRIDE + l:CMP_STRIDE + l + 1, :]).astype(BF16)
                acc_a = acc_a + _dot(xa, w1_ref[l * HEAD_DIM:(l + 1) * HEAD_DIM, :])
                acc_b = acc_b + _dot(xb, w1_ref[half + l * HEAD_DIM:half + (l + 1) * HEAD_DIM, :])
            hid = _gelu_tanh(acc_a + pltpu.roll(acc_b, nc - 1, 0))
            out = _dot(hid.astype(BF16), w2_ref[...])
            if which == 0:
                lane = lax.broadcasted_iota(I32, out.shape, 1)
                out = _rope_tile(out, c_ref[...], s_ref[...], lane)
            out_ref[g] = out.astype(BF16)


def _compress(xc, pek, w1k, w2k, pev, w1v, w2v, cos_c, sin_c):
    B, nc, W = xc.shape
    full = lambda a: pl.BlockSpec(a.shape, lambda b: (0,) * a.ndim)
    out_sds = jax.ShapeDtypeStruct((B, NSA_GROUPS, nc, HEAD_DIM), BF16)
    out_spec = pl.BlockSpec((None, NSA_GROUPS, nc, HEAD_DIM), lambda b: (b, 0, 0, 0))
    return pl.pallas_call(
        _compress_kernel,
        grid=(B,),
        in_specs=[pl.BlockSpec((None, nc, W), lambda b: (b, 0, 0)),
                  full(pek), full(w1k), full(w2k), full(pev), full(w1v), full(w2v),
                  pl.BlockSpec((None, nc, HEAD_DIM), lambda b: (b, 0, 0)),
                  pl.BlockSpec((None, nc, HEAD_DIM), lambda b: (b, 0, 0))],
        out_specs=[out_spec, out_spec],
        out_shape=[out_sds, out_sds],
        compiler_params=_cparams(("arbitrary",)),
        name="compress",
    )(xc, pek, w1k, w2k, pev, w1v, w2v, cos_c, sin_c)


def _transpose_into(src_ref, dst_ref, n_rows):
    def body(b, _):
        r0 = pl.multiple_of(b * HEAD_DIM, HEAD_DIM)
        blk = src_ref[pl.ds(r0, HEAD_DIM), :].astype(F32)
        dst_ref[:, pl.ds(r0, HEAD_DIM)] = blk.T.astype(BF16)
        return 0
    lax.fori_loop(0, n_rows // HEAD_DIM, body, 0)


def _softmax_cols(s, mask):
    s = jnp.where(mask, s, NEG_INF)
    m = jnp.max(s, axis=0, keepdims=True)
    p = jnp.exp(s - m)
    l = jnp.sum(p, axis=0, keepdims=True)
    return jnp.where(mask, p * (1.0 / l), 0.0)


def _nsa_kernel(q_ref, kc_ref, vc_ref, ks_ref, vs_ref, kw_ref, vw_ref, gt_ref, c2st_ref, o_ref,
                vst_ref, vwt_ref, vct_ref, sel_ref, *, seq):
    tq = NSA_TQ
    R = NSA_REP
    i = pl.program_id(2)
    q0 = i * tq
    nc = kc_ref.shape[0]
    n_slc = sel_ref.shape[0]

    @pl.when(i == 0)
    def _():
        _transpose_into(vs_ref, vst_ref, seq)
        _transpose_into(vw_ref, vwt_ref, seq)
        _transpose_into(vc_ref, vct_ref, nc)

    q = q_ref[...]
    q4 = jnp.concatenate([q[:, r * HEAD_DIM:(r + 1) * HEAD_DIM] for r in range(R)], axis=0)
    lanes = R * tq

    def t_of(shape):
        return q0 + (lax.broadcasted_iota(I32, shape, 1) & (tq - 1))

    sc = _dot_nt(kc_ref[...], q4)
    n_idx = lax.broadcasted_iota(I32, (nc, lanes), 0)
    mask_c = (n_idx * CMP_STRIDE + (CMP_BLOCK - 1)) <= t_of((nc, lanes))
    p_c = _softmax_cols(sc, mask_c)
    o_cmp = _dot(vct_ref[...], p_c.astype(BF16))

    p_sum = p_c[:, 0:tq]
    for r in range(1, R):
        p_sum = p_sum + p_c[:, r * tq:(r + 1) * tq]
    p_hi, p_lo = _split_hi_lo(p_sum)
    imp = _dot(c2st_ref[...], p_hi) + _dot(c2st_ref[...], p_lo)
    jj = lax.broadcasted_iota(I32, (n_slc, tq), 0)
    cur = (q0 + lax.broadcasted_iota(I32, (n_slc, tq), 1)) // SEL_BLOCK
    valid = jj <= cur
    forced = (jj == 0) | (jj == cur) | (jj == cur - 1)
    score = jnp.where(forced, SEL_FORCED, jnp.where(valid, imp, -SEL_FORCED))
    sel = jnp.zeros((n_slc, tq), F32)
    for _, _, pick in _topk_rows(score, jj, n_slc, min(SEL_TOPN, n_slc)):
        sel = jnp.where(pick, 1.0, sel)
    sel_ref[...] = jnp.where(valid, sel, 0.0)

    blocks_per_tile = SEL_KT // SEL_BLOCK

    def sel_body(kt, carry):
        m, l, acc = carry
        k0 = pl.multiple_of(kt * SEL_KT, SEL_KT)
        s = _dot_nt(ks_ref[pl.ds(k0, SEL_KT), :], q4)
        rows = sel_ref[pl.ds(pl.multiple_of(kt * blocks_per_tile, blocks_per_tile), blocks_per_tile), :]
        m1 = jnp.concatenate([jnp.broadcast_to(rows[b:b + 1, :], (SEL_BLOCK, tq)) for b in range(blocks_per_tile)], axis=0)
        m4 = jnp.concatenate([m1] * R, axis=1)
        kpos = k0 + lax.broadcasted_iota(I32, (SEL_KT, lanes), 0)
        mask = (m4 > 0.5) & (kpos <= t_of((SEL_KT, lanes)))
        s = jnp.where(mask, s, NEG_INF)
        m_new = jnp.maximum(m, jnp.max(s, axis=0, keepdims=True))
        alpha = jnp.exp(m - m_new)
        p = jnp.exp(s - m_new)
        l = alpha * l + jnp.sum(p, axis=0, keepdims=True)
        acc = alpha * acc + _dot(vst_ref[:, pl.ds(k0, SEL_KT)], p.astype(BF16))
        return m_new, l, acc

    n_kt = (q0 + tq - 1) // SEL_KT + 1
    init = (jnp.full((1, lanes), NEG_INF, F32), jnp.zeros((1, lanes), F32), jnp.zeros((HEAD_DIM, lanes), F32))
    _, l_s, acc_s = lax.fori_loop(0, n_kt, sel_body, init)
    o_slc = acc_s * (1.0 / l_s)

    band = WINDOW + tq
    start = pl.multiple_of(jnp.maximum(q0 - WINDOW, 0), HEAD_DIM)
    sw = _dot_nt(kw_ref[pl.ds(start, band), :], q4)
    dpos = t_of((band, lanes)) - (start + lax.broadcasted_iota(I32, (band, lanes), 0))
    p_w = _softmax_cols(sw, (dpos >= 0) & (dpos < WINDOW))
    o_win = _dot(vwt_ref[:, pl.ds(start, band)], p_w.astype(BF16))

    gt = gt_ref[...]
    for r in range(R):
        sl = slice(r * tq, (r + 1) * tq)
        o_r = (gt[r:r + 1, :] * o_cmp[:, sl] + gt[R + r:R + r + 1, :] * o_slc[:, sl]
               + gt[2 * R + r:2 * R + r + 1, :] * o_win[:, sl])
        o_ref[:, r * HEAD_DIM:(r + 1) * HEAD_DIM] = o_r.T.astype(BF16)


def _nsa_attention(proj3, kc, vc, gates_t, c2st):
    B, S, _ = proj3.shape
    G = NSA_GROUPS
    nc = kc.shape[2]
    n_slc = c2st.shape[0]
    tq = NSA_TQ
    nq = S // tq
    hb = HEAD_DIM
    gw = NSA_REP * HEAD_DIM
    seq_spec = lambda off: pl.BlockSpec((None, S, hb), lambda b, g, i: (b, 0, off // hb + g))
    cmp_spec = pl.BlockSpec((None, None, nc, hb), lambda b, g, i: (b, g, 0, 0))
    return pl.pallas_call(
        functools.partial(_nsa_kernel, seq=S),
        grid=(B, G, nq),
        in_specs=[pl.BlockSpec((None, tq, gw), lambda b, g, i: (b, i, OFF_QA // gw + g)),
                  cmp_spec, cmp_spec,
                  seq_spec(OFF_KSLC), seq_spec(OFF_VSLC), seq_spec(OFF_KWIN), seq_spec(OFF_VWIN),
                  pl.BlockSpec((GATE_ROWS, tq), lambda b, g, i: (g, b * nq + i)),
                  pl.BlockSpec(c2st.shape, lambda b, g, i: (0, 0))],
        out_specs=pl.BlockSpec((None, tq, gw), lambda b, g, i: (b, i, g)),
        out_shape=jax.ShapeDtypeStruct((B, S, NSA_WIDTH), BF16),
        scratch_shapes=[pltpu.VMEM((hb, S), BF16), pltpu.VMEM((hb, S), BF16), pltpu.VMEM((hb, nc), BF16),
                        pltpu.VMEM((n_slc, tq), F32)],
        compiler_params=_cparams(("arbitrary", "arbitrary", "arbitrary")),
        name="nsa_attn",
    )(proj3, kc, vc, proj3, proj3, proj3, proj3, gates_t, c2st)


def _moba_kernel(q_ref, k_ref, v_ref, o_ref, vt_ref, kmh_ref, kml_ref, sel_ref, *, seq):
    blk = MOBA_BLOCK
    i = pl.program_id(2)
    nb = seq // blk
    nbp = sel_ref.shape[0]

    @pl.when(i == 0)
    def _():
        _transpose_into(v_ref, vt_ref, seq)
        means = [jnp.sum(k_ref[n * blk:(n + 1) * blk, :].astype(F32), axis=0, keepdims=True) * (1.0 / blk)
                 for n in range(nb)]
        if nbp > nb:
            means.append(jnp.zeros((nbp - nb, HEAD_DIM), F32))
        hi, lo = _split_hi_lo(jnp.concatenate(means, axis=0))
        kmh_ref[...] = hi
        kml_ref[...] = lo

    q = q_ref[...]
    gate = _dot_nt(kmh_ref[...], q) + _dot_nt(kml_ref[...], q)
    n_idx = lax.broadcasted_iota(I32, (nbp, blk), 0)
    past = n_idx < i
    sel = jnp.zeros((nbp, blk), F32)
    for _, _, pick in _topk_rows(jnp.where(past, gate, -SEL_FORCED), n_idx, nbp, min(MOBA_TOPK, nb)):
        sel = jnp.where(pick, 1.0, sel)
    sel_ref[...] = jnp.where(past, sel, 0.0)

    own0 = pl.multiple_of(i * blk, blk)
    s = _dot_nt(k_ref[pl.ds(own0, blk), :], q)
    causal = lax.broadcasted_iota(I32, (blk, blk), 0) <= lax.broadcasted_iota(I32, (blk, blk), 1)
    s = jnp.where(causal, s, NEG_INF)
    m = jnp.max(s, axis=0, keepdims=True)
    p = jnp.exp(s - m)
    l = jnp.sum(p, axis=0, keepdims=True)
    acc = _dot(vt_ref[:, pl.ds(own0, blk)], p.astype(BF16))

    def body(n, carry):
        m, l, acc = carry
        k0 = pl.multiple_of(n * blk, blk)
        s = _dot_nt(k_ref[pl.ds(k0, blk), :], q)
        s = jnp.where(sel_ref[pl.ds(n, 1), :] > 0.5, s, NEG_INF)
        m_new = jnp.maximum(m, jnp.max(s, axis=0, keepdims=True))
        alpha = jnp.exp(m - m_new)
        p = jnp.exp(s - m_new)
        l = alpha * l + jnp.sum(p, axis=0, keepdims=True)
        acc = alpha * acc + _dot(vt_ref[:, pl.ds(k0, blk)], p.astype(BF16))
        return m_new, l, acc

    _, l, acc = lax.fori_loop(0, i, body, (m, l, acc))
    o_ref[...] = (acc * (1.0 / l)).T.astype(BF16)


def _moba_attention(proj3):
    B, S, _ = proj3.shape
    H = MOBA_HEADS
    blk = MOBA_BLOCK
    hb = HEAD_DIM
    nbp = max(8, -(-(S // blk) // 8) * 8)
    seq_spec = lambda off: pl.BlockSpec((None, S, hb), lambda b, h, i: (b, 0, off // hb + h))
    return pl.pallas_call(
        functools.partial(_moba_kernel, seq=S),
        grid=(B, H, S // blk),
        in_specs=[pl.BlockSpec((None, blk, hb), lambda b, h, i: (b, i, OFF_QB // hb + h)),
                  seq_spec(OFF_KB), seq_spec(OFF_VB)],
        out_specs=pl.BlockSpec((None, blk, hb), lambda b, h, i: (b, i, h)),
        out_shape=jax.ShapeDtypeStruct((B, S, MOBA_WIDTH), BF16),
        scratch_shapes=[pltpu.VMEM((hb, S), BF16), pltpu.VMEM((nbp, hb), BF16), pltpu.VMEM((nbp, hb), BF16),
                        pltpu.VMEM((nbp, blk), F32)],
        compiler_params=_cparams(("arbitrary", "arbitrary", "arbitrary")),
        name="moba_attn",
    )(proj3, proj3, proj3)


def _merge_proj_kernel(oa_ref, ob_ref, wa_ref, wb_ref, ga_ref, gb_ref, o_ref):
    a = _dot(oa_ref[...], wa_ref[...])
    b = _dot(ob_ref[...], wb_ref[...])
    o_ref[...] = (ga_ref[...].astype(F32) * a + gb_ref[...].astype(F32) * b).astype(BF16)


def _merge_proj(oa, ob, wa, wb, proj, tm):
    T = oa.shape[0]
    D = wa.shape[1]
    tn = min(COL_TILE, D)
    g0 = PLAIN_END // tn
    return pl.pallas_call(
        _merge_proj_kernel,
        grid=(T // tm, D // tn),
        in_specs=[pl.BlockSpec((tm, NSA_WIDTH), lambda i, j: (i, 0)),
                  pl.BlockSpec((tm, MOBA_WIDTH), lambda i, j: (i, 0)),
                  pl.BlockSpec((NSA_WIDTH, tn), lambda i, j: (0, j)),
                  pl.BlockSpec((MOBA_WIDTH, tn), lambda i, j: (0, j)),
                  pl.BlockSpec((tm, tn), lambda i, j: (i, g0 + j)),
                  pl.BlockSpec((tm, tn), lambda i, j: (i, g0 + D // tn + j))],
        out_specs=pl.BlockSpec((tm, tn), lambda i, j: (i, j)),
        out_shape=jax.ShapeDtypeStruct((T, D), BF16),
        compiler_params=_cparams(("arbitrary", "arbitrary")),
        name="merge_proj",
    )(oa, ob, wa, wb, proj, proj)


def _out_router_kernel(x_ref, mix_ref, wo_ref, g_ref, wrh_ref, wrl_ref, br_ref, x1_ref, h2_ref, te_ref, tw_ref):
    tm = x_ref.shape[0]
    E = wrh_ref.shape[0]
    x1 = x_ref[...] + _dot(mix_ref[...], wo_ref[...])
    x1_ref[...] = x1
    h2 = x1 * lax.rsqrt(jnp.mean(x1 * x1, axis=-1, keepdims=True) + NORM_EPS) * g_ref[...]
    h2_ref[...] = h2
    h_hi, h_lo = _split_hi_lo(h2)
    logits = (_dot_nt(wrh_ref[...], h_hi) + _dot_nt(wrh_ref[...], h_lo) + _dot_nt(wrl_ref[...], h_hi)
              + br_ref[...])
    e_idx = lax.broadcasted_iota(I32, (E, tm), 0)
    picks = _topk_rows(logits, e_idx, E, TOP_K)
    vals = [p[0] for p in picks]
    exps = [jnp.exp(v - vals[0]) for v in vals]
    inv = 1.0 / (exps[0] + exps[1] + exps[2] + exps[3])
    te_ref[...] = jnp.concatenate([p[1] for p in picks] + [jnp.zeros((8 - TOP_K, tm), I32)], axis=0)
    w_t = jnp.concatenate([e * inv for e in exps] + [jnp.zeros((HEAD_DIM - TOP_K, tm), F32)], axis=0)
    tw_ref[...] = w_t.T


def _out_router(x2, mix, wo, g, wrh, wrl, br, tm):
    T, D = x2.shape
    E = wrh.shape[0]
    full = lambda a: pl.BlockSpec(a.shape, lambda i: (0,) * a.ndim)
    return pl.pallas_call(
        _out_router_kernel,
        grid=(T // tm,),
        in_specs=[pl.BlockSpec((tm, D), lambda i: (i, 0)), pl.BlockSpec((tm, D), lambda i: (i, 0)),
                  full(wo), full(g), full(wrh), full(wrl), full(br)],
        out_specs=[pl.BlockSpec((tm, D), lambda i: (i, 0)), pl.BlockSpec((tm, D), lambda i: (i, 0)),
                   pl.BlockSpec((8, tm), lambda i: (0, i)), pl.BlockSpec((tm, HEAD_DIM), lambda i: (i, 0))],
        out_shape=[jax.ShapeDtypeStruct((T, D), F32), jax.ShapeDtypeStruct((T, D), F32),
                   jax.ShapeDtypeStruct((8, T), I32), jax.ShapeDtypeStruct((T, HEAD_DIM), F32)],
        compiler_params=_cparams(("arbitrary",)),
        name="out_router",
    )(x2, mix, wo, g, wrh, wrl, br)


def _route_kernel(te_ref, dest_ref, blke_ref, run_ref, pstart_ref):
    ph = pl.program_id(0)
    i = pl.program_id(1)
    E = run_ref.shape[0]
    tm = te_ref.shape[1]
    nbp = blke_ref.shape[1]
    te = te_ref[...]
    e_idx = lax.broadcasted_iota(I32, (E, tm), 0)
    ohs = [te[k:k + 1, :] == e_idx for k in range(TOP_K)]
    oh = jnp.where(ohs[0] | ohs[1] | ohs[2] | ohs[3], 1.0, 0.0)
    tile_cnt = jnp.sum(oh, axis=1, keepdims=True)

    @pl.when((ph == 0) & (i == 0))
    def _():
        run_ref[...] = jnp.zeros(run_ref.shape, F32)

    @pl.when(ph == 0)
    def _():
        run_ref[...] = run_ref[...] + tile_cnt
        dest_ref[...] = jnp.zeros(dest_ref.shape, I32)

    @pl.when((ph == 1) & (i == 0))
    def _():
        counts = run_ref[...]
        padded = jnp.floor((counts + (ROW_BLOCK - 1)) * (1.0 / ROW_BLOCK)) * ROW_BLOCK
        row = lax.broadcasted_iota(I32, counts.shape, 0)
        incl = padded
        sh = 1
        while sh < E:
            incl = incl + jnp.where(row >= sh, pltpu.roll(incl, sh, 0), 0.0)
            sh *= 2
        pstart_ref[...] = incl - padded
        blk_start = (lax.broadcasted_iota(I32, (E, nbp), 1) * ROW_BLOCK).astype(F32)
        blke_ref[...] = jnp.sum(jnp.where(incl[:, 0:1] <= blk_start, 1, 0), axis=0, keepdims=True).astype(I32)
        run_ref[...] = jnp.zeros(run_ref.shape, F32)

    @pl.when(ph == 1)
    def _():
        upper = jnp.where(lax.broadcasted_iota(I32, (tm, tm), 0) < lax.broadcasted_iota(I32, (tm, tm), 1), 1.0, 0.0)
        before = _dot(oh.astype(BF16), upper.astype(BF16))
        val = before + pstart_ref[:, 0:1] + run_ref[:, 0:1]
        rows = [jnp.sum(jnp.where(ohs[k], val, 0.0), axis=0, keepdims=True) for k in range(TOP_K)]
        dest_ref[...] = jnp.concatenate(rows + [jnp.zeros((8 - TOP_K, tm), F32)], axis=0).astype(I32)
        run_ref[...] = run_ref[...] + tile_cnt


def _route(top_e, n_experts, n_blk_pad, tm):
    T = top_e.shape[1]
    return pl.pallas_call(
        _route_kernel,
        grid=(2, T // tm),
        in_specs=[pl.BlockSpec((8, tm), lambda ph, i: (0, i))],
        out_specs=[pl.BlockSpec((8, tm), lambda ph, i: (0, i * ph)),
                   pl.BlockSpec((1, n_blk_pad), lambda ph, i: (0, 0))],
        out_shape=[jax.ShapeDtypeStruct((8, T), I32), jax.ShapeDtypeStruct((1, n_blk_pad), I32)],
        scratch_shapes=[pltpu.VMEM((n_experts, HEAD_DIM), F32), pltpu.VMEM((n_experts, HEAD_DIM), F32)],
        compiler_params=_cparams(("arbitrary", "arbitrary")),
        name="route",
    )(top_e)


def _row_copy(src_ref, src_row, dst_ref, dst_row, sem):
    return pltpu.make_async_copy(src_ref.at[pl.ds(src_row, 1), :], dst_ref.at[pl.ds(dst_row, 1), :], sem)


def _dispatch_kernel(dest_ref, h_ref, xz_ref, xpad_ref, sem):
    del xz_ref
    tm = h_ref.shape[0]

    def issue(t, _):
        for k in range(TOP_K):
            _row_copy(h_ref, t, xpad_ref, dest_ref[k, t], sem).start()
        return 0
    lax.fori_loop(0, tm, issue, 0)

    def drain(t, _):
        for k in range(TOP_K):
            _row_copy(h_ref, 0, xpad_ref, 0, sem).wait()
        return 0
    lax.fori_loop(0, tm, drain, 0)


def _dispatch(dest, h2, x_zero, tm):
    T, D = h2.shape
    return pl.pallas_call(
        _dispatch_kernel,
        grid=(T // tm,),
        in_specs=[pl.BlockSpec((8, tm), lambda i: (0, i), memory_space=pltpu.SMEM),
                  pl.BlockSpec((tm, D), lambda i: (i, 0)),
                  pl.BlockSpec(memory_space=pl.ANY)],
        out_specs=pl.BlockSpec(memory_space=pl.ANY),
        out_shape=jax.ShapeDtypeStruct(x_zero.shape, x_zero.dtype),
        scratch_shapes=[pltpu.SemaphoreType.DMA(())],
        input_output_aliases={2: 0},
        compiler_params=_cparams(("arbitrary",)),
        name="dispatch",
    )(dest, h2, x_zero)


def _expert_changed(be_ref, m):
    prev = be_ref[jnp.maximum(m - 1, 0)]
    return (m == 0) | (be_ref[m] != prev)


def _expert_up_kernel(be_ref, x_ref, wg_ref, wl_ref, bg_ref, bl_ref, act_ref, wgb_ref, wlb_ref, *, n_experts):
    m = pl.program_id(1)

    @pl.when(_expert_changed(be_ref, m))
    def _():
        wgb_ref[...] = wg_ref[...].astype(BF16)
        wlb_ref[...] = wl_ref[...].astype(BF16)

    @pl.when(be_ref[m] < n_experts)
    def _():
        x = x_ref[...].astype(BF16)
        gate = jnp.minimum(_dot(x, wgb_ref[...]) + bg_ref[...], SWIGLU_LIMIT)
        lin = jnp.clip(_dot(x, wlb_ref[...]) + bl_ref[...], -SWIGLU_LIMIT, SWIGLU_LIMIT)
        act_ref[...] = (gate * _sigmoid(SWIGLU_ALPHA * gate) * (lin + 1.0)).astype(BF16)

    @pl.when(be_ref[m] >= n_experts)
    def _():
        act_ref[...] = jnp.zeros(act_ref.shape, BF16)


def _expert_up(blk_e, x_pad, w_gu, b_gu, n_blk, tn):
    P, D = x_pad.shape
    E = w_gu.shape[0]
    F = w_gu.shape[2] // 2
    nf = F // tn
    ex = lambda be, m: jnp.minimum(be[m], E - 1)
    grid_spec = pltpu.PrefetchScalarGridSpec(
        num_scalar_prefetch=1,
        grid=(nf, n_blk),
        in_specs=[pl.BlockSpec((ROW_BLOCK, D), lambda n, m, be: (m, 0)),
                  pl.BlockSpec((None, D, tn), lambda n, m, be: (ex(be, m), 0, n)),
                  pl.BlockSpec((None, D, tn), lambda n, m, be: (ex(be, m), 0, nf + n)),
                  pl.BlockSpec((None, 1, tn), lambda n, m, be: (ex(be, m), 0, n)),
                  pl.BlockSpec((None, 1, tn), lambda n, m, be: (ex(be, m), 0, nf + n))],
        out_specs=pl.BlockSpec((ROW_BLOCK, tn), lambda n, m, be: (m, n)),
        scratch_shapes=[pltpu.VMEM((D, tn), BF16), pltpu.VMEM((D, tn), BF16)])
    return pl.pallas_call(
        functools.partial(_expert_up_kernel, n_experts=E),
        grid_spec=grid_spec,
        out_shape=jax.ShapeDtypeStruct((P, F), BF16),
        compiler_params=_cparams(("arbitrary", "arbitrary")),
        name="expert_up",
    )(blk_e, x_pad, w_gu, w_gu, b_gu, b_gu)


def _expert_down_kernel(be_ref, a_ref, w_ref, b_ref, y_ref, wb_ref, *, n_experts):
    m = pl.program_id(1)

    @pl.when(_expert_changed(be_ref, m))
    def _():
        wb_ref[...] = w_ref[...].astype(BF16)

    @pl.when(be_ref[m] < n_experts)
    def _():
        y_ref[...] = _dot(a_ref[...], wb_ref[...]) + b_ref[...]

    @pl.when(be_ref[m] >= n_experts)
    def _():
        y_ref[...] = jnp.zeros(y_ref.shape, F32)


def _expert_down(blk_e, act, w_d, b_d, n_blk, tn):
    P, F = act.shape
    E, _, D = w_d.shape
    ex = lambda be, m: jnp.minimum(be[m], E - 1)
    grid_spec = pltpu.PrefetchScalarGridSpec(
        num_scalar_prefetch=1,
        grid=(D // tn, n_blk),
        in_specs=[pl.BlockSpec((ROW_BLOCK, F), lambda n, m, be: (m, 0)),
                  pl.BlockSpec((None, F, tn), lambda n, m, be: (ex(be, m), 0, n)),
                  pl.BlockSpec((None, 1, tn), lambda n, m, be: (ex(be, m), 0, n))],
        out_specs=pl.BlockSpec((ROW_BLOCK, tn), lambda n, m, be: (m, n)),
        scratch_shapes=[pltpu.VMEM((F, tn), BF16)])
    return pl.pallas_call(
        functools.partial(_expert_down_kernel, n_experts=E),
        grid_spec=grid_spec,
        out_shape=jax.ShapeDtypeStruct((P, D), F32),
        compiler_params=_cparams(("arbitrary", "arbitrary")),
        name="expert_down",
    )(blk_e, act, w_d, b_d)


def _combine_kernel(dest_ref, x1_ref, tw_ref, g_ref, ypad_ref, o_ref, ybuf_ref, sem):
    tm = x1_ref.shape[0]

    def issue(t, _):
        for k in range(TOP_K):
            _row_copy(ypad_ref, dest_ref[k, t], ybuf_ref.at[k], t, sem).start()
        return 0
    lax.fori_loop(0, tm, issue, 0)

    def drain(t, _):
        for k in range(TOP_K):
            _row_copy(ypad_ref, 0, ybuf_ref.at[k], 0, sem).wait()
        return 0
    lax.fori_loop(0, tm, drain, 0)

    tw = tw_ref[...]
    acc = x1_ref[...]
    for k in range(TOP_K):
        acc = acc + tw[:, k:k + 1] * ybuf_ref[k]
    o_ref[...] = acc * lax.rsqrt(jnp.mean(acc * acc, axis=-1, keepdims=True) + NORM_EPS) * g_ref[...]


def _combine(dest, x1, top_w, g, y_pad, tm):
    T, D = x1.shape
    return pl.pallas_call(
        _combine_kernel,
        grid=(T // tm,),
        in_specs=[pl.BlockSpec((8, tm), lambda i: (0, i), memory_space=pltpu.SMEM),
                  pl.BlockSpec((tm, D), lambda i: (i, 0)),
                  pl.BlockSpec((tm, HEAD_DIM), lambda i: (i, 0)),
                  pl.BlockSpec((1, D), lambda i: (0, 0)),
                  pl.BlockSpec(memory_space=pl.ANY)],
        out_specs=pl.BlockSpec((tm, D), lambda i: (i, 0)),
        out_shape=jax.ShapeDtypeStruct((T, D), F32),
        scratch_shapes=[pltpu.VMEM((TOP_K, tm, D), F32), pltpu.SemaphoreType.DMA(())],
        compiler_params=_cparams(("arbitrary",)),
        name="combine",
    )(dest, x1, top_w, g, y_pad)


def _rope_tables(pos):
    inv_freq = ROPE_THETA ** (-jnp.arange(0, ROPE_DIM, 2, dtype=F32) / ROPE_DIM)
    ang = pos.astype(F32)[..., None] * inv_freq
    cos, sin = jnp.cos(ang), jnp.sin(ang)
    rest = HEAD_DIM - ROPE_DIM
    cos_t = jnp.concatenate([cos, cos, jnp.ones(cos.shape[:-1] + (rest,), F32)], axis=-1)
    sin_t = jnp.concatenate([-sin, sin, jnp.zeros(sin.shape[:-1] + (rest,), F32)], axis=-1)
    return cos_t, sin_t


def _cmp_to_slc_t(seq, nc_pad):
    n_cmp = (seq - CMP_BLOCK) // CMP_STRIDE + 1
    n_slc = seq // SEL_BLOCK
    cmp_start = np.arange(n_cmp) * CMP_STRIDE
    slc_start = np.arange(n_slc) * SEL_BLOCK
    overlap = np.clip(np.minimum(cmp_start[:, None] + CMP_BLOCK, slc_start[None, :] + SEL_BLOCK)
                      - np.maximum(cmp_start[:, None], slc_start[None, :]), 0, None)
    out = np.zeros((n_slc, nc_pad), np.float32)
    out[:, :n_cmp] = (overlap / CMP_BLOCK).T
    return jnp.asarray(out, BF16)


def _regroup_w_in(w):
    D = w.shape[0]
    o_kv = NSA_WIDTH
    o_gn = o_kv + 6 * KV_WIDTH
    o_b = o_gn + 3 * NSA_HEADS
    o_gm = o_b + 3 * MOBA_WIDTH
    kv = lambda s: w[:, o_kv + s * KV_WIDTH:o_kv + (s + 1) * KV_WIDTH]
    mb = lambda s: w[:, o_b + s * MOBA_WIDTH:o_b + (s + 1) * MOBA_WIDTH]
    cols = [w[:, :NSA_WIDTH], kv(2), kv(4), mb(0), mb(1), kv(0), kv(1), kv(3), kv(5), mb(2), w[:, o_gm:o_gm + 2 * D]]
    w_main = jnp.concatenate(cols, axis=1).astype(BF16)
    wg = w[:, o_gn:o_gn + 3 * NSA_HEADS]
    wg = wg.reshape(D, 3, NSA_GROUPS, NSA_REP).transpose(2, 1, 3, 0).reshape(NSA_GROUPS, 3 * NSA_REP, D)
    wg = jnp.pad(wg, ((0, 0), (0, GATE_ROWS - 3 * NSA_REP), (0, 0))).reshape(NSA_GROUPS * GATE_ROWS, D)
    return w_main, wg.astype(BF16)


def kernel(x, positions, g_attn_norm, w_in, pe_cmp_k, w_cmp_k1, w_cmp_k2, pe_cmp_v, w_cmp_v1, w_cmp_v2, w_proj_nsa, w_proj_moba, w_out, g_ffn_norm, w_router, b_router, w_gate_up, b_gate_up, w_down, b_down, g_final_norm):
    B, S, D = x.shape
    T = B * S
    E = w_router.shape[-1]
    F = w_down.shape[-2]
    depth = w_in.shape[0]
    assert depth == 1, "the final norm is fused into the single layer's combine step"
    assert S % MOBA_BLOCK == 0 and S >= WINDOW + NSA_TQ and (2 * D) % COL_TILE == 0 and T % 512 == 0
    nc = S // CMP_STRIDE
    tm_big = min(1024, T)

    cos_t, sin_t = _rope_tables(positions)
    cmp_end = np.arange(nc) * CMP_STRIDE + CMP_BLOCK - 1
    cmp_end = np.minimum(cmp_end, S - 1)
    cos_c, sin_c = cos_t[:, cmp_end], sin_t[:, cmp_end]
    c2st = _cmp_to_slc_t(S, nc)

    x2 = x.reshape(T, D)
    for layer in range(depth):
        w_main, w_gates_t = _regroup_w_in(w_in[layer])
        h, gates_t = _norm_gates(x2, g_attn_norm[layer].reshape(1, D), w_gates_t, tm_big)
        proj = _in_proj(h, w_main, cos_t.reshape(T, HEAD_DIM), sin_t.reshape(T, HEAD_DIM), tm_big)
        proj3 = proj.reshape(B, S, proj.shape[1])

        xc = proj3[:, :, OFF_KCMP:OFF_KCMP + 2 * KV_WIDTH].reshape(B, nc, CMP_STRIDE * 2 * KV_WIDTH)
        kc, vc = _compress(xc, pe_cmp_k[layer], w_cmp_k1[layer].astype(BF16), w_cmp_k2[layer].astype(BF16),
                           pe_cmp_v[layer], w_cmp_v1[layer].astype(BF16), w_cmp_v2[layer].astype(BF16), cos_c, sin_c)
        o_a = _nsa_attention(proj3, kc, vc, gates_t, c2st)
        o_b = _moba_attention(proj3)

        mix = _merge_proj(o_a.reshape(T, NSA_WIDTH), o_b.reshape(T, MOBA_WIDTH),
                          w_proj_nsa[layer].astype(BF16), w_proj_moba[layer].astype(BF16), proj, tm_big)
        wr_t = w_router[layer].T
        wr_hi = wr_t.astype(BF16)
        wr_lo = (wr_t - wr_hi.astype(F32)).astype(BF16)
        x1, h2, top_e, top_w = _out_router(x2, mix, w_out[layer].astype(BF16), g_ffn_norm[layer].reshape(1, D),
                                           wr_hi, wr_lo, b_router[layer].reshape(E, 1), 256)

        n_blk = (T * TOP_K) // ROW_BLOCK + E
        n_blk_pad = -(-n_blk // HEAD_DIM) * HEAD_DIM
        dest, blk_e = _route(top_e, E, n_blk_pad, 512)
        x_pad = _dispatch(dest, h2, jnp.zeros((n_blk * ROW_BLOCK, D), F32), 512)
        blk_e = blk_e.reshape(n_blk_pad)
        act = _expert_up(blk_e, x_pad, w_gate_up[layer], b_gate_up[layer].reshape(E, 1, 2 * F), n_blk, min(512, F))
        y_pad = _expert_down(blk_e, act, w_down[layer], b_down[layer].reshape(E, 1, D), n_blk, min(512, D))
        x2 = _combine(dest, x1, top_w, g_final_norm.reshape(1, D), y_pad, 256)
    return x2.reshape(B, S, D)
```

```python
import functools

import jax
import jax.numpy as jnp
import numpy as np
from jax import lax
from jax.experimental import pallas as pl
from jax.experimental.pallas import tpu as pltpu

F32 = jnp.float32
BF16 = jnp.bfloat16
I32 = jnp.int32

HEAD_DIM = 128
ROPE_DIM = HEAD_DIM // 4
ROPE_HALF = ROPE_DIM // 2
ROPE_THETA = 500000.0
NORM_EPS = 1e-5
NEG_INF = -1e30
REMOVED = -3e38
SEL_FORCED = 1e9

NSA_HEADS = 8
NSA_GROUPS = 2
NSA_REP = NSA_HEADS // NSA_GROUPS
CMP_BLOCK = 32
CMP_STRIDE = 16
CMP_HIDDEN = 256
SEL_BLOCK = 64
SEL_TOPN = 16
WINDOW = 512
MOBA_HEADS = 8
MOBA_BLOCK = 256
MOBA_TOPK = 3
TOP_K = 4
SWIGLU_LIMIT = 7.0
SWIGLU_ALPHA = 1.702
ROW_BLOCK = 256

NSA_WIDTH = NSA_HEADS * HEAD_DIM
MOBA_WIDTH = MOBA_HEADS * HEAD_DIM
KV_WIDTH = NSA_GROUPS * HEAD_DIM
SCALE = HEAD_DIM ** -0.5
Q_SCALE = SCALE * 1.4426950408889634

COL_TILE = 512
OFF_QA = 0
OFF_KSLC = NSA_WIDTH
OFF_KWIN = OFF_KSLC + KV_WIDTH
OFF_QB = OFF_KWIN + KV_WIDTH
OFF_KB = OFF_QB + MOBA_WIDTH
ROPE_COLS = OFF_KB + MOBA_WIDTH
OFF_KCMP = ROPE_COLS
OFF_VCMP = OFF_KCMP + KV_WIDTH
OFF_VSLC = OFF_VCMP + KV_WIDTH
OFF_VWIN = OFF_VSLC + KV_WIDTH
OFF_VB = OFF_VWIN + KV_WIDTH
PLAIN_END = OFF_VB + MOBA_WIDTH
N_ROPE_TILES = ROPE_COLS // COL_TILE
N_PLAIN_TILES = (PLAIN_END - ROPE_COLS) // COL_TILE
Q_TILES = tuple(range(OFF_QA // COL_TILE, OFF_KSLC // COL_TILE)) + tuple(range(OFF_QB // COL_TILE, OFF_KB // COL_TILE))

VMEM_LIMIT = 56 * 1024 * 1024
NSA_TQ = 128
SEL_KT = 512
GATE_ROWS = 16
IN_PROJ_CHUNK = 256
NSA_SUB = 2
MOBA_HEADS_PER_STEP = 4


def _cparams(sem):
    return pltpu.CompilerParams(dimension_semantics=sem, vmem_limit_bytes=VMEM_LIMIT)


def _sigmoid(z):
    return 1.0 / (1.0 + jnp.exp(-z))


def _dot(a, b):
    return jnp.dot(a, b, preferred_element_type=F32)


def _dot_nt(a, b):
    return lax.dot_general(a, b, (((1,), (1,)), ((), ())), preferred_element_type=F32)


def _split_hi_lo(v):
    hi = v.astype(BF16)
    lo = (v - hi.astype(F32)).astype(BF16)
    return hi, lo


def _topk_rows(work, row_idx, n_rows, k):
    picks = []
    for _ in range(k):
        m = jnp.max(work, axis=0, keepdims=True)
        first = jnp.min(jnp.where(work == m, row_idx, n_rows), axis=0, keepdims=True)
        pick = row_idx == first
        picks.append((m, first, pick))
        work = jnp.where(pick, REMOVED, work)
    return picks


def _norm_gates_kernel(x_ref, g_ref, wgt_ref, h_ref, gt_ref):
    x = x_ref[...]
    h = x * lax.rsqrt(jnp.mean(x * x, axis=-1, keepdims=True) + NORM_EPS) * g_ref[...]
    hb = h.astype(BF16)
    h_ref[...] = hb
    gt_ref[...] = _sigmoid(_dot_nt(wgt_ref[...], hb))


def _norm_gates(x2, g, wgt, tm):
    T, D = x2.shape
    R = wgt.shape[0]
    return pl.pallas_call(
        _norm_gates_kernel,
        grid=(T // tm,),
        in_specs=[pl.BlockSpec((tm, D), lambda i: (i, 0)),
                  pl.BlockSpec((1, D), lambda i: (0, 0)),
                  pl.BlockSpec((R, D), lambda i: (0, 0))],
        out_specs=[pl.BlockSpec((tm, D), lambda i: (i, 0)),
                   pl.BlockSpec((R, tm), lambda i: (0, i))],
        out_shape=[jax.ShapeDtypeStruct((T, D), BF16), jax.ShapeDtypeStruct((R, T), F32)],
        compiler_params=_cparams(("arbitrary",)),
        name="norm_gates",
    )(x2, g, wgt)


def _rope_tile(xh, c, s, lane):
    rot = jnp.where(lane < ROPE_HALF, pltpu.roll(xh, HEAD_DIM - ROPE_HALF, 1), pltpu.roll(xh, ROPE_HALF, 1))
    return xh * c + rot * s


def _in_proj_kernel(h_ref, w_ref, c_ref, s_ref, o_ref):
    j = pl.program_id(1)
    tm = h_ref.shape[0]
    chunk = min(IN_PROJ_CHUNK, tm)

    def by_chunks(epilogue):
        for r0 in range(0, tm, chunk):
            rows = slice(r0, r0 + chunk)
            epilogue(rows, _dot(h_ref[rows, :], w_ref[...]))

    @pl.when(j < N_ROPE_TILES)
    def _():
        is_q = j == Q_TILES[0]
        for qt in Q_TILES[1:]:
            is_q = is_q | (j == qt)
        f = jnp.where(is_q, Q_SCALE, 1.0).astype(F32)
        lane = lax.broadcasted_iota(I32, (chunk, HEAD_DIM), 1)

        def rope(rows, acc):
            c = c_ref[rows, :] * f
            s = s_ref[rows, :] * f
            for hh in range(COL_TILE // HEAD_DIM):
                sl = slice(hh * HEAD_DIM, (hh + 1) * HEAD_DIM)
                o_ref[rows, sl] = _rope_tile(acc[:, sl], c, s, lane).astype(BF16)
        by_chunks(rope)

    @pl.when((j >= N_ROPE_TILES) & (j < N_ROPE_TILES + N_PLAIN_TILES))
    def _():
        def plain(rows, acc):
            o_ref[rows, :] = acc.astype(BF16)
        by_chunks(plain)

    @pl.when(j >= N_ROPE_TILES + N_PLAIN_TILES)
    def _():
        def gate(rows, acc):
            o_ref[rows, :] = _sigmoid(acc).astype(BF16)
        by_chunks(gate)


def _in_proj(h, w, cos_t, sin_t, tm):
    T, D = h.shape
    N = w.shape[1]
    return pl.pallas_call(
        _in_proj_kernel,
        grid=(T // tm, N // COL_TILE),
        in_specs=[pl.BlockSpec((tm, D), lambda i, j: (i, 0)),
                  pl.BlockSpec((D, COL_TILE), lambda i, j: (0, j)),
                  pl.BlockSpec((tm, HEAD_DIM), lambda i, j: (i, 0)),
                  pl.BlockSpec((tm, HEAD_DIM), lambda i, j: (i, 0))],
        out_specs=pl.BlockSpec((tm, COL_TILE), lambda i, j: (i, j)),
        out_shape=jax.ShapeDtypeStruct((T, N), BF16),
        compiler_params=_cparams(("arbitrary", "arbitrary")),
        name="in_proj",
    )(h, w, cos_t, sin_t)


def _gelu_tanh(x):
    return 0.5 * x * (1.0 + jnp.tanh(0.7978845608028654 * (x + 0.044715 * x * x * x)))


def _compress_kernel(x_ref, pek_ref, w1k_ref, w2k_ref, pev_ref, w1v_ref, w2v_ref, c_ref, s_ref, kc_ref, vc_ref):
    nc = x_ref.shape[0]
    half = CMP_STRIDE * HEAD_DIM
    tok_w = 2 * KV_WIDTH
    for which, (pe_ref, w1_ref, w2_ref, out_ref) in enumerate(
            ((pek_ref, w1k_ref, w2k_ref, kc_ref), (pev_ref, w1v_ref, w2v_ref, vc_ref))):
        for g in range(NSA_GROUPS):
            acc_a = jnp.zeros((nc, CMP_HIDDEN), F32)
            acc_b = jnp.zeros((nc, CMP_HIDDEN), F32)
            for l in range(CMP_STRIDE):
                off = l * tok_w + which * KV_WIDTH + g * HEAD_DIM
                xl = x_ref[:, off:off + HEAD_DIM].astype(F32)
                xa = (xl + pe_ref[l:l + 1, :]).astype(BF16)
                xb = (xl + pe_ref[CMP_STRIDE + l:CMP_STRIDE + l + 1, :]).astype(BF16)
                acc_a = acc_a + _dot(xa, w1_ref[l * HEAD_DIM:(l + 1) * HEAD_DIM, :])
                acc_b = acc_b + _dot(xb, w1_ref[half + l * HEAD_DIM:half + (l + 1) * HEAD_DIM, :])
            hid = _gelu_tanh(acc_a + pltpu.roll(acc_b, nc - 1, 0))
            out = _dot(hid.astype(BF16), w2_ref[...])
            if which == 0:
                lane = lax.broadcasted_iota(I32, out.shape, 1)
                out = _rope_tile(out, c_ref[...], s_ref[...], lane)
            out_ref[g] = out.astype(BF16)


def _compress(xc, pek, w1k, w2k, pev, w1v, w2v, cos_c, sin_c):
    B, nc, W = xc.shape
    full = lambda a: pl.BlockSpec(a.shape, lambda b: (0,) * a.ndim)
    out_sds = jax.ShapeDtypeStruct((B, NSA_GROUPS, nc, HEAD_DIM), BF16)
    out_spec = pl.BlockSpec((None, NSA_GROUPS, nc, HEAD_DIM), lambda b: (b, 0, 0, 0))
    return pl.pallas_call(
        _compress_kernel,
        grid=(B,),
        in_specs=[pl.BlockSpec((None, nc, W), lambda b: (b, 0, 0)),
                  full(pek), full(w1k), full(w2k), full(pev), full(w1v), full(w2v),
                  pl.BlockSpec((None, nc, HEAD_DIM), lambda b: (b, 0, 0)),
                  pl.BlockSpec((None, nc, HEAD_DIM), lambda b: (b, 0, 0))],
        out_specs=[out_spec, out_spec],
        out_shape=[out_sds, out_sds],
        compiler_params=_cparams(("arbitrary",)),
        name="compress",
    )(xc, pek, w1k, w2k, pev, w1v, w2v, cos_c, sin_c)


def _transpose_into(src_ref, col0, dst_ref, n_rows):
    def body(b, _):
        r0 = pl.multiple_of(b * HEAD_DIM, HEAD_DIM)
        blk = src_ref[pl.ds(r0, HEAD_DIM), col0:col0 + HEAD_DIM].astype(F32)
        dst_ref[:, pl.ds(r0, HEAD_DIM)] = blk.T.astype(BF16)
        return 0
    lax.fori_loop(0, n_rows // HEAD_DIM, body, 0)


def _topk_mask(score, k):
    n, L = score.shape
    sub = 8
    assert n % sub == 0
    groups = n // sub
    segs = [score[sub * g:sub * (g + 1), :] for g in range(groups)]
    sub_idx = lax.broadcasted_iota(I32, (sub, L), 0)
    later = [jnp.where(sub_idx > r, 1.0, 0.0) for r in range(sub)]
    ranks = [jnp.zeros((sub, L), F32) for _ in range(groups)]
    for i in range(n):
        gi, ri = divmod(i, sub)
        row = score[i:i + 1, :]
        for g in range(groups):
            if g > gi:
                inc = jnp.where(row >= segs[g], 1.0, 0.0)
            elif g < gi:
                inc = jnp.where(row > segs[g], 1.0, 0.0)
            else:
                inc = jnp.where(row > segs[g], 1.0, jnp.where(row == segs[g], later[ri], 0.0))
            ranks[g] = ranks[g] + inc
    return jnp.concatenate(ranks, axis=0) < k


def _bias_rows_to_cols(sel, lanes_out=HEAD_DIM):
    n, L = sel.shape
    bias_t = jnp.where(sel, 0.0, NEG_INF)
    if n < lanes_out:
        bias_t = jnp.concatenate([bias_t, jnp.zeros((lanes_out - n, L), F32)], axis=0)
    return bias_t.T.astype(BF16)


def _softmax_cols(s, live=None):
    m = jnp.max(s, axis=0, keepdims=True)
    p = jnp.exp2(s - m)
    inv = 1.0 / jnp.sum(p, axis=0, keepdims=True)
    if live is not None:
        inv = jnp.where(live, inv, 0.0)
    return p * inv


def _online_update(s, vt, carry):
    m, l, acc = carry
    m_new = jnp.maximum(m, jnp.max(s, axis=0, keepdims=True))
    alpha = jnp.exp2(m - m_new)
    p = jnp.exp2(s - m_new)
    l = alpha * l + jnp.sum(p, axis=0, keepdims=True)
    acc = alpha * acc + _dot(vt, p.astype(BF16))
    return m_new, l, acc


def _nsa_kernel(q_ref, kc_ref, vc_ref, ks_ref, vs_ref, kw_ref, vw_ref, gt_ref, c2st_ref, e_ref, o_ref,
                ksa_ref, vst_ref, vwt_ref, vct_ref, *, seq):
    tq = NSA_TQ
    U = NSA_SUB
    R = NSA_REP
    i = pl.program_id(2)
    base = i * (U * tq)
    nc = kc_ref.shape[0]
    n_slc = c2st_ref.shape[0]
    lanes = R * tq
    band = WINDOW + tq
    tile_r = lambda a: jnp.concatenate([a] * R, axis=1)
    rowi = lambda n: lax.broadcasted_iota(I32, (n, tq), 0)

    @pl.when(i == 0)
    def _():
        _transpose_into(vs_ref, 0, vst_ref, seq)
        _transpose_into(vw_ref, 0, vwt_ref, seq)
        _transpose_into(vc_ref, 0, vct_ref, nc)
        ksa_ref[:, 0:HEAD_DIM] = ks_ref[...]
        ksa_ref[:, HEAD_DIM:2 * HEAD_DIM] = e_ref[...]

    q4a, tcols, o_cmp, o_win = [], [], [], []
    for u in range(U):
        q0 = base + u * tq
        q = q_ref[u * tq:(u + 1) * tq, :]
        heads = [q[:, r * HEAD_DIM:(r + 1) * HEAD_DIM] for r in range(R)]
        q4 = jnp.concatenate(heads, axis=0)
        tcol = lambda n, q0=q0: q0 + lax.broadcasted_iota(I32, (n, tq), 1)
        tcols.append(tcol)

        bias_c = jnp.where(rowi(nc) * CMP_STRIDE + (CMP_BLOCK - 1) <= tcol(nc), 0.0, NEG_INF)
        live_c = tcol(1) >= CMP_BLOCK - 1
        p_c = _softmax_cols(_dot_nt(kc_ref[...], q4) + tile_r(bias_c), tile_r(live_c))
        o_cmp.append(_dot(vct_ref[...], p_c.astype(BF16)))

        p_sum = p_c[:, 0:tq]
        for r in range(1, R):
            p_sum = p_sum + p_c[:, r * tq:(r + 1) * tq]
        p_hi, p_lo = _split_hi_lo(p_sum)
        imp = _dot(c2st_ref[...], p_hi) + _dot(c2st_ref[...], p_lo)
        jj = rowi(n_slc)
        cur = tcol(n_slc) // SEL_BLOCK
        valid = jj <= cur
        forced = (jj == 0) | (jj == cur) | (jj == cur - 1)
        score = jnp.where(forced, SEL_FORCED, jnp.where(valid, imp, -SEL_FORCED))
        sel = _topk_mask(score, min(SEL_TOPN, n_slc)) & valid
        bias = _bias_rows_to_cols(sel)
        q4a.append(jnp.concatenate([jnp.concatenate([h, bias], axis=1) for h in heads], axis=0))

        start = pl.multiple_of(jnp.maximum(q0 - WINDOW, 0), HEAD_DIM)
        dpos = tcol(band) - (start + rowi(band))
        bias_w = jnp.where((dpos >= 0) & (dpos < WINDOW), 0.0, NEG_INF)
        p_w = _softmax_cols(_dot_nt(kw_ref[pl.ds(start, band), :], q4) + tile_r(bias_w))
        o_win.append(_dot(vwt_ref[:, pl.ds(start, band)], p_w.astype(BF16)))

    def sel_body(kt, carry):
        k0 = pl.multiple_of(kt * SEL_KT, SEL_KT)
        return tuple(_online_update(_dot_nt(ksa_ref[pl.ds(k0, SEL_KT), :], q4a[u]),
                                    vst_ref[:, pl.ds(k0, SEL_KT)], carry[u]) for u in range(U))

    n_full = base // SEL_KT
    init = (jnp.full((1, lanes), NEG_INF, F32), jnp.zeros((1, lanes), F32), jnp.zeros((HEAD_DIM, lanes), F32))
    carry = lax.fori_loop(0, n_full, sel_body, (init,) * U)
    k0 = pl.multiple_of(n_full * SEL_KT, SEL_KT)
    gt = gt_ref[...]
    for u in range(U):
        causal = jnp.where(k0 + rowi(SEL_KT) <= tcols[u](SEL_KT), 0.0, NEG_INF)
        s = _dot_nt(ksa_ref[pl.ds(k0, SEL_KT), :], q4a[u]) + tile_r(causal)
        _, l_s, acc_s = _online_update(s, vst_ref[:, pl.ds(k0, SEL_KT)], carry[u])
        o_slc = acc_s * (1.0 / l_s)

        g = gt[:, u * tq:(u + 1) * tq]
        for r in range(R):
            sl = slice(r * tq, (r + 1) * tq)
            o_r = (g[r:r + 1, :] * o_cmp[u][:, sl] + g[R + r:R + r + 1, :] * o_slc[:, sl]
                   + g[2 * R + r:2 * R + r + 1, :] * o_win[u][:, sl])
            o_ref[u * tq:(u + 1) * tq, r * HEAD_DIM:(r + 1) * HEAD_DIM] = o_r.T.astype(BF16)


def _block_onehot(seq, block):
    assert seq // block <= HEAD_DIM
    e = np.zeros((seq, HEAD_DIM), np.float32)
    e[np.arange(seq), np.arange(seq) // block] = 1.0
    return jnp.asarray(e, BF16)


def _nsa_attention(proj3, kc, vc, gates_t, c2st):
    B, S, _ = proj3.shape
    G = NSA_GROUPS
    nc = kc.shape[2]
    tq = NSA_TQ * NSA_SUB
    assert S % tq == 0 and SEL_KT % tq == 0
    nq = S // tq
    hb = HEAD_DIM
    gw = NSA_REP * HEAD_DIM
    e_sel = _block_onehot(S, SEL_BLOCK)
    seq_spec = lambda off: pl.BlockSpec((None, S, hb), lambda b, g, i: (b, 0, off // hb + g))
    cmp_spec = pl.BlockSpec((None, None, nc, hb), lambda b, g, i: (b, g, 0, 0))
    return pl.pallas_call(
        functools.partial(_nsa_kernel, seq=S),
        grid=(B, G, nq),
        in_specs=[pl.BlockSpec((None, tq, gw), lambda b, g, i: (b, i, OFF_QA // gw + g)),
                  cmp_spec, cmp_spec,
                  seq_spec(OFF_KSLC), seq_spec(OFF_VSLC), seq_spec(OFF_KWIN), seq_spec(OFF_VWIN),
                  pl.BlockSpec((GATE_ROWS, tq), lambda b, g, i: (g, b * nq + i)),
                  pl.BlockSpec(c2st.shape, lambda b, g, i: (0, 0)),
                  pl.BlockSpec(e_sel.shape, lambda b, g, i: (0, 0))],
        out_specs=pl.BlockSpec((None, tq, gw), lambda b, g, i: (b, i, g)),
        out_shape=jax.ShapeDtypeStruct((B, S, NSA_WIDTH), BF16),
        scratch_shapes=[pltpu.VMEM((S, 2 * hb), BF16), pltpu.VMEM((hb, S), BF16), pltpu.VMEM((hb, S), BF16),
                        pltpu.VMEM((hb, nc), BF16)],
        compiler_params=_cparams(("arbitrary", "arbitrary", "arbitrary")),
        name="nsa_attn",
    )(proj3, kc, vc, proj3, proj3, proj3, proj3, gates_t, c2st, e_sel)


def _moba_kernel(q_ref, k_ref, v_ref, e_ref, o_ref, ka_ref, vt_ref, kmh_ref, kml_ref, *, seq):
    blk = MOBA_BLOCK
    HP = MOBA_HEADS_PER_STEP
    i = pl.program_id(2)
    nb = seq // blk
    nbp = kmh_ref.shape[1]
    hsl = lambda h: slice(h * HEAD_DIM, (h + 1) * HEAD_DIM)

    @pl.when(i == 0)
    def _():
        for h in range(HP):
            _transpose_into(v_ref, h * HEAD_DIM, vt_ref.at[h], seq)
            ka_ref[h, :, 0:HEAD_DIM] = k_ref[:, hsl(h)]
            ka_ref[h, :, HEAD_DIM:2 * HEAD_DIM] = e_ref[...]
            means = [jnp.sum(k_ref[n * blk:(n + 1) * blk, hsl(h)].astype(F32), axis=0, keepdims=True) * (1.0 / blk)
                     for n in range(nb)]
            if nbp > nb:
                means.append(jnp.zeros((nbp - nb, HEAD_DIM), F32))
            hi, lo = _split_hi_lo(jnp.concatenate(means, axis=0))
            kmh_ref[h] = hi
            kml_ref[h] = lo

    own0 = pl.multiple_of(i * blk, blk)
    past = lax.broadcasted_iota(I32, (nbp, blk), 0) < i
    causal = jnp.where(lax.broadcasted_iota(I32, (blk, blk), 0) <= lax.broadcasted_iota(I32, (blk, blk), 1), 0.0, NEG_INF)
    qa, carry = [], []
    for h in range(HP):
        q = q_ref[:, hsl(h)]
        gate = _dot_nt(kmh_ref[h], q) + _dot_nt(kml_ref[h], q)
        sel = _topk_mask(jnp.where(past, gate, -SEL_FORCED), min(MOBA_TOPK, nb)) & past
        qa.append(jnp.concatenate([q, _bias_rows_to_cols(sel)], axis=1))
        s = _dot_nt(k_ref[pl.ds(own0, blk), hsl(h)], q) + causal
        m = jnp.max(s, axis=0, keepdims=True)
        p = jnp.exp2(s - m)
        carry.append((m, jnp.sum(p, axis=0, keepdims=True), _dot(vt_ref[h, :, pl.ds(own0, blk)], p.astype(BF16))))

    kt = 2 * blk

    def body(j, carry):
        k0 = pl.multiple_of(j * kt, kt)
        return tuple(_online_update(_dot_nt(ka_ref[h, pl.ds(k0, kt), :], qa[h]), vt_ref[h, :, pl.ds(k0, kt)], carry[h])
                     for h in range(HP))

    carry = lax.fori_loop(0, (i + 1) // 2, body, tuple(carry))
    for h in range(HP):
        _, l, acc = carry[h]
        o_ref[:, hsl(h)] = (acc * (1.0 / l)).T.astype(BF16)


def _moba_attention(proj3):
    B, S, _ = proj3.shape
    HP = MOBA_HEADS_PER_STEP
    blk = MOBA_BLOCK
    hb = HEAD_DIM
    assert (S // blk) % 2 == 0 and MOBA_HEADS % HP == 0
    nbp = max(8, -(-(S // blk) // 8) * 8)
    e_blk = _block_onehot(S, blk)
    wide = HP * hb
    seq_spec = lambda off: pl.BlockSpec((None, S, wide), lambda b, h, i: (b, 0, off // wide + h))
    return pl.pallas_call(
        functools.partial(_moba_kernel, seq=S),
        grid=(B, MOBA_HEADS // HP, S // blk),
        in_specs=[pl.BlockSpec((None, blk, wide), lambda b, h, i: (b, i, OFF_QB // wide + h)),
                  seq_spec(OFF_KB), seq_spec(OFF_VB),
                  pl.BlockSpec(e_blk.shape, lambda b, h, i: (0, 0))],
        out_specs=pl.BlockSpec((None, blk, wide), lambda b, h, i: (b, i, h)),
        out_shape=jax.ShapeDtypeStruct((B, S, MOBA_WIDTH), BF16),
        scratch_shapes=[pltpu.VMEM((HP, S, 2 * hb), BF16), pltpu.VMEM((HP, hb, S), BF16),
                        pltpu.VMEM((HP, nbp, hb), BF16), pltpu.VMEM((HP, nbp, hb), BF16)],
        compiler_params=_cparams(("arbitrary", "arbitrary", "arbitrary")),
        name="moba_attn",
    )(proj3, proj3, proj3, e_blk)


def _merge_proj_kernel(oa_ref, ob_ref, wa_ref, wb_ref, ga_ref, gb_ref, o_ref):
    a = _dot(oa_ref[...], wa_ref[...])
    b = _dot(ob_ref[...], wb_ref[...])
    o_ref[...] = (ga_ref[...].astype(F32) * a + gb_ref[...].astype(F32) * b).astype(BF16)


def _merge_proj(oa, ob, wa, wb, proj, tm):
    T = oa.shape[0]
    D = wa.shape[1]
    tn = min(COL_TILE, D)
    g0 = PLAIN_END // tn
    return pl.pallas_call(
        _merge_proj_kernel,
        grid=(T // tm, D // tn),
        in_specs=[pl.BlockSpec((tm, NSA_WIDTH), lambda i, j: (i, 0)),
                  pl.BlockSpec((tm, MOBA_WIDTH), lambda i, j: (i, 0)),
                  pl.BlockSpec((NSA_WIDTH, tn), lambda i, j: (0, j)),
                  pl.BlockSpec((MOBA_WIDTH, tn), lambda i, j: (0, j)),
                  pl.BlockSpec((tm, tn), lambda i, j: (i, g0 + j)),
                  pl.BlockSpec((tm, tn), lambda i, j: (i, g0 + D // tn + j))],
        out_specs=pl.BlockSpec((tm, tn), lambda i, j: (i, j)),
        out_shape=jax.ShapeDtypeStruct((T, D), BF16),
        compiler_params=_cparams(("arbitrary", "arbitrary")),
        name="merge_proj",
    )(oa, ob, wa, wb, proj, proj)


def _out_router_kernel(x_ref, mix_ref, wo_ref, g_ref, wrh_ref, wrl_ref, br_ref, x1_ref, h2_ref, te_ref, tw_ref):
    tm = x_ref.shape[0]
    E = wrh_ref.shape[0]
    x1 = x_ref[...] + _dot(mix_ref[...], wo_ref[...])
    x1_ref[...] = x1
    h2 = x1 * lax.rsqrt(jnp.mean(x1 * x1, axis=-1, keepdims=True) + NORM_EPS) * g_ref[...]
    h2_ref[...] = h2
    h_hi, h_lo = _split_hi_lo(h2)
    logits = (_dot_nt(wrh_ref[...], h_hi) + _dot_nt(wrh_ref[...], h_lo) + _dot_nt(wrl_ref[...], h_hi)
              + br_ref[...])
    e_idx = lax.broadcasted_iota(I32, (E, tm), 0)
    picks = _topk_rows(logits, e_idx, E, TOP_K)
    vals = [p[0] for p in picks]
    exps = [jnp.exp(v - vals[0]) for v in vals]
    inv = 1.0 / (exps[0] + exps[1] + exps[2] + exps[3])
    te_ref[...] = jnp.concatenate([p[1] for p in picks] + [jnp.zeros((8 - TOP_K, tm), I32)], axis=0)
    w_t = jnp.concatenate([e * inv for e in exps] + [jnp.zeros((HEAD_DIM - TOP_K, tm), F32)], axis=0)
    tw_ref[...] = w_t.T


def _out_router(x2, mix, wo, g, wrh, wrl, br, tm):
    T, D = x2.shape
    E = wrh.shape[0]
    full = lambda a: pl.BlockSpec(a.shape, lambda i: (0,) * a.ndim)
    return pl.pallas_call(
        _out_router_kernel,
        grid=(T // tm,),
        in_specs=[pl.BlockSpec((tm, D), lambda i: (i, 0)), pl.BlockSpec((tm, D), lambda i: (i, 0)),
                  full(wo), full(g), full(wrh), full(wrl), full(br)],
        out_specs=[pl.BlockSpec((tm, D), lambda i: (i, 0)), pl.BlockSpec((tm, D), lambda i: (i, 0)),
                   pl.BlockSpec((8, tm), lambda i: (0, i)), pl.BlockSpec((tm, HEAD_DIM), lambda i: (i, 0))],
        out_shape=[jax.ShapeDtypeStruct((T, D), F32), jax.ShapeDtypeStruct((T, D), F32),
                   jax.ShapeDtypeStruct((8, T), I32), jax.ShapeDtypeStruct((T, HEAD_DIM), F32)],
        compiler_params=_cparams(("arbitrary",)),
        name="out_router",
    )(x2, mix, wo, g, wrh, wrl, br)


def _route_kernel(te_ref, dest_ref, blke_ref, run_ref, pstart_ref):
    ph = pl.program_id(0)
    i = pl.program_id(1)
    E = run_ref.shape[0]
    tm = te_ref.shape[1]
    nbp = blke_ref.shape[1]
    te = te_ref[...]
    e_idx = lax.broadcasted_iota(I32, (E, tm), 0)
    ohs = [te[k:k + 1, :] == e_idx for k in range(TOP_K)]
    oh = jnp.where(ohs[0] | ohs[1] | ohs[2] | ohs[3], 1.0, 0.0)
    tile_cnt = jnp.sum(oh, axis=1, keepdims=True)

    @pl.when((ph == 0) & (i == 0))
    def _():
        run_ref[...] = jnp.zeros(run_ref.shape, F32)

    @pl.when(ph == 0)
    def _():
        run_ref[...] = run_ref[...] + tile_cnt
        dest_ref[...] = jnp.zeros(dest_ref.shape, I32)

    @pl.when((ph == 1) & (i == 0))
    def _():
        counts = run_ref[...]
        padded = jnp.floor((counts + (ROW_BLOCK - 1)) * (1.0 / ROW_BLOCK)) * ROW_BLOCK
        row = lax.broadcasted_iota(I32, counts.shape, 0)
        incl = padded
        sh = 1
        while sh < E:
            incl = incl + jnp.where(row >= sh, pltpu.roll(incl, sh, 0), 0.0)
            sh *= 2
        pstart_ref[...] = incl - padded
        blk_start = (lax.broadcasted_iota(I32, (E, nbp), 1) * ROW_BLOCK).astype(F32)
        blke_ref[...] = jnp.sum(jnp.where(incl[:, 0:1] <= blk_start, 1, 0), axis=0, keepdims=True).astype(I32)
        run_ref[...] = jnp.zeros(run_ref.shape, F32)

    @pl.when(ph == 1)
    def _():
        upper = jnp.where(lax.broadcasted_iota(I32, (tm, tm), 0) < lax.broadcasted_iota(I32, (tm, tm), 1), 1.0, 0.0)
        before = _dot(oh.astype(BF16), upper.astype(BF16))
        val = before + pstart_ref[:, 0:1] + run_ref[:, 0:1]
        rows = [jnp.sum(jnp.where(ohs[k], val, 0.0), axis=0, keepdims=True) for k in range(TOP_K)]
        dest_ref[...] = jnp.concatenate(rows + [jnp.zeros((8 - TOP_K, tm), F32)], axis=0).astype(I32)
        run_ref[...] = run_ref[...] + tile_cnt


def _route(top_e, n_experts, n_blk_pad, tm):
    T = top_e.shape[1]
    return pl.pallas_call(
        _route_kernel,
        grid=(2, T // tm),
        in_specs=[pl.BlockSpec((8, tm), lambda ph, i: (0, i))],
        out_specs=[pl.BlockSpec((8, tm), lambda ph, i: (0, i * ph)),
                   pl.BlockSpec((1, n_blk_pad), lambda ph, i: (0, 0))],
        out_shape=[jax.ShapeDtypeStruct((8, T), I32), jax.ShapeDtypeStruct((1, n_blk_pad), I32)],
        scratch_shapes=[pltpu.VMEM((n_experts, HEAD_DIM), F32), pltpu.VMEM((n_experts, HEAD_DIM), F32)],
        compiler_params=_cparams(("arbitrary", "arbitrary")),
        name="route",
    )(top_e)


def _row_copy(src_ref, src_row, dst_ref, dst_row, sem):
    return pltpu.make_async_copy(src_ref.at[pl.ds(src_row, 1), :], dst_ref.at[pl.ds(dst_row, 1), :], sem)


def _dispatch_kernel(dest_ref, h_ref, xz_ref, xpad_ref, sem):
    del xz_ref
    tm = h_ref.shape[0]

    def issue(t, _):
        for k in range(TOP_K):
            _row_copy(h_ref, t, xpad_ref, dest_ref[k, t], sem).start()
        return 0
    lax.fori_loop(0, tm, issue, 0)

    def drain(t, _):
        for k in range(TOP_K):
            _row_copy(h_ref, 0, xpad_ref, 0, sem).wait()
        return 0
    lax.fori_loop(0, tm, drain, 0)


def _dispatch(dest, h2, x_zero, tm):
    T, D = h2.shape
    return pl.pallas_call(
        _dispatch_kernel,
        grid=(T // tm,),
        in_specs=[pl.BlockSpec((8, tm), lambda i: (0, i), memory_space=pltpu.SMEM),
                  pl.BlockSpec((tm, D), lambda i: (i, 0)),
                  pl.BlockSpec(memory_space=pl.ANY)],
        out_specs=pl.BlockSpec(memory_space=pl.ANY),
        out_shape=jax.ShapeDtypeStruct(x_zero.shape, x_zero.dtype),
        scratch_shapes=[pltpu.SemaphoreType.DMA(())],
        input_output_aliases={2: 0},
        compiler_params=_cparams(("arbitrary",)),
        name="dispatch",
    )(dest, h2, x_zero)


def _expert_changed(be_ref, m):
    prev = be_ref[jnp.maximum(m - 1, 0)]
    return (m == 0) | (be_ref[m] != prev)


def _expert_up_kernel(be_ref, x_ref, wg_ref, wl_ref, bg_ref, bl_ref, act_ref, wgb_ref, wlb_ref, *, n_experts):
    m = pl.program_id(1)

    @pl.when(_expert_changed(be_ref, m))
    def _():
        wgb_ref[...] = wg_ref[...].astype(BF16)
        wlb_ref[...] = wl_ref[...].astype(BF16)

    @pl.when(be_ref[m] < n_experts)
    def _():
        x = x_ref[...].astype(BF16)
        gate = jnp.minimum(_dot(x, wgb_ref[...]) + bg_ref[...], SWIGLU_LIMIT)
        lin = jnp.clip(_dot(x, wlb_ref[...]) + bl_ref[...], -SWIGLU_LIMIT, SWIGLU_LIMIT)
        act_ref[...] = (gate * _sigmoid(SWIGLU_ALPHA * gate) * (lin + 1.0)).astype(BF16)

    @pl.when(be_ref[m] >= n_experts)
    def _():
        act_ref[...] = jnp.zeros(act_ref.shape, BF16)


def _expert_up(blk_e, x_pad, w_gu, b_gu, n_blk, tn):
    P, D = x_pad.shape
    E = w_gu.shape[0]
    F = w_gu.shape[2] // 2
    nf = F // tn
    ex = lambda be, m: jnp.minimum(be[m], E - 1)
    grid_spec = pltpu.PrefetchScalarGridSpec(
        num_scalar_prefetch=1,
        grid=(nf, n_blk),
        in_specs=[pl.BlockSpec((ROW_BLOCK, D), lambda n, m, be: (m, 0)),
                  pl.BlockSpec((None, D, tn), lambda n, m, be: (ex(be, m), 0, n)),
                  pl.BlockSpec((None, D, tn), lambda n, m, be: (ex(be, m), 0, nf + n)),
                  pl.BlockSpec((None, 1, tn), lambda n, m, be: (ex(be, m), 0, n)),
                  pl.BlockSpec((None, 1, tn), lambda n, m, be: (ex(be, m), 0, nf + n))],
        out_specs=pl.BlockSpec((ROW_BLOCK, tn), lambda n, m, be: (m, n)),
        scratch_shapes=[pltpu.VMEM((D, tn), BF16), pltpu.VMEM((D, tn), BF16)])
    return pl.pallas_call(
        functools.partial(_expert_up_kernel, n_experts=E),
        grid_spec=grid_spec,
        out_shape=jax.ShapeDtypeStruct((P, F), BF16),
        compiler_params=_cparams(("arbitrary", "arbitrary")),
        name="expert_up",
    )(blk_e, x_pad, w_gu, w_gu, b_gu, b_gu)


def _expert_down_kernel(be_ref, a_ref, w_ref, b_ref, y_ref, wb_ref, *, n_experts):
    m = pl.program_id(1)

    @pl.when(_expert_changed(be_ref, m))
    def _():
        wb_ref[...] = w_ref[...].astype(BF16)

    @pl.when(be_ref[m] < n_experts)
    def _():
        y_ref[...] = _dot(a_ref[...], wb_ref[...]) + b_ref[...]

    @pl.when(be_ref[m] >= n_experts)
    def _():
        y_ref[...] = jnp.zeros(y_ref.shape, F32)


def _expert_down(blk_e, act, w_d, b_d, n_blk, tn):
    P, F = act.shape
    E, _, D = w_d.shape
    ex = lambda be, m: jnp.minimum(be[m], E - 1)
    grid_spec = pltpu.PrefetchScalarGridSpec(
        num_scalar_prefetch=1,
        grid=(D // tn, n_blk),
        in_specs=[pl.BlockSpec((ROW_BLOCK, F), lambda n, m, be: (m, 0)),
                  pl.BlockSpec((None, F, tn), lambda n, m, be: (ex(be, m), 0, n)),
                  pl.BlockSpec((None, 1, tn), lambda n, m, be: (ex(be, m), 0, n))],
        out_specs=pl.BlockSpec((ROW_BLOCK, tn), lambda n, m, be: (m, n)),
        scratch_shapes=[pltpu.VMEM((F, tn), BF16)])
    return pl.pallas_call(
        functools.partial(_expert_down_kernel, n_experts=E),
        grid_spec=grid_spec,
        out_shape=jax.ShapeDtypeStruct((P, D), F32),
        compiler_params=_cparams(("arbitrary", "arbitrary")),
        name="expert_down",
    )(blk_e, act, w_d, b_d)


def _combine_kernel(dest_ref, x1_ref, tw_ref, g_ref, ypad_ref, o_ref, ybuf_ref, sem):
    tm = x1_ref.shape[0]

    def issue(t, _):
        for k in range(TOP_K):
            _row_copy(ypad_ref, dest_ref[k, t], ybuf_ref.at[k], t, sem).start()
        return 0
    lax.fori_loop(0, tm, issue, 0)

    def drain(t, _):
        for k in range(TOP_K):
            _row_copy(ypad_ref, 0, ybuf_ref.at[k], 0, sem).wait()
        return 0
    lax.fori_loop(0, tm, drain, 0)

    tw = tw_ref[...]
    acc = x1_ref[...]
    for k in range(TOP_K):
        acc = acc + tw[:, k:k + 1] * ybuf_ref[k]
    o_ref[...] = acc * lax.rsqrt(jnp.mean(acc * acc, axis=-1, keepdims=True) + NORM_EPS) * g_ref[...]


def _combine(dest, x1, top_w, g, y_pad, tm):
    T, D = x1.shape
    return pl.pallas_call(
        _combine_kernel,
        grid=(T // tm,),
        in_specs=[pl.BlockSpec((8, tm), lambda i: (0, i), memory_space=pltpu.SMEM),
                  pl.BlockSpec((tm, D), lambda i: (i, 0)),
                  pl.BlockSpec((tm, HEAD_DIM), lambda i: (i, 0)),
                  pl.BlockSpec((1, D), lambda i: (0, 0)),
                  pl.BlockSpec(memory_space=pl.ANY)],
        out_specs=pl.BlockSpec((tm, D), lambda i: (i, 0)),
        out_shape=jax.ShapeDtypeStruct((T, D), F32),
        scratch_shapes=[pltpu.VMEM((TOP_K, tm, D), F32), pltpu.SemaphoreType.DMA(())],
        compiler_params=_cparams(("arbitrary",)),
        name="combine",
    )(dest, x1, top_w, g, y_pad)


def _rope_tables(pos):
    inv_freq = ROPE_THETA ** (-jnp.arange(0, ROPE_DIM, 2, dtype=F32) / ROPE_DIM)
    ang = pos.astype(F32)[..., None] * inv_freq
    cos, sin = jnp.cos(ang), jnp.sin(ang)
    rest = HEAD_DIM - ROPE_DIM
    cos_t = jnp.concatenate([cos, cos, jnp.ones(cos.shape[:-1] + (rest,), F32)], axis=-1)
    sin_t = jnp.concatenate([-sin, sin, jnp.zeros(sin.shape[:-1] + (rest,), F32)], axis=-1)
    return cos_t, sin_t


def _cmp_to_slc_t(seq, nc_pad):
    n_cmp = (seq - CMP_BLOCK) // CMP_STRIDE + 1
    n_slc = seq // SEL_BLOCK
    cmp_start = np.arange(n_cmp) * CMP_STRIDE
    slc_start = np.arange(n_slc) * SEL_BLOCK
    overlap = np.clip(np.minimum(cmp_start[:, None] + CMP_BLOCK, slc_start[None, :] + SEL_BLOCK)
                      - np.maximum(cmp_start[:, None], slc_start[None, :]), 0, None)
    out = np.zeros((n_slc, nc_pad), np.float32)
    out[:, :n_cmp] = (overlap / CMP_BLOCK).T
    return jnp.asarray(out, BF16)


def _regroup_w_in(w):
    D = w.shape[0]
    o_kv = NSA_WIDTH
    o_gn = o_kv + 6 * KV_WIDTH
    o_b = o_gn + 3 * NSA_HEADS
    o_gm = o_b + 3 * MOBA_WIDTH
    kv = lambda s: w[:, o_kv + s * KV_WIDTH:o_kv + (s + 1) * KV_WIDTH]
    mb = lambda s: w[:, o_b + s * MOBA_WIDTH:o_b + (s + 1) * MOBA_WIDTH]
    cols = [w[:, :NSA_WIDTH], kv(2), kv(4), mb(0), mb(1), kv(0), kv(1), kv(3), kv(5), mb(2), w[:, o_gm:o_gm + 2 * D]]
    w_main = jnp.concatenate(cols, axis=1).astype(BF16)
    wg = w[:, o_gn:o_gn + 3 * NSA_HEADS]
    wg = wg.reshape(D, 3, NSA_GROUPS, NSA_REP).transpose(2, 1, 3, 0).reshape(NSA_GROUPS, 3 * NSA_REP, D)
    wg = jnp.pad(wg, ((0, 0), (0, GATE_ROWS - 3 * NSA_REP), (0, 0))).reshape(NSA_GROUPS * GATE_ROWS, D)
    return w_main, wg.astype(BF16)


def kernel(x, positions, g_attn_norm, w_in, pe_cmp_k, w_cmp_k1, w_cmp_k2, pe_cmp_v, w_cmp_v1, w_cmp_v2, w_proj_nsa, w_proj_moba, w_out, g_ffn_norm, w_router, b_router, w_gate_up, b_gate_up, w_down, b_down, g_final_norm):
    B, S, D = x.shape
    T = B * S
    E = w_router.shape[-1]
    F = w_down.shape[-2]
    depth = w_in.shape[0]
    assert depth == 1, "the final norm is fused into the single layer's combine step"
    assert S % MOBA_BLOCK == 0 and S >= WINDOW + NSA_TQ and (2 * D) % COL_TILE == 0 and T % 512 == 0
    nc = S // CMP_STRIDE
    tm_big = min(1024, T)

    cos_t, sin_t = _rope_tables(positions)
    cmp_end = np.arange(nc) * CMP_STRIDE + CMP_BLOCK - 1
    cmp_end = np.minimum(cmp_end, S - 1)
    cos_c, sin_c = cos_t[:, cmp_end], sin_t[:, cmp_end]
    c2st = _cmp_to_slc_t(S, nc)

    x2 = x.reshape(T, D)
    for layer in range(depth):
        w_main, w_gates_t = _regroup_w_in(w_in[layer])
        h, gates_t = _norm_gates(x2, g_attn_norm[layer].reshape(1, D), w_gates_t, tm_big)
        proj = _in_proj(h, w_main, cos_t.reshape(T, HEAD_DIM), sin_t.reshape(T, HEAD_DIM), tm_big)
        proj3 = proj.reshape(B, S, proj.shape[1])

        xc = proj3[:, :, OFF_KCMP:OFF_KCMP + 2 * KV_WIDTH].reshape(B, nc, CMP_STRIDE * 2 * KV_WIDTH)
        kc, vc = _compress(xc, pe_cmp_k[layer], w_cmp_k1[layer].astype(BF16), w_cmp_k2[layer].astype(BF16),
                           pe_cmp_v[layer], w_cmp_v1[layer].astype(BF16), w_cmp_v2[layer].astype(BF16), cos_c, sin_c)
        o_a = _nsa_attention(proj3, kc, vc, gates_t, c2st)
        o_b = _moba_attention(proj3)

        mix = _merge_proj(o_a.reshape(T, NSA_WIDTH), o_b.reshape(T, MOBA_WIDTH),
                          w_proj_nsa[layer].astype(BF16), w_proj_moba[layer].astype(BF16), proj, tm_big)
        wr_t = w_router[layer].T
        wr_hi = wr_t.astype(BF16)
        wr_lo = (wr_t - wr_hi.astype(F32)).astype(BF16)
        x1, h2, top_e, top_w = _out_router(x2, mix, w_out[layer].astype(BF16), g_ffn_norm[layer].reshape(1, D),
                                           wr_hi, wr_lo, b_router[layer].reshape(E, 1), 256)

        n_blk = (T * TOP_K) // ROW_BLOCK + E
        n_blk_pad = -(-n_blk // HEAD_DIM) * HEAD_DIM
        dest, blk_e = _route(top_e, E, n_blk_pad, 512)
        x_pad = _dispatch(dest, h2, jnp.zeros((n_blk * ROW_BLOCK, D), F32), 512)
        blk_e = blk_e.reshape(n_blk_pad)
        act = _expert_up(blk_e, x_pad, w_gate_up[layer], b_gate_up[layer].reshape(E, 1, 2 * F), n_blk, min(1024, F))
        y_pad = _expert_down(blk_e, act, w_down[layer], b_down[layer].reshape(E, 1, D), n_blk, D)
        x2 = _combine(dest, x1, top_w, g_final_norm.reshape(1, D), y_pad, 256)
    return x2.reshape(B, S, D)
```

```python
import functools

import jax
import jax.numpy as jnp
import numpy as np
from jax import lax
from jax.experimental import pallas as pl
from jax.experimental.pallas import tpu as pltpu

F32 = jnp.float32
BF16 = jnp.bfloat16
I32 = jnp.int32

HEAD_DIM = 128
ROPE_DIM = HEAD_DIM // 4
ROPE_HALF = ROPE_DIM // 2
ROPE_THETA = 500000.0
NORM_EPS = 1e-5
NEG_INF = -1e30
REMOVED = -3e38
SEL_FORCED = 1e9

NSA_HEADS = 8
NSA_GROUPS = 2
NSA_REP = NSA_HEADS // NSA_GROUPS
CMP_BLOCK = 32
CMP_STRIDE = 16
CMP_HIDDEN = 256
SEL_BLOCK = 64
SEL_TOPN = 16
WINDOW = 512
MOBA_HEADS = 8
MOBA_BLOCK = 256
MOBA_TOPK = 3
TOP_K = 4
SWIGLU_LIMIT = 7.0
SWIGLU_ALPHA = 1.702
ROW_BLOCK = 256

NSA_WIDTH = NSA_HEADS * HEAD_DIM
MOBA_WIDTH = MOBA_HEADS * HEAD_DIM
KV_WIDTH = NSA_GROUPS * HEAD_DIM
SCALE = HEAD_DIM ** -0.5
Q_SCALE = SCALE * 1.4426950408889634

COL_TILE = 512
OFF_QA = 0
OFF_KSLC = NSA_WIDTH
OFF_KWIN = OFF_KSLC + KV_WIDTH
OFF_QB = OFF_KWIN + KV_WIDTH
OFF_KB = OFF_QB + MOBA_WIDTH
ROPE_COLS = OFF_KB + MOBA_WIDTH
OFF_KCMP = ROPE_COLS
OFF_VCMP = OFF_KCMP + KV_WIDTH
OFF_VSLC = OFF_VCMP + KV_WIDTH
OFF_VWIN = OFF_VSLC + KV_WIDTH
OFF_VB = OFF_VWIN + KV_WIDTH
PLAIN_END = OFF_VB + MOBA_WIDTH
N_ROPE_TILES = ROPE_COLS // COL_TILE
N_PLAIN_TILES = (PLAIN_END - ROPE_COLS) // COL_TILE
Q_TILES = tuple(range(OFF_QA // COL_TILE, OFF_KSLC // COL_TILE)) + tuple(range(OFF_QB // COL_TILE, OFF_KB // COL_TILE))

VMEM_LIMIT = 56 * 1024 * 1024
NSA_TQ = 128
SEL_KT = 512
GATE_ROWS = 16
IN_PROJ_CHUNK = 256
NSA_SUB = 4
MOBA_HEADS_PER_STEP = 4


def _cparams(sem):
    return pltpu.CompilerParams(dimension_semantics=sem, vmem_limit_bytes=VMEM_LIMIT)


def _sigmoid(z):
    return 1.0 / (1.0 + jnp.exp(-z))


def _dot(a, b):
    return jnp.dot(a, b, preferred_element_type=F32)


def _dot_nt(a, b):
    return lax.dot_general(a, b, (((1,), (1,)), ((), ())), preferred_element_type=F32)


def _split_hi_lo(v):
    hi = v.astype(BF16)
    lo = (v - hi.astype(F32)).astype(BF16)
    return hi, lo


def _topk_rows(work, row_idx, n_rows, k):
    picks = []
    for _ in range(k):
        m = jnp.max(work, axis=0, keepdims=True)
        first = jnp.min(jnp.where(work == m, row_idx, n_rows), axis=0, keepdims=True)
        pick = row_idx == first
        picks.append((m, first, pick))
        work = jnp.where(pick, REMOVED, work)
    return picks


def _norm_gates_kernel(x_ref, g_ref, wgt_ref, h_ref, gt_ref):
    x = x_ref[...]
    h = x * lax.rsqrt(jnp.mean(x * x, axis=-1, keepdims=True) + NORM_EPS) * g_ref[...]
    hb = h.astype(BF16)
    h_ref[...] = hb
    gt_ref[...] = _sigmoid(_dot_nt(wgt_ref[...], hb))


def _norm_gates(x2, g, wgt, tm):
    T, D = x2.shape
    R = wgt.shape[0]
    return pl.pallas_call(
        _norm_gates_kernel,
        grid=(T // tm,),
        in_specs=[pl.BlockSpec((tm, D), lambda i: (i, 0)),
                  pl.BlockSpec((1, D), lambda i: (0, 0)),
                  pl.BlockSpec((R, D), lambda i: (0, 0))],
        out_specs=[pl.BlockSpec((tm, D), lambda i: (i, 0)),
                   pl.BlockSpec((R, tm), lambda i: (0, i))],
        out_shape=[jax.ShapeDtypeStruct((T, D), BF16), jax.ShapeDtypeStruct((R, T), F32)],
        compiler_params=_cparams(("arbitrary",)),
        name="norm_gates",
    )(x2, g, wgt)


def _rope_tile(xh, c, s, lane):
    rot = jnp.where(lane < ROPE_HALF, pltpu.roll(xh, HEAD_DIM - ROPE_HALF, 1), pltpu.roll(xh, ROPE_HALF, 1))
    return xh * c + rot * s


def _in_proj_kernel(h_ref, w_ref, c_ref, s_ref, o_ref):
    j = pl.program_id(1)
    tm = h_ref.shape[0]
    chunk = min(IN_PROJ_CHUNK, tm)

    def by_chunks(epilogue):
        for r0 in range(0, tm, chunk):
            rows = slice(r0, r0 + chunk)
            epilogue(rows, _dot(h_ref[rows, :], w_ref[...]))

    @pl.when(j < N_ROPE_TILES)
    def _():
        is_q = j == Q_TILES[0]
        for qt in Q_TILES[1:]:
            is_q = is_q | (j == qt)
        f = jnp.where(is_q, Q_SCALE, 1.0).astype(F32)
        lane = lax.broadcasted_iota(I32, (chunk, HEAD_DIM), 1)

        def rope(rows, acc):
            c = c_ref[rows, :] * f
            s = s_ref[rows, :] * f
            for hh in range(COL_TILE // HEAD_DIM):
                sl = slice(hh * HEAD_DIM, (hh + 1) * HEAD_DIM)
                o_ref[rows, sl] = _rope_tile(acc[:, sl], c, s, lane).astype(BF16)
        by_chunks(rope)

    @pl.when((j >= N_ROPE_TILES) & (j < N_ROPE_TILES + N_PLAIN_TILES))
    def _():
        def plain(rows, acc):
            o_ref[rows, :] = acc.astype(BF16)
        by_chunks(plain)

    @pl.when(j >= N_ROPE_TILES + N_PLAIN_TILES)
    def _():
        def gate(rows, acc):
            o_ref[rows, :] = _sigmoid(acc).astype(BF16)
        by_chunks(gate)


def _in_proj(h, w, cos_t, sin_t, tm):
    T, D = h.shape
    N = w.shape[1]
    return pl.pallas_call(
        _in_proj_kernel,
        grid=(T // tm, N // COL_TILE),
        in_specs=[pl.BlockSpec((tm, D), lambda i, j: (i, 0)),
                  pl.BlockSpec((D, COL_TILE), lambda i, j: (0, j)),
                  pl.BlockSpec((tm, HEAD_DIM), lambda i, j: (i, 0)),
                  pl.BlockSpec((tm, HEAD_DIM), lambda i, j: (i, 0))],
        out_specs=pl.BlockSpec((tm, COL_TILE), lambda i, j: (i, j)),
        out_shape=jax.ShapeDtypeStruct((T, N), BF16),
        compiler_params=_cparams(("arbitrary", "arbitrary")),
        name="in_proj",
    )(h, w, cos_t, sin_t)


def _gelu_tanh(x):
    return 0.5 * x * (1.0 + jnp.tanh(0.7978845608028654 * (x + 0.044715 * x * x * x)))


def _compress_kernel(x_ref, pek_ref, w1k_ref, w2k_ref, pev_ref, w1v_ref, w2v_ref, c_ref, s_ref, kc_ref, vc_ref):
    nc = x_ref.shape[0]
    half = CMP_STRIDE * HEAD_DIM
    tok_w = 2 * KV_WIDTH
    for which, (pe_ref, w1_ref, w2_ref, out_ref) in enumerate(
            ((pek_ref, w1k_ref, w2k_ref, kc_ref), (pev_ref, w1v_ref, w2v_ref, vc_ref))):
        for g in range(NSA_GROUPS):
            acc_a = jnp.zeros((nc, CMP_HIDDEN), F32)
            acc_b = jnp.zeros((nc, CMP_HIDDEN), F32)
            for l in range(CMP_STRIDE):
                off = l * tok_w + which * KV_WIDTH + g * HEAD_DIM
                xl = x_ref[:, off:off + HEAD_DIM].astype(F32)
                xa = (xl + pe_ref[l:l + 1, :]).astype(BF16)
                xb = (xl + pe_ref[CMP_STRIDE + l:CMP_STRIDE + l + 1, :]).astype(BF16)
                acc_a = acc_a + _dot(xa, w1_ref[l * HEAD_DIM:(l + 1) * HEAD_DIM, :])
                acc_b = acc_b + _dot(xb, w1_ref[half + l * HEAD_DIM:half + (l + 1) * HEAD_DIM, :])
            hid = _gelu_tanh(acc_a + pltpu.roll(acc_b, nc - 1, 0))
            out = _dot(hid.astype(BF16), w2_ref[...])
            if which == 0:
                lane = lax.broadcasted_iota(I32, out.shape, 1)
                out = _rope_tile(out, c_ref[...], s_ref[...], lane)
            out_ref[g] = out.astype(BF16)


def _compress(xc, pek, w1k, w2k, pev, w1v, w2v, cos_c, sin_c):
    B, nc, W = xc.shape
    full = lambda a: pl.BlockSpec(a.shape, lambda b: (0,) * a.ndim)
    out_sds = jax.ShapeDtypeStruct((B, NSA_GROUPS, nc, HEAD_DIM), BF16)
    out_spec = pl.BlockSpec((None, NSA_GROUPS, nc, HEAD_DIM), lambda b: (b, 0, 0, 0))
    return pl.pallas_call(
        _compress_kernel,
        grid=(B,),
        in_specs=[pl.BlockSpec((None, nc, W), lambda b: (b, 0, 0)),
                  full(pek), full(w1k), full(w2k), full(pev), full(w1v), full(w2v),
                  pl.BlockSpec((None, nc, HEAD_DIM), lambda b: (b, 0, 0)),
                  pl.BlockSpec((None, nc, HEAD_DIM), lambda b: (b, 0, 0))],
        out_specs=[out_spec, out_spec],
        out_shape=[out_sds, out_sds],
        compiler_params=_cparams(("arbitrary",)),
        name="compress",
    )(xc, pek, w1k, w2k, pev, w1v, w2v, cos_c, sin_c)


def _transpose_into(src_ref, col0, dst_ref, n_rows):
    def body(b, _):
        r0 = pl.multiple_of(b * HEAD_DIM, HEAD_DIM)
        blk = src_ref[pl.ds(r0, HEAD_DIM), col0:col0 + HEAD_DIM].astype(F32)
        dst_ref[:, pl.ds(r0, HEAD_DIM)] = blk.T.astype(BF16)
        return 0
    lax.fori_loop(0, n_rows // HEAD_DIM, body, 0)


def _topk_mask(score, k):
    n, L = score.shape
    sub = 8
    assert n % sub == 0
    groups = n // sub
    segs = [score[sub * g:sub * (g + 1), :] for g in range(groups)]
    sub_idx = lax.broadcasted_iota(I32, (sub, L), 0)
    later = [jnp.where(sub_idx > r, 1.0, 0.0) for r in range(sub)]
    ranks = [jnp.zeros((sub, L), F32) for _ in range(groups)]
    for i in range(n):
        gi, ri = divmod(i, sub)
        row = score[i:i + 1, :]
        for g in range(groups):
            if g > gi:
                inc = jnp.where(row >= segs[g], 1.0, 0.0)
            elif g < gi:
                inc = jnp.where(row > segs[g], 1.0, 0.0)
            else:
                inc = jnp.where(row > segs[g], 1.0, jnp.where(row == segs[g], later[ri], 0.0))
            ranks[g] = ranks[g] + inc
    return jnp.concatenate(ranks, axis=0) < k


def _bias_rows_to_cols(sel, lanes_out=HEAD_DIM):
    n, L = sel.shape
    bias_t = jnp.where(sel, 0.0, NEG_INF)
    if n < lanes_out:
        bias_t = jnp.concatenate([bias_t, jnp.zeros((lanes_out - n, L), F32)], axis=0)
    return bias_t.T.astype(BF16)


def _softmax_cols(s, live=None):
    m = jnp.max(s, axis=0, keepdims=True)
    p = jnp.exp2(s - m)
    inv = 1.0 / jnp.sum(p, axis=0, keepdims=True)
    if live is not None:
        inv = jnp.where(live, inv, 0.0)
    return p * inv


def _online_update(s, vt, carry):
    m, l, acc = carry
    m_new = jnp.maximum(m, jnp.max(s, axis=0, keepdims=True))
    alpha = jnp.exp2(m - m_new)
    p = jnp.exp2(s - m_new)
    l = alpha * l + jnp.sum(p, axis=0, keepdims=True)
    acc = alpha * acc + _dot(vt, p.astype(BF16))
    return m_new, l, acc


def _nsa_kernel(q_ref, kc_ref, vc_ref, ks_ref, vs_ref, kw_ref, vw_ref, gt_ref, c2st_ref, e_ref, o_ref,
                ksa_ref, vst_ref, vwt_ref, vct_ref, *, seq):
    tq = NSA_TQ
    U = NSA_SUB
    R = NSA_REP
    i = pl.program_id(2)
    base = i * (U * tq)
    nc = kc_ref.shape[0]
    n_slc = c2st_ref.shape[0]
    lanes = R * tq
    band = WINDOW + tq
    tile_r = lambda a: jnp.concatenate([a] * R, axis=1)
    rowi = lambda n: lax.broadcasted_iota(I32, (n, tq), 0)

    @pl.when(i == 0)
    def _():
        _transpose_into(vs_ref, 0, vst_ref, seq)
        _transpose_into(vw_ref, 0, vwt_ref, seq)
        _transpose_into(vc_ref, 0, vct_ref, nc)
        ksa_ref[:, 0:HEAD_DIM] = ks_ref[...]
        ksa_ref[:, HEAD_DIM:2 * HEAD_DIM] = e_ref[...]

    subs = range(U)
    heads = [[q_ref[u * tq:(u + 1) * tq, r * HEAD_DIM:(r + 1) * HEAD_DIM] for r in range(R)] for u in subs]
    q4 = [jnp.concatenate(heads[u], axis=0) for u in subs]
    tcols = [lambda n, q0=base + u * tq: q0 + lax.broadcasted_iota(I32, (n, tq), 1) for u in subs]
    starts = [pl.multiple_of(jnp.maximum(base + u * tq - WINDOW, 0), HEAD_DIM) for u in subs]

    s_c = [_dot_nt(kc_ref[...], q4[u]) for u in subs]
    s_w = [_dot_nt(kw_ref[pl.ds(starts[u], band), :], q4[u]) for u in subs]

    p_c = []
    for u in subs:
        bias_c = jnp.where(rowi(nc) * CMP_STRIDE + (CMP_BLOCK - 1) <= tcols[u](nc), 0.0, NEG_INF)
        live_c = tcols[u](1) >= CMP_BLOCK - 1
        p_c.append(_softmax_cols(s_c[u] + tile_r(bias_c), tile_r(live_c)))
    o_cmp = [_dot(vct_ref[...], p_c[u].astype(BF16)) for u in subs]

    imp = []
    for u in subs:
        p_sum = p_c[u][:, 0:tq]
        for r in range(1, R):
            p_sum = p_sum + p_c[u][:, r * tq:(r + 1) * tq]
        p_hi, p_lo = _split_hi_lo(p_sum)
        imp.append(_dot(c2st_ref[...], p_hi) + _dot(c2st_ref[...], p_lo))

    p_w = []
    for u in subs:
        dpos = tcols[u](band) - (starts[u] + rowi(band))
        bias_w = jnp.where((dpos >= 0) & (dpos < WINDOW), 0.0, NEG_INF)
        p_w.append(_softmax_cols(s_w[u] + tile_r(bias_w)))
    o_win = [_dot(vwt_ref[:, pl.ds(starts[u], band)], p_w[u].astype(BF16)) for u in subs]

    q4a = []
    for u in subs:
        jj = rowi(n_slc)
        cur = tcols[u](n_slc) // SEL_BLOCK
        valid = jj <= cur
        forced = (jj == 0) | (jj == cur) | (jj == cur - 1)
        score = jnp.where(forced, SEL_FORCED, jnp.where(valid, imp[u], -SEL_FORCED))
        sel = _topk_mask(score, min(SEL_TOPN, n_slc)) & valid
        bias = _bias_rows_to_cols(sel)
        q4a.append(jnp.concatenate([jnp.concatenate([h, bias], axis=1) for h in heads[u]], axis=0))

    def sel_body(kt, carry):
        k0 = pl.multiple_of(kt * SEL_KT, SEL_KT)
        s = [_dot_nt(ksa_ref[pl.ds(k0, SEL_KT), :], q4a[u]) for u in range(U)]
        return tuple(_online_update(s[u], vst_ref[:, pl.ds(k0, SEL_KT)], carry[u]) for u in range(U))

    n_full = base // SEL_KT
    init = (jnp.full((1, lanes), NEG_INF, F32), jnp.zeros((1, lanes), F32), jnp.zeros((HEAD_DIM, lanes), F32))
    carry = lax.fori_loop(0, n_full, sel_body, (init,) * U)
    k0 = pl.multiple_of(n_full * SEL_KT, SEL_KT)
    gt = gt_ref[...]
    s_d = [_dot_nt(ksa_ref[pl.ds(k0, SEL_KT), :], q4a[u]) for u in subs]
    for u in subs:
        causal = jnp.where(k0 + rowi(SEL_KT) <= tcols[u](SEL_KT), 0.0, NEG_INF)
        _, l_s, acc_s = _online_update(s_d[u] + tile_r(causal), vst_ref[:, pl.ds(k0, SEL_KT)], carry[u])
        o_slc = acc_s * (1.0 / l_s)

        g = gt[:, u * tq:(u + 1) * tq]
        for r in range(R):
            sl = slice(r * tq, (r + 1) * tq)
            o_r = (g[r:r + 1, :] * o_cmp[u][:, sl] + g[R + r:R + r + 1, :] * o_slc[:, sl]
                   + g[2 * R + r:2 * R + r + 1, :] * o_win[u][:, sl])
            o_ref[u * tq:(u + 1) * tq, r * HEAD_DIM:(r + 1) * HEAD_DIM] = o_r.T.astype(BF16)


def _block_onehot(seq, block):
    assert seq // block <= HEAD_DIM
    e = np.zeros((seq, HEAD_DIM), np.float32)
    e[np.arange(seq), np.arange(seq) // block] = 1.0
    return jnp.asarray(e, BF16)


def _nsa_attention(proj3, kc, vc, gates_t, c2st):
    B, S, _ = proj3.shape
    G = NSA_GROUPS
    nc = kc.shape[2]
    tq = NSA_TQ * NSA_SUB
    assert S % tq == 0 and SEL_KT % tq == 0
    nq = S // tq
    hb = HEAD_DIM
    gw = NSA_REP * HEAD_DIM
    e_sel = _block_onehot(S, SEL_BLOCK)
    seq_spec = lambda off: pl.BlockSpec((None, S, hb), lambda b, g, i: (b, 0, off // hb + g))
    cmp_spec = pl.BlockSpec((None, None, nc, hb), lambda b, g, i: (b, g, 0, 0))
    return pl.pallas_call(
        functools.partial(_nsa_kernel, seq=S),
        grid=(B, G, nq),
        in_specs=[pl.BlockSpec((None, tq, gw), lambda b, g, i: (b, i, OFF_QA // gw + g)),
                  cmp_spec, cmp_spec,
                  seq_spec(OFF_KSLC), seq_spec(OFF_VSLC), seq_spec(OFF_KWIN), seq_spec(OFF_VWIN),
                  pl.BlockSpec((GATE_ROWS, tq), lambda b, g, i: (g, b * nq + i)),
                  pl.BlockSpec(c2st.shape, lambda b, g, i: (0, 0)),
                  pl.BlockSpec(e_sel.shape, lambda b, g, i: (0, 0))],
        out_specs=pl.BlockSpec((None, tq, gw), lambda b, g, i: (b, i, g)),
        out_shape=jax.ShapeDtypeStruct((B, S, NSA_WIDTH), BF16),
        scratch_shapes=[pltpu.VMEM((S, 2 * hb), BF16), pltpu.VMEM((hb, S), BF16), pltpu.VMEM((hb, S), BF16),
                        pltpu.VMEM((hb, nc), BF16)],
        compiler_params=_cparams(("arbitrary", "arbitrary", "arbitrary")),
        name="nsa_attn",
    )(proj3, kc, vc, proj3, proj3, proj3, proj3, gates_t, c2st, e_sel)


def _moba_kernel(q_ref, k_ref, v_ref, e_ref, o_ref, ka_ref, vt_ref, kmh_ref, kml_ref, *, seq):
    blk = MOBA_BLOCK
    HP = MOBA_HEADS_PER_STEP
    i = pl.program_id(2)
    nb = seq // blk
    nbp = kmh_ref.shape[1]
    hsl = lambda h: slice(h * HEAD_DIM, (h + 1) * HEAD_DIM)

    @pl.when(i == 0)
    def _():
        for h in range(HP):
            _transpose_into(v_ref, h * HEAD_DIM, vt_ref.at[h], seq)
            ka_ref[h, :, 0:HEAD_DIM] = k_ref[:, hsl(h)]
            ka_ref[h, :, HEAD_DIM:2 * HEAD_DIM] = e_ref[...]
            means = [jnp.sum(k_ref[n * blk:(n + 1) * blk, hsl(h)].astype(F32), axis=0, keepdims=True) * (1.0 / blk)
                     for n in range(nb)]
            if nbp > nb:
                means.append(jnp.zeros((nbp - nb, HEAD_DIM), F32))
            hi, lo = _split_hi_lo(jnp.concatenate(means, axis=0))
            kmh_ref[h] = hi
            kml_ref[h] = lo

    own0 = pl.multiple_of(i * blk, blk)
    past = lax.broadcasted_iota(I32, (nbp, blk), 0) < i
    causal = jnp.where(lax.broadcasted_iota(I32, (blk, blk), 0) <= lax.broadcasted_iota(I32, (blk, blk), 1), 0.0, NEG_INF)
    hs = range(HP)
    q = [q_ref[:, hsl(h)] for h in hs]
    gate = [_dot_nt(kmh_ref[h], q[h]) + _dot_nt(kml_ref[h], q[h]) for h in hs]
    s_own = [_dot_nt(k_ref[pl.ds(own0, blk), hsl(h)], q[h]) for h in hs]
    m_own = [jnp.max(s_own[h] + causal, axis=0, keepdims=True) for h in hs]
    p_own = [jnp.exp2(s_own[h] + causal - m_own[h]) for h in hs]
    carry = [(m_own[h], jnp.sum(p_own[h], axis=0, keepdims=True),
              _dot(vt_ref[h, :, pl.ds(own0, blk)], p_own[h].astype(BF16))) for h in hs]
    qa = []
    for h in hs:
        sel = _topk_mask(jnp.where(past, gate[h], -SEL_FORCED), min(MOBA_TOPK, nb)) & past
        qa.append(jnp.concatenate([q[h], _bias_rows_to_cols(sel)], axis=1))

    kt = 2 * blk

    def body(j, carry):
        k0 = pl.multiple_of(j * kt, kt)
        s = [_dot_nt(ka_ref[h, pl.ds(k0, kt), :], qa[h]) for h in range(HP)]
        return tuple(_online_update(s[h], vt_ref[h, :, pl.ds(k0, kt)], carry[h]) for h in range(HP))

    carry = lax.fori_loop(0, (i + 1) // 2, body, tuple(carry))
    for h in range(HP):
        _, l, acc = carry[h]
        o_ref[:, hsl(h)] = (acc * (1.0 / l)).T.astype(BF16)


def _moba_attention(proj3):
    B, S, _ = proj3.shape
    HP = MOBA_HEADS_PER_STEP
    blk = MOBA_BLOCK
    hb = HEAD_DIM
    assert (S // blk) % 2 == 0 and MOBA_HEADS % HP == 0
    nbp = max(8, -(-(S // blk) // 8) * 8)
    e_blk = _block_onehot(S, blk)
    wide = HP * hb
    seq_spec = lambda off: pl.BlockSpec((None, S, wide), lambda b, h, i: (b, 0, off // wide + h))
    return pl.pallas_call(
        functools.partial(_moba_kernel, seq=S),
        grid=(B, MOBA_HEADS // HP, S // blk),
        in_specs=[pl.BlockSpec((None, blk, wide), lambda b, h, i: (b, i, OFF_QB // wide + h)),
                  seq_spec(OFF_KB), seq_spec(OFF_VB),
                  pl.BlockSpec(e_blk.shape, lambda b, h, i: (0, 0))],
        out_specs=pl.BlockSpec((None, blk, wide), lambda b, h, i: (b, i, h)),
        out_shape=jax.ShapeDtypeStruct((B, S, MOBA_WIDTH), BF16),
        scratch_shapes=[pltpu.VMEM((HP, S, 2 * hb), BF16), pltpu.VMEM((HP, hb, S), BF16),
                        pltpu.VMEM((HP, nbp, hb), BF16), pltpu.VMEM((HP, nbp, hb), BF16)],
        compiler_params=_cparams(("arbitrary", "arbitrary", "arbitrary")),
        name="moba_attn",
    )(proj3, proj3, proj3, e_blk)


def _merge_proj_kernel(oa_ref, ob_ref, wa_ref, wb_ref, ga_ref, gb_ref, o_ref):
    a = _dot(oa_ref[...], wa_ref[...])
    b = _dot(ob_ref[...], wb_ref[...])
    o_ref[...] = (ga_ref[...].astype(F32) * a + gb_ref[...].astype(F32) * b).astype(BF16)


def _merge_proj(oa, ob, wa, wb, proj, tm):
    T = oa.shape[0]
    D = wa.shape[1]
    tn = min(COL_TILE, D)
    g0 = PLAIN_END // tn
    return pl.pallas_call(
        _merge_proj_kernel,
        grid=(T // tm, D // tn),
        in_specs=[pl.BlockSpec((tm, NSA_WIDTH), lambda i, j: (i, 0)),
                  pl.BlockSpec((tm, MOBA_WIDTH), lambda i, j: (i, 0)),
                  pl.BlockSpec((NSA_WIDTH, tn), lambda i, j: (0, j)),
                  pl.BlockSpec((MOBA_WIDTH, tn), lambda i, j: (0, j)),
                  pl.BlockSpec((tm, tn), lambda i, j: (i, g0 + j)),
                  pl.BlockSpec((tm, tn), lambda i, j: (i, g0 + D // tn + j))],
        out_specs=pl.BlockSpec((tm, tn), lambda i, j: (i, j)),
        out_shape=jax.ShapeDtypeStruct((T, D), BF16),
        compiler_params=_cparams(("arbitrary", "arbitrary")),
        name="merge_proj",
    )(oa, ob, wa, wb, proj, proj)


def _out_router_kernel(x_ref, mix_ref, wo_ref, g_ref, wrh_ref, wrl_ref, br_ref, x1_ref, h2_ref, te_ref, tw_ref):
    tm = x_ref.shape[0]
    E = wrh_ref.shape[0]
    x1 = x_ref[...] + _dot(mix_ref[...], wo_ref[...])
    x1_ref[...] = x1
    h2 = x1 * lax.rsqrt(jnp.mean(x1 * x1, axis=-1, keepdims=True) + NORM_EPS) * g_ref[...]
    h2_ref[...] = h2
    h_hi, h_lo = _split_hi_lo(h2)
    logits = (_dot_nt(wrh_ref[...], h_hi) + _dot_nt(wrh_ref[...], h_lo) + _dot_nt(wrl_ref[...], h_hi)
              + br_ref[...])
    e_idx = lax.broadcasted_iota(I32, (E, tm), 0)
    picks = _topk_rows(logits, e_idx, E, TOP_K)
    vals = [p[0] for p in picks]
    exps = [jnp.exp(v - vals[0]) for v in vals]
    inv = 1.0 / (exps[0] + exps[1] + exps[2] + exps[3])
    te_ref[...] = jnp.concatenate([p[1] for p in picks] + [jnp.zeros((8 - TOP_K, tm), I32)], axis=0)
    w_t = jnp.concatenate([e * inv for e in exps] + [jnp.zeros((HEAD_DIM - TOP_K, tm), F32)], axis=0)
    tw_ref[...] = w_t.T


def _out_router(x2, mix, wo, g, wrh, wrl, br, tm):
    T, D = x2.shape
    E = wrh.shape[0]
    full = lambda a: pl.BlockSpec(a.shape, lambda i: (0,) * a.ndim)
    return pl.pallas_call(
        _out_router_kernel,
        grid=(T // tm,),
        in_specs=[pl.BlockSpec((tm, D), lambda i: (i, 0)), pl.BlockSpec((tm, D), lambda i: (i, 0)),
                  full(wo), full(g), full(wrh), full(wrl), full(br)],
        out_specs=[pl.BlockSpec((tm, D), lambda i: (i, 0)), pl.BlockSpec((tm, D), lambda i: (i, 0)),
                   pl.BlockSpec((8, tm), lambda i: (0, i)), pl.BlockSpec((tm, HEAD_DIM), lambda i: (i, 0))],
        out_shape=[jax.ShapeDtypeStruct((T, D), F32), jax.ShapeDtypeStruct((T, D), F32),
                   jax.ShapeDtypeStruct((8, T), I32), jax.ShapeDtypeStruct((T, HEAD_DIM), F32)],
        compiler_params=_cparams(("arbitrary",)),
        name="out_router",
    )(x2, mix, wo, g, wrh, wrl, br)


def _route_kernel(te_ref, dest_ref, blke_ref, padlo_ref, padhi_ref, run_ref, pstart_ref):
    ph = pl.program_id(0)
    i = pl.program_id(1)
    E = run_ref.shape[0]
    tm = te_ref.shape[1]
    nbp = blke_ref.shape[1]
    te = te_ref[...]
    e_idx = lax.broadcasted_iota(I32, (E, tm), 0)
    ohs = [te[k:k + 1, :] == e_idx for k in range(TOP_K)]
    oh = jnp.where(ohs[0] | ohs[1] | ohs[2] | ohs[3], 1.0, 0.0)
    tile_cnt = jnp.sum(oh, axis=1, keepdims=True)

    @pl.when((ph == 0) & (i == 0))
    def _():
        run_ref[...] = jnp.zeros(run_ref.shape, F32)

    @pl.when(ph == 0)
    def _():
        run_ref[...] = run_ref[...] + tile_cnt
        dest_ref[...] = jnp.zeros(dest_ref.shape, I32)

    @pl.when((ph == 1) & (i == 0))
    def _():
        counts = run_ref[...]
        padded = jnp.floor((counts + (ROW_BLOCK - 1)) * (1.0 / ROW_BLOCK)) * ROW_BLOCK
        row = lax.broadcasted_iota(I32, counts.shape, 0)
        incl = padded
        sh = 1
        while sh < E:
            incl = incl + jnp.where(row >= sh, pltpu.roll(incl, sh, 0), 0.0)
            sh *= 2
        pstart_ref[...] = incl - padded
        padlo_ref[...] = (incl - padded + counts).astype(I32)
        padhi_ref[...] = incl.astype(I32)
        blk_start = (lax.broadcasted_iota(I32, (E, nbp), 1) * ROW_BLOCK).astype(F32)
        blke_ref[...] = jnp.sum(jnp.where(incl[:, 0:1] <= blk_start, 1, 0), axis=0, keepdims=True).astype(I32)
        run_ref[...] = jnp.zeros(run_ref.shape, F32)

    @pl.when(ph == 1)
    def _():
        upper = jnp.where(lax.broadcasted_iota(I32, (tm, tm), 0) < lax.broadcasted_iota(I32, (tm, tm), 1), 1.0, 0.0)
        before = _dot(oh.astype(BF16), upper.astype(BF16))
        val = before + pstart_ref[:, 0:1] + run_ref[:, 0:1]
        rows = [jnp.sum(jnp.where(ohs[k], val, 0.0), axis=0, keepdims=True) for k in range(TOP_K)]
        dest_ref[...] = jnp.concatenate(rows + [jnp.zeros((8 - TOP_K, tm), F32)], axis=0).astype(I32)
        run_ref[...] = run_ref[...] + tile_cnt


def _route(top_e, n_experts, n_blk_pad, tm):
    T = top_e.shape[1]
    return pl.pallas_call(
        _route_kernel,
        grid=(2, T // tm),
        in_specs=[pl.BlockSpec((8, tm), lambda ph, i: (0, i))],
        out_specs=[pl.BlockSpec((8, tm), lambda ph, i: (0, i * ph)),
                   pl.BlockSpec((1, n_blk_pad), lambda ph, i: (0, 0)),
                   pl.BlockSpec((n_experts, HEAD_DIM), lambda ph, i: (0, 0)),
                   pl.BlockSpec((n_experts, HEAD_DIM), lambda ph, i: (0, 0))],
        out_shape=[jax.ShapeDtypeStruct((8, T), I32), jax.ShapeDtypeStruct((1, n_blk_pad), I32),
                   jax.ShapeDtypeStruct((n_experts, HEAD_DIM), I32), jax.ShapeDtypeStruct((n_experts, HEAD_DIM), I32)],
        scratch_shapes=[pltpu.VMEM((n_experts, HEAD_DIM), F32), pltpu.VMEM((n_experts, HEAD_DIM), F32)],
        compiler_params=_cparams(("arbitrary", "arbitrary")),
        name="route",
    )(top_e)


def _row_copy(src_ref, src_row, dst_ref, dst_row, sem):
    return pltpu.make_async_copy(src_ref.at[pl.ds(src_row, 1), :], dst_ref.at[pl.ds(dst_row, 1), :], sem)


def _dispatch_kernel(pad_ref, dest_ref, h_ref, xpad_ref, zero_ref, sem, zsem):
    tm = h_ref.shape[0]

    @pl.when(pl.program_id(0) == 0)
    def _():
        zero_ref[...] = jnp.zeros(zero_ref.shape, F32)

        def per_expert(e, _):
            lo, hi = pad_ref[0, e], pad_ref[1, e]

            def issue(r, _):
                _row_copy(zero_ref, 0, xpad_ref, r, zsem).start()
                return 0
            lax.fori_loop(lo, hi, issue, 0)

            def drain(r, _):
                _row_copy(zero_ref, 0, xpad_ref, 0, zsem).wait()
                return 0
            lax.fori_loop(lo, hi, drain, 0)
            return 0
        lax.fori_loop(0, pad_ref.shape[1], per_expert, 0)

    def issue(t, _):
        for k in range(TOP_K):
            _row_copy(h_ref, t, xpad_ref, dest_ref[k, t], sem).start(priority=k % 2)
        return 0
    lax.fori_loop(0, tm, issue, 0)

    def drain(t, _):
        for k in range(TOP_K):
            _row_copy(h_ref, 0, xpad_ref, 0, sem).wait()
        return 0
    lax.fori_loop(0, tm, drain, 0)


def _dispatch(pads, dest, h2, n_rows, tm):
    T, D = h2.shape
    grid_spec = pltpu.PrefetchScalarGridSpec(
        num_scalar_prefetch=1,
        grid=(T // tm,),
        in_specs=[pl.BlockSpec((8, tm), lambda i, pads: (0, i), memory_space=pltpu.SMEM),
                  pl.BlockSpec((tm, D), lambda i, pads: (i, 0))],
        out_specs=pl.BlockSpec(memory_space=pl.ANY),
        scratch_shapes=[pltpu.VMEM((8, D), F32), pltpu.SemaphoreType.DMA(()), pltpu.SemaphoreType.DMA(())])
    return pl.pallas_call(
        _dispatch_kernel,
        grid_spec=grid_spec,
        out_shape=jax.ShapeDtypeStruct((n_rows, D), F32),
        compiler_params=_cparams(("arbitrary",)),
        name="dispatch",
    )(pads, dest, h2)


def _expert_changed(be_ref, m):
    prev = be_ref[jnp.maximum(m - 1, 0)]
    return (m == 0) | (be_ref[m] != prev)


def _expert_up_kernel(be_ref, x_ref, wg_ref, wl_ref, bg_ref, bl_ref, act_ref, wgb_ref, wlb_ref, *, n_experts):
    m = pl.program_id(1)

    @pl.when(_expert_changed(be_ref, m))
    def _():
        wgb_ref[...] = wg_ref[...].astype(BF16)
        wlb_ref[...] = wl_ref[...].astype(BF16)

    @pl.when(be_ref[m] < n_experts)
    def _():
        x = x_ref[...].astype(BF16)
        gate = jnp.minimum(_dot(x, wgb_ref[...]) + bg_ref[...], SWIGLU_LIMIT)
        lin = jnp.clip(_dot(x, wlb_ref[...]) + bl_ref[...], -SWIGLU_LIMIT, SWIGLU_LIMIT)
        act_ref[...] = (gate * _sigmoid(SWIGLU_ALPHA * gate) * (lin + 1.0)).astype(BF16)

    @pl.when(be_ref[m] >= n_experts)
    def _():
        act_ref[...] = jnp.zeros(act_ref.shape, BF16)


def _expert_up(blk_e, x_pad, w_gu, b_gu, n_blk, tn):
    P, D = x_pad.shape
    E = w_gu.shape[0]
    F = w_gu.shape[2] // 2
    nf = F // tn
    ex = lambda be, m: jnp.minimum(be[m], E - 1)
    grid_spec = pltpu.PrefetchScalarGridSpec(
        num_scalar_prefetch=1,
        grid=(nf, n_blk),
        in_specs=[pl.BlockSpec((ROW_BLOCK, D), lambda n, m, be: (m, 0)),
                  pl.BlockSpec((None, D, tn), lambda n, m, be: (ex(be, m), 0, n)),
                  pl.BlockSpec((None, D, tn), lambda n, m, be: (ex(be, m), 0, nf + n)),
                  pl.BlockSpec((None, 1, tn), lambda n, m, be: (ex(be, m), 0, n)),
                  pl.BlockSpec((None, 1, tn), lambda n, m, be: (ex(be, m), 0, nf + n))],
        out_specs=pl.BlockSpec((ROW_BLOCK, tn), lambda n, m, be: (m, n)),
        scratch_shapes=[pltpu.VMEM((D, tn), BF16), pltpu.VMEM((D, tn), BF16)])
    return pl.pallas_call(
        functools.partial(_expert_up_kernel, n_experts=E),
        grid_spec=grid_spec,
        out_shape=jax.ShapeDtypeStruct((P, F), BF16),
        compiler_params=_cparams(("arbitrary", "arbitrary")),
        name="expert_up",
    )(blk_e, x_pad, w_gu, w_gu, b_gu, b_gu)


def _expert_down_kernel(be_ref, a_ref, w_ref, b_ref, y_ref, wb_ref, *, n_experts):
    m = pl.program_id(1)

    @pl.when(_expert_changed(be_ref, m))
    def _():
        wb_ref[...] = w_ref[...].astype(BF16)

    @pl.when(be_ref[m] < n_experts)
    def _():
        y_ref[...] = _dot(a_ref[...], wb_ref[...]) + b_ref[...]

    @pl.when(be_ref[m] >= n_experts)
    def _():
        y_ref[...] = jnp.zeros(y_ref.shape, F32)


def _expert_down(blk_e, act, w_d, b_d, n_blk, tn):
    P, F = act.shape
    E, _, D = w_d.shape
    ex = lambda be, m: jnp.minimum(be[m], E - 1)
    grid_spec = pltpu.PrefetchScalarGridSpec(
        num_scalar_prefetch=1,
        grid=(D // tn, n_blk),
        in_specs=[pl.BlockSpec((ROW_BLOCK, F), lambda n, m, be: (m, 0)),
                  pl.BlockSpec((None, F, tn), lambda n, m, be: (ex(be, m), 0, n)),
                  pl.BlockSpec((None, 1, tn), lambda n, m, be: (ex(be, m), 0, n))],
        out_specs=pl.BlockSpec((ROW_BLOCK, tn), lambda n, m, be: (m, n)),
        scratch_shapes=[pltpu.VMEM((F, tn), BF16)])
    return pl.pallas_call(
        functools.partial(_expert_down_kernel, n_experts=E),
        grid_spec=grid_spec,
        out_shape=jax.ShapeDtypeStruct((P, D), F32),
        compiler_params=_cparams(("arbitrary", "arbitrary")),
        name="expert_down",
    )(blk_e, act, w_d, b_d)


def _combine_kernel(dest_ref, dest_next_ref, x1_ref, tw_ref, g_ref, ypad_ref, o_ref, ybuf_ref, sem):
    tm = x1_ref.shape[0]
    i = pl.program_id(0)
    slot = i % 2

    def gather(d_ref, s):
        def issue(t, _):
            for k in range(TOP_K):
                _row_copy(ypad_ref, d_ref[k, t], ybuf_ref.at[s, k], t, sem.at[s]).start(priority=k % 2)
            return 0
        lax.fori_loop(0, tm, issue, 0)

    @pl.when(i == 0)
    def _():
        gather(dest_ref, 0)

    @pl.when(i + 1 < pl.num_programs(0))
    def _():
        gather(dest_next_ref, 1 - slot)

    def drain(t, _):
        for k in range(TOP_K):
            _row_copy(ypad_ref, 0, ybuf_ref.at[slot, k], 0, sem.at[slot]).wait()
        return 0
    lax.fori_loop(0, tm, drain, 0)

    tw = tw_ref[...]
    acc = x1_ref[...]
    for k in range(TOP_K):
        acc = acc + tw[:, k:k + 1] * ybuf_ref[slot, k]
    o_ref[...] = acc * lax.rsqrt(jnp.mean(acc * acc, axis=-1, keepdims=True) + NORM_EPS) * g_ref[...]


def _combine(dest, x1, top_w, g, y_pad, tm):
    T, D = x1.shape
    n = T // tm
    return pl.pallas_call(
        _combine_kernel,
        grid=(n,),
        in_specs=[pl.BlockSpec((8, tm), lambda i: (0, i), memory_space=pltpu.SMEM),
                  pl.BlockSpec((8, tm), lambda i: (0, jnp.minimum(i + 1, n - 1)), memory_space=pltpu.SMEM),
                  pl.BlockSpec((tm, D), lambda i: (i, 0)),
                  pl.BlockSpec((tm, HEAD_DIM), lambda i: (i, 0)),
                  pl.BlockSpec((1, D), lambda i: (0, 0)),
                  pl.BlockSpec(memory_space=pl.ANY)],
        out_specs=pl.BlockSpec((tm, D), lambda i: (i, 0)),
        out_shape=jax.ShapeDtypeStruct((T, D), F32),
        scratch_shapes=[pltpu.VMEM((2, TOP_K, tm, D), F32), pltpu.SemaphoreType.DMA((2,))],
        compiler_params=_cparams(("arbitrary",)),
        name="combine",
    )(dest, dest, x1, top_w, g, y_pad)


def _rope_tables(pos):
    inv_freq = ROPE_THETA ** (-jnp.arange(0, ROPE_DIM, 2, dtype=F32) / ROPE_DIM)
    ang = pos.astype(F32)[..., None] * inv_freq
    cos, sin = jnp.cos(ang), jnp.sin(ang)
    rest = HEAD_DIM - ROPE_DIM
    cos_t = jnp.concatenate([cos, cos, jnp.ones(cos.shape[:-1] + (rest,), F32)], axis=-1)
    sin_t = jnp.concatenate([-sin, sin, jnp.zeros(sin.shape[:-1] + (rest,), F32)], axis=-1)
    return cos_t, sin_t


def _cmp_to_slc_t(seq, nc_pad):
    n_cmp = (seq - CMP_BLOCK) // CMP_STRIDE + 1
    n_slc = seq // SEL_BLOCK
    cmp_start = np.arange(n_cmp) * CMP_STRIDE
    slc_start = np.arange(n_slc) * SEL_BLOCK
    overlap = np.clip(np.minimum(cmp_start[:, None] + CMP_BLOCK, slc_start[None, :] + SEL_BLOCK)
                      - np.maximum(cmp_start[:, None], slc_start[None, :]), 0, None)
    out = np.zeros((n_slc, nc_pad), np.float32)
    out[:, :n_cmp] = (overlap / CMP_BLOCK).T
    return jnp.asarray(out, BF16)


def _regroup_w_in(w):
    D = w.shape[0]
    o_kv = NSA_WIDTH
    o_gn = o_kv + 6 * KV_WIDTH
    o_b = o_gn + 3 * NSA_HEADS
    o_gm = o_b + 3 * MOBA_WIDTH
    kv = lambda s: w[:, o_kv + s * KV_WIDTH:o_kv + (s + 1) * KV_WIDTH]
    mb = lambda s: w[:, o_b + s * MOBA_WIDTH:o_b + (s + 1) * MOBA_WIDTH]
    cols = [w[:, :NSA_WIDTH], kv(2), kv(4), mb(0), mb(1), kv(0), kv(1), kv(3), kv(5), mb(2), w[:, o_gm:o_gm + 2 * D]]
    w_main = jnp.concatenate(cols, axis=1).astype(BF16)
    wg = w[:, o_gn:o_gn + 3 * NSA_HEADS]
    wg = wg.reshape(D, 3, NSA_GROUPS, NSA_REP).transpose(2, 1, 3, 0).reshape(NSA_GROUPS, 3 * NSA_REP, D)
    wg = jnp.pad(wg, ((0, 0), (0, GATE_ROWS - 3 * NSA_REP), (0, 0))).reshape(NSA_GROUPS * GATE_ROWS, D)
    return w_main, wg.astype(BF16)


def kernel(x, positions, g_attn_norm, w_in, pe_cmp_k, w_cmp_k1, w_cmp_k2, pe_cmp_v, w_cmp_v1, w_cmp_v2, w_proj_nsa, w_proj_moba, w_out, g_ffn_norm, w_router, b_router, w_gate_up, b_gate_up, w_down, b_down, g_final_norm):
    B, S, D = x.shape
    T = B * S
    E = w_router.shape[-1]
    F = w_down.shape[-2]
    depth = w_in.shape[0]
    assert depth == 1, "the final norm is fused into the single layer's combine step"
    assert S % MOBA_BLOCK == 0 and S >= WINDOW + NSA_TQ and (2 * D) % COL_TILE == 0 and T % 512 == 0
    nc = S // CMP_STRIDE
    tm_big = min(1024, T)

    cos_t, sin_t = _rope_tables(positions)
    cmp_end = np.arange(nc) * CMP_STRIDE + CMP_BLOCK - 1
    cmp_end = np.minimum(cmp_end, S - 1)
    cos_c, sin_c = cos_t[:, cmp_end], sin_t[:, cmp_end]
    c2st = _cmp_to_slc_t(S, nc)

    x2 = x.reshape(T, D)
    for layer in range(depth):
        w_main, w_gates_t = _regroup_w_in(w_in[layer])
        h, gates_t = _norm_gates(x2, g_attn_norm[layer].reshape(1, D), w_gates_t, tm_big)
        proj = _in_proj(h, w_main, cos_t.reshape(T, HEAD_DIM), sin_t.reshape(T, HEAD_DIM), tm_big)
        proj3 = proj.reshape(B, S, proj.shape[1])

        xc = proj3[:, :, OFF_KCMP:OFF_KCMP + 2 * KV_WIDTH].reshape(B, nc, CMP_STRIDE * 2 * KV_WIDTH)
        kc, vc = _compress(xc, pe_cmp_k[layer], w_cmp_k1[layer].astype(BF16), w_cmp_k2[layer].astype(BF16),
                           pe_cmp_v[layer], w_cmp_v1[layer].astype(BF16), w_cmp_v2[layer].astype(BF16), cos_c, sin_c)
        o_a = _nsa_attention(proj3, kc, vc, gates_t, c2st)
        o_b = _moba_attention(proj3)

        mix = _merge_proj(o_a.reshape(T, NSA_WIDTH), o_b.reshape(T, MOBA_WIDTH),
                          w_proj_nsa[layer].astype(BF16), w_proj_moba[layer].astype(BF16), proj, tm_big)
        wr_t = w_router[layer].T
        wr_hi = wr_t.astype(BF16)
        wr_lo = (wr_t - wr_hi.astype(F32)).astype(BF16)
        x1, h2, top_e, top_w = _out_router(x2, mix, w_out[layer].astype(BF16), g_ffn_norm[layer].reshape(1, D),
                                           wr_hi, wr_lo, b_router[layer].reshape(E, 1), 512)

        n_blk = (T * TOP_K) // ROW_BLOCK + E
        n_blk_pad = -(-n_blk // HEAD_DIM) * HEAD_DIM
        dest, blk_e, pad_lo, pad_hi = _route(top_e, E, n_blk_pad, 512)
        pads = jnp.stack([pad_lo[:, 0], pad_hi[:, 0]])
        x_pad = _dispatch(pads, dest, h2, n_blk * ROW_BLOCK, 512)
        blk_e = blk_e.reshape(n_blk_pad)
        act = _expert_up(blk_e, x_pad, w_gate_up[layer], b_gate_up[layer].reshape(E, 1, 2 * F), n_blk, min(1024, F))
        y_pad = _expert_down(blk_e, act, w_down[layer], b_down[layer].reshape(E, 1, D), n_blk, D)
        x2 = _combine(dest, x1, top_w, g_final_norm.reshape(1, D), y_pad, 256)
    return x2.reshape(B, S, D)
```

```python
import functools

import jax
import jax.numpy as jnp
import numpy as np
from jax import lax
from jax.experimental import pallas as pl
from jax.experimental.pallas import tpu as pltpu

F32 = jnp.float32
BF16 = jnp.bfloat16
I32 = jnp.int32

HEAD_DIM = 128
ROPE_DIM = HEAD_DIM // 4
ROPE_HALF = ROPE_DIM // 2
ROPE_THETA = 500000.0
NORM_EPS = 1e-5
NEG_INF = -1e30
REMOVED = -3e38
SEL_FORCED = 1e9

NSA_HEADS = 8
NSA_GROUPS = 2
NSA_REP = NSA_HEADS // NSA_GROUPS
CMP_BLOCK = 32
CMP_STRIDE = 16
CMP_HIDDEN = 256
SEL_BLOCK = 64
SEL_TOPN = 16
WINDOW = 512
MOBA_HEADS = 8
MOBA_BLOCK = 256
MOBA_TOPK = 3
TOP_K = 4
SWIGLU_LIMIT = 7.0
SWIGLU_ALPHA = 1.702
ROW_BLOCK = 256

NSA_WIDTH = NSA_HEADS * HEAD_DIM
MOBA_WIDTH = MOBA_HEADS * HEAD_DIM
KV_WIDTH = NSA_GROUPS * HEAD_DIM
SCALE = HEAD_DIM ** -0.5
Q_SCALE = SCALE * 1.4426950408889634

COL_TILE = 512
OFF_QA = 0
OFF_KSLC = NSA_WIDTH
OFF_KWIN = OFF_KSLC + KV_WIDTH
OFF_QB = OFF_KWIN + KV_WIDTH
OFF_KB = OFF_QB + MOBA_WIDTH
ROPE_COLS = OFF_KB + MOBA_WIDTH
OFF_KCMP = ROPE_COLS
OFF_VCMP = OFF_KCMP + KV_WIDTH
OFF_VSLC = OFF_VCMP + KV_WIDTH
OFF_VWIN = OFF_VSLC + KV_WIDTH
OFF_VB = OFF_VWIN + KV_WIDTH
PLAIN_END = OFF_VB + MOBA_WIDTH
N_ROPE_TILES = ROPE_COLS // COL_TILE
N_PLAIN_TILES = (PLAIN_END - ROPE_COLS) // COL_TILE
Q_TILES = tuple(range(OFF_QA // COL_TILE, OFF_KSLC // COL_TILE)) + tuple(range(OFF_QB // COL_TILE, OFF_KB // COL_TILE))

VMEM_LIMIT = 56 * 1024 * 1024
NSA_TQ = 128
SEL_KT = 512
GATE_ROWS = 16
IN_PROJ_CHUNK = 256
NSA_SUB = 4
MOBA_HEADS_PER_STEP = 4


def _cparams(sem):
    return pltpu.CompilerParams(dimension_semantics=sem, vmem_limit_bytes=VMEM_LIMIT)


def _sigmoid(z):
    return 1.0 / (1.0 + jnp.exp(-z))


def _dot(a, b):
    return jnp.dot(a, b, preferred_element_type=F32)


def _dot_nt(a, b):
    return lax.dot_general(a, b, (((1,), (1,)), ((), ())), preferred_element_type=F32)


def _split_hi_lo(v):
    hi = v.astype(BF16)
    lo = (v - hi.astype(F32)).astype(BF16)
    return hi, lo


def _topk_rows(work, row_idx, n_rows, k):
    picks = []
    for _ in range(k):
        m = jnp.max(work, axis=0, keepdims=True)
        first = jnp.min(jnp.where(work == m, row_idx, n_rows), axis=0, keepdims=True)
        pick = row_idx == first
        picks.append((m, first, pick))
        work = jnp.where(pick, REMOVED, work)
    return picks


def _norm_gates_kernel(x_ref, g_ref, wgt_ref, h_ref, gt_ref):
    x = x_ref[...]
    h = x * lax.rsqrt(jnp.mean(x * x, axis=-1, keepdims=True) + NORM_EPS) * g_ref[...]
    hb = h.astype(BF16)
    h_ref[...] = hb
    gt_ref[...] = _sigmoid(_dot_nt(wgt_ref[...], hb))


def _norm_gates(x2, g, wgt, tm):
    T, D = x2.shape
    R = wgt.shape[0]
    return pl.pallas_call(
        _norm_gates_kernel,
        grid=(T // tm,),
        in_specs=[pl.BlockSpec((tm, D), lambda i: (i, 0)),
                  pl.BlockSpec((1, D), lambda i: (0, 0)),
                  pl.BlockSpec((R, D), lambda i: (0, 0))],
        out_specs=[pl.BlockSpec((tm, D), lambda i: (i, 0)),
                   pl.BlockSpec((R, tm), lambda i: (0, i))],
        out_shape=[jax.ShapeDtypeStruct((T, D), BF16), jax.ShapeDtypeStruct((R, T), F32)],
        compiler_params=_cparams(("arbitrary",)),
        name="norm_gates",
    )(x2, g, wgt)


def _rope_tile(xh, c, s, lane):
    rot = jnp.where(lane < ROPE_HALF, pltpu.roll(xh, HEAD_DIM - ROPE_HALF, 1), pltpu.roll(xh, ROPE_HALF, 1))
    return xh * c + rot * s


def _in_proj_kernel(h_ref, w_ref, c_ref, s_ref, o_ref):
    j = pl.program_id(1)
    tm = h_ref.shape[0]
    chunk = min(IN_PROJ_CHUNK, tm)

    def by_chunks(epilogue):
        for r0 in range(0, tm, chunk):
            rows = slice(r0, r0 + chunk)
            epilogue(rows, _dot(h_ref[rows, :], w_ref[...]))

    @pl.when(j < N_ROPE_TILES)
    def _():
        is_q = j == Q_TILES[0]
        for qt in Q_TILES[1:]:
            is_q = is_q | (j == qt)
        f = jnp.where(is_q, Q_SCALE, 1.0).astype(F32)
        lane = lax.broadcasted_iota(I32, (chunk, HEAD_DIM), 1)

        def rope(rows, acc):
            c = c_ref[rows, :] * f
            s = s_ref[rows, :] * f
            for hh in range(COL_TILE // HEAD_DIM):
                sl = slice(hh * HEAD_DIM, (hh + 1) * HEAD_DIM)
                o_ref[rows, sl] = _rope_tile(acc[:, sl], c, s, lane).astype(BF16)
        by_chunks(rope)

    @pl.when((j >= N_ROPE_TILES) & (j < N_ROPE_TILES + N_PLAIN_TILES))
    def _():
        def plain(rows, acc):
            o_ref[rows, :] = acc.astype(BF16)
        by_chunks(plain)

    @pl.when(j >= N_ROPE_TILES + N_PLAIN_TILES)
    def _():
        def gate(rows, acc):
            o_ref[rows, :] = _sigmoid(acc).astype(BF16)
        by_chunks(gate)


def _in_proj(h, w, cos_t, sin_t, tm):
    T, D = h.shape
    N = w.shape[1]
    return pl.pallas_call(
        _in_proj_kernel,
        grid=(T // tm, N // COL_TILE),
        in_specs=[pl.BlockSpec((tm, D), lambda i, j: (i, 0)),
                  pl.BlockSpec((D, COL_TILE), lambda i, j: (0, j)),
                  pl.BlockSpec((tm, HEAD_DIM), lambda i, j: (i, 0)),
                  pl.BlockSpec((tm, HEAD_DIM), lambda i, j: (i, 0))],
        out_specs=pl.BlockSpec((tm, COL_TILE), lambda i, j: (i, j)),
        out_shape=jax.ShapeDtypeStruct((T, N), BF16),
        compiler_params=_cparams(("arbitrary", "arbitrary")),
        name="in_proj",
    )(h, w, cos_t, sin_t)


def _gelu_tanh(x):
    return 0.5 * x * (1.0 + jnp.tanh(0.7978845608028654 * (x + 0.044715 * x * x * x)))


def _compress_kernel(x_ref, pek_ref, w1k_ref, w2k_ref, pev_ref, w1v_ref, w2v_ref, c_ref, s_ref, kc_ref, vc_ref):
    nc = x_ref.shape[0]
    half = CMP_STRIDE * HEAD_DIM
    tok_w = 2 * KV_WIDTH
    for which, (pe_ref, w1_ref, w2_ref, out_ref) in enumerate(
            ((pek_ref, w1k_ref, w2k_ref, kc_ref), (pev_ref, w1v_ref, w2v_ref, vc_ref))):
        for g in range(NSA_GROUPS):
            acc_a = jnp.zeros((nc, CMP_HIDDEN), F32)
            acc_b = jnp.zeros((nc, CMP_HIDDEN), F32)
            for l in range(CMP_STRIDE):
                off = l * tok_w + which * KV_WIDTH + g * HEAD_DIM
                xl = x_ref[:, off:off + HEAD_DIM].astype(F32)
                xa = (xl + pe_ref[l:l + 1, :]).astype(BF16)
                xb = (xl + pe_ref[CMP_STRIDE + l:CMP_STRIDE + l + 1, :]).astype(BF16)
                acc_a = acc_a + _dot(xa, w1_ref[l * HEAD_DIM:(l + 1) * HEAD_DIM, :])
                acc_b = acc_b + _dot(xb, w1_ref[half + l * HEAD_DIM:half + (l + 1) * HEAD_DIM, :])
            hid = _gelu_tanh(acc_a + pltpu.roll(acc_b, nc - 1, 0))
            out = _dot(hid.astype(BF16), w2_ref[...])
            if which == 0:
                lane = lax.broadcasted_iota(I32, out.shape, 1)
                out = _rope_tile(out, c_ref[...], s_ref[...], lane)
            out_ref[g] = out.astype(BF16)


def _compress(xc, pek, w1k, w2k, pev, w1v, w2v, cos_c, sin_c):
    B, nc, W = xc.shape
    full = lambda a: pl.BlockSpec(a.shape, lambda b: (0,) * a.ndim)
    out_sds = jax.ShapeDtypeStruct((B, NSA_GROUPS, nc, HEAD_DIM), BF16)
    out_spec = pl.BlockSpec((None, NSA_GROUPS, nc, HEAD_DIM), lambda b: (b, 0, 0, 0))
    return pl.pallas_call(
        _compress_kernel,
        grid=(B,),
        in_specs=[pl.BlockSpec((None, nc, W), lambda b: (b, 0, 0)),
                  full(pek), full(w1k), full(w2k), full(pev), full(w1v), full(w2v),
                  pl.BlockSpec((None, nc, HEAD_DIM), lambda b: (b, 0, 0)),
                  pl.BlockSpec((None, nc, HEAD_DIM), lambda b: (b, 0, 0))],
        out_specs=[out_spec, out_spec],
        out_shape=[out_sds, out_sds],
        compiler_params=_cparams(("arbitrary",)),
        name="compress",
    )(xc, pek, w1k, w2k, pev, w1v, w2v, cos_c, sin_c)


def _transpose_into(src_ref, col0, dst_ref, n_rows):
    def body(b, _):
        r0 = pl.multiple_of(b * HEAD_DIM, HEAD_DIM)
        blk = src_ref[pl.ds(r0, HEAD_DIM), col0:col0 + HEAD_DIM].astype(F32)
        dst_ref[:, pl.ds(r0, HEAD_DIM)] = blk.T.astype(BF16)
        return 0
    lax.fori_loop(0, n_rows // HEAD_DIM, body, 0)


def _topk_mask(score, k):
    n, L = score.shape
    sub = 8
    assert n % sub == 0
    groups = n // sub
    segs = [score[sub * g:sub * (g + 1), :] for g in range(groups)]
    sub_idx = lax.broadcasted_iota(I32, (sub, L), 0)
    later = [jnp.where(sub_idx > r, 1.0, 0.0) for r in range(sub)]
    ranks = [jnp.zeros((sub, L), F32) for _ in range(groups)]
    for i in range(n):
        gi, ri = divmod(i, sub)
        row = score[i:i + 1, :]
        for g in range(groups):
            if g > gi:
                inc = jnp.where(row >= segs[g], 1.0, 0.0)
            elif g < gi:
                inc = jnp.where(row > segs[g], 1.0, 0.0)
            else:
                inc = jnp.where(row > segs[g], 1.0, jnp.where(row == segs[g], later[ri], 0.0))
            ranks[g] = ranks[g] + inc
    return jnp.concatenate(ranks, axis=0) < k


def _bias_rows_to_cols(sel, lanes_out=HEAD_DIM):
    n, L = sel.shape
    bias_t = jnp.where(sel, 0.0, NEG_INF)
    if n < lanes_out:
        bias_t = jnp.concatenate([bias_t, jnp.zeros((lanes_out - n, L), F32)], axis=0)
    return bias_t.T.astype(BF16)


def _softmax_cols(s, live=None):
    m = jnp.max(s, axis=0, keepdims=True)
    p = jnp.exp2(s - m)
    inv = 1.0 / jnp.sum(p, axis=0, keepdims=True)
    if live is not None:
        inv = jnp.where(live, inv, 0.0)
    return p * inv


def _online_update(s, vt, carry):
    m, l, acc = carry
    m_new = jnp.maximum(m, jnp.max(s, axis=0, keepdims=True))
    alpha = jnp.exp2(m - m_new)
    p = jnp.exp2(s - m_new)
    l = alpha * l + jnp.sum(p, axis=0, keepdims=True)
    acc = alpha * acc + _dot(vt, p.astype(BF16))
    return m_new, l, acc


def _nsa_kernel(q_ref, kc_ref, vc_ref, ks_ref, vs_ref, kw_ref, vw_ref, gt_ref, c2st_ref, e_ref, o_ref,
                ksa_ref, vst_ref, vwt_ref, vct_ref, *, seq):
    tq = NSA_TQ
    U = NSA_SUB
    R = NSA_REP
    i = pl.program_id(2)
    base = i * (U * tq)
    nc = kc_ref.shape[0]
    n_slc = c2st_ref.shape[0]
    lanes = R * tq
    band = WINDOW + tq
    tile_r = lambda a: jnp.concatenate([a] * R, axis=1)
    rowi = lambda n: lax.broadcasted_iota(I32, (n, tq), 0)

    @pl.when(i == 0)
    def _():
        _transpose_into(vs_ref, 0, vst_ref, seq)
        _transpose_into(vw_ref, 0, vwt_ref, seq)
        _transpose_into(vc_ref, 0, vct_ref, nc)
        ksa_ref[:, 0:HEAD_DIM] = ks_ref[...]
        ksa_ref[:, HEAD_DIM:2 * HEAD_DIM] = e_ref[...]

    subs = range(U)
    heads = [[q_ref[u * tq:(u + 1) * tq, r * HEAD_DIM:(r + 1) * HEAD_DIM] for r in range(R)] for u in subs]
    q4 = [jnp.concatenate(heads[u], axis=0) for u in subs]
    tcols = [lambda n, q0=base + u * tq: q0 + lax.broadcasted_iota(I32, (n, tq), 1) for u in subs]
    starts = [pl.multiple_of(jnp.maximum(base + u * tq - WINDOW, 0), HEAD_DIM) for u in subs]

    s_c = [_dot_nt(kc_ref[...], q4[u]) for u in subs]
    s_w = [_dot_nt(kw_ref[pl.ds(starts[u], band), :], q4[u]) for u in subs]

    p_c = []
    for u in subs:
        bias_c = jnp.where(rowi(nc) * CMP_STRIDE + (CMP_BLOCK - 1) <= tcols[u](nc), 0.0, NEG_INF)
        live_c = tcols[u](1) >= CMP_BLOCK - 1
        p_c.append(_softmax_cols(s_c[u] + tile_r(bias_c), tile_r(live_c)))
    o_cmp = [_dot(vct_ref[...], p_c[u].astype(BF16)) for u in subs]

    imp = []
    for u in subs:
        p_sum = p_c[u][:, 0:tq]
        for r in range(1, R):
            p_sum = p_sum + p_c[u][:, r * tq:(r + 1) * tq]
        p_hi, p_lo = _split_hi_lo(p_sum)
        imp.append(_dot(c2st_ref[...], p_hi) + _dot(c2st_ref[...], p_lo))

    p_w = []
    for u in subs:
        dpos = tcols[u](band) - (starts[u] + rowi(band))
        bias_w = jnp.where((dpos >= 0) & (dpos < WINDOW), 0.0, NEG_INF)
        p_w.append(_softmax_cols(s_w[u] + tile_r(bias_w)))
    o_win = [_dot(vwt_ref[:, pl.ds(starts[u], band)], p_w[u].astype(BF16)) for u in subs]

    q4a = []
    for u in subs:
        jj = rowi(n_slc)
        cur = tcols[u](n_slc) // SEL_BLOCK
        valid = jj <= cur
        forced = (jj == 0) | (jj == cur) | (jj == cur - 1)
        score = jnp.where(forced, SEL_FORCED, jnp.where(valid, imp[u], -SEL_FORCED))
        sel = _topk_mask(score, min(SEL_TOPN, n_slc)) & valid
        bias = _bias_rows_to_cols(sel)
        q4a.append(jnp.concatenate([jnp.concatenate([h, bias], axis=1) for h in heads[u]], axis=0))

    def sel_body(kt, carry):
        k0 = pl.multiple_of(kt * SEL_KT, SEL_KT)
        s = [_dot_nt(ksa_ref[pl.ds(k0, SEL_KT), :], q4a[u]) for u in range(U)]
        return tuple(_online_update(s[u], vst_ref[:, pl.ds(k0, SEL_KT)], carry[u]) for u in range(U))

    n_full = base // SEL_KT
    init = (jnp.full((1, lanes), NEG_INF, F32), jnp.zeros((1, lanes), F32), jnp.zeros((HEAD_DIM, lanes), F32))
    carry = lax.fori_loop(0, n_full, sel_body, (init,) * U)
    k0 = pl.multiple_of(n_full * SEL_KT, SEL_KT)
    gt = gt_ref[...]
    s_d = [_dot_nt(ksa_ref[pl.ds(k0, SEL_KT), :], q4a[u]) for u in subs]
    for u in subs:
        causal = jnp.where(k0 + rowi(SEL_KT) <= tcols[u](SEL_KT), 0.0, NEG_INF)
        _, l_s, acc_s = _online_update(s_d[u] + tile_r(causal), vst_ref[:, pl.ds(k0, SEL_KT)], carry[u])
        o_slc = acc_s * (1.0 / l_s)

        g = gt[:, u * tq:(u + 1) * tq]
        for r in range(R):
            sl = slice(r * tq, (r + 1) * tq)
            o_r = (g[r:r + 1, :] * o_cmp[u][:, sl] + g[R + r:R + r + 1, :] * o_slc[:, sl]
                   + g[2 * R + r:2 * R + r + 1, :] * o_win[u][:, sl])
            o_ref[u * tq:(u + 1) * tq, r * HEAD_DIM:(r + 1) * HEAD_DIM] = o_r.T.astype(BF16)


def _block_onehot(seq, block):
    assert seq // block <= HEAD_DIM
    e = np.zeros((seq, HEAD_DIM), np.float32)
    e[np.arange(seq), np.arange(seq) // block] = 1.0
    return jnp.asarray(e, BF16)


def _nsa_attention(proj3, kc, vc, gates_t, c2st):
    B, S, _ = proj3.shape
    G = NSA_GROUPS
    nc = kc.shape[2]
    tq = NSA_TQ * NSA_SUB
    assert S % tq == 0 and SEL_KT % tq == 0
    nq = S // tq
    hb = HEAD_DIM
    gw = NSA_REP * HEAD_DIM
    e_sel = _block_onehot(S, SEL_BLOCK)
    seq_spec = lambda off: pl.BlockSpec((None, S, hb), lambda b, g, i: (b, 0, off // hb + g))
    cmp_spec = pl.BlockSpec((None, None, nc, hb), lambda b, g, i: (b, g, 0, 0))
    return pl.pallas_call(
        functools.partial(_nsa_kernel, seq=S),
        grid=(B, G, nq),
        in_specs=[pl.BlockSpec((None, tq, gw), lambda b, g, i: (b, i, OFF_QA // gw + g)),
                  cmp_spec, cmp_spec,
                  seq_spec(OFF_KSLC), seq_spec(OFF_VSLC), seq_spec(OFF_KWIN), seq_spec(OFF_VWIN),
                  pl.BlockSpec((GATE_ROWS, tq), lambda b, g, i: (g, b * nq + i)),
                  pl.BlockSpec(c2st.shape, lambda b, g, i: (0, 0)),
                  pl.BlockSpec(e_sel.shape, lambda b, g, i: (0, 0))],
        out_specs=pl.BlockSpec((None, tq, gw), lambda b, g, i: (b, i, g)),
        out_shape=jax.ShapeDtypeStruct((B, S, NSA_WIDTH), BF16),
        scratch_shapes=[pltpu.VMEM((S, 2 * hb), BF16), pltpu.VMEM((hb, S), BF16), pltpu.VMEM((hb, S), BF16),
                        pltpu.VMEM((hb, nc), BF16)],
        compiler_params=_cparams(("arbitrary", "arbitrary", "arbitrary")),
        name="nsa_attn",
    )(proj3, kc, vc, proj3, proj3, proj3, proj3, gates_t, c2st, e_sel)


def _moba_kernel(q_ref, k_ref, v_ref, e_ref, o_ref, ka_ref, vt_ref, kmh_ref, kml_ref, *, seq):
    blk = MOBA_BLOCK
    HP = MOBA_HEADS_PER_STEP
    i = pl.program_id(2)
    nb = seq // blk
    nbp = kmh_ref.shape[1]
    hsl = lambda h: slice(h * HEAD_DIM, (h + 1) * HEAD_DIM)

    @pl.when(i == 0)
    def _():
        for h in range(HP):
            _transpose_into(v_ref, h * HEAD_DIM, vt_ref.at[h], seq)
            ka_ref[h, :, 0:HEAD_DIM] = k_ref[:, hsl(h)]
            ka_ref[h, :, HEAD_DIM:2 * HEAD_DIM] = e_ref[...]
            means = [jnp.sum(k_ref[n * blk:(n + 1) * blk, hsl(h)].astype(F32), axis=0, keepdims=True) * (1.0 / blk)
                     for n in range(nb)]
            if nbp > nb:
                means.append(jnp.zeros((nbp - nb, HEAD_DIM), F32))
            hi, lo = _split_hi_lo(jnp.concatenate(means, axis=0))
            kmh_ref[h] = hi
            kml_ref[h] = lo

    own0 = pl.multiple_of(i * blk, blk)
    past = lax.broadcasted_iota(I32, (nbp, blk), 0) < i
    causal = jnp.where(lax.broadcasted_iota(I32, (blk, blk), 0) <= lax.broadcasted_iota(I32, (blk, blk), 1), 0.0, NEG_INF)
    hs = range(HP)
    q = [q_ref[:, hsl(h)] for h in hs]
    gate = [_dot_nt(kmh_ref[h], q[h]) + _dot_nt(kml_ref[h], q[h]) for h in hs]
    s_own = [_dot_nt(k_ref[pl.ds(own0, blk), hsl(h)], q[h]) for h in hs]
    m_own = [jnp.max(s_own[h] + causal, axis=0, keepdims=True) for h in hs]
    p_own = [jnp.exp2(s_own[h] + causal - m_own[h]) for h in hs]
    carry = [(m_own[h], jnp.sum(p_own[h], axis=0, keepdims=True),
              _dot(vt_ref[h, :, pl.ds(own0, blk)], p_own[h].astype(BF16))) for h in hs]
    qa = []
    for h in hs:
        sel = _topk_mask(jnp.where(past, gate[h], -SEL_FORCED), min(MOBA_TOPK, nb)) & past
        qa.append(jnp.concatenate([q[h], _bias_rows_to_cols(sel)], axis=1))

    kt = 2 * blk

    def body(j, carry):
        k0 = pl.multiple_of(j * kt, kt)
        s = [_dot_nt(ka_ref[h, pl.ds(k0, kt), :], qa[h]) for h in range(HP)]
        return tuple(_online_update(s[h], vt_ref[h, :, pl.ds(k0, kt)], carry[h]) for h in range(HP))

    carry = lax.fori_loop(0, (i + 1) // 2, body, tuple(carry))
    for h in range(HP):
        _, l, acc = carry[h]
        o_ref[:, hsl(h)] = (acc * (1.0 / l)).T.astype(BF16)


def _moba_attention(proj3):
    B, S, _ = proj3.shape
    HP = MOBA_HEADS_PER_STEP
    blk = MOBA_BLOCK
    hb = HEAD_DIM
    assert (S // blk) % 2 == 0 and MOBA_HEADS % HP == 0
    nbp = max(8, -(-(S // blk) // 8) * 8)
    e_blk = _block_onehot(S, blk)
    wide = HP * hb
    seq_spec = lambda off: pl.BlockSpec((None, S, wide), lambda b, h, i: (b, 0, off // wide + h))
    return pl.pallas_call(
        functools.partial(_moba_kernel, seq=S),
        grid=(B, MOBA_HEADS // HP, S // blk),
        in_specs=[pl.BlockSpec((None, blk, wide), lambda b, h, i: (b, i, OFF_QB // wide + h)),
                  seq_spec(OFF_KB), seq_spec(OFF_VB),
                  pl.BlockSpec(e_blk.shape, lambda b, h, i: (0, 0))],
        out_specs=pl.BlockSpec((None, blk, wide), lambda b, h, i: (b, i, h)),
        out_shape=jax.ShapeDtypeStruct((B, S, MOBA_WIDTH), BF16),
        scratch_shapes=[pltpu.VMEM((HP, S, 2 * hb), BF16), pltpu.VMEM((HP, hb, S), BF16),
                        pltpu.VMEM((HP, nbp, hb), BF16), pltpu.VMEM((HP, nbp, hb), BF16)],
        compiler_params=_cparams(("arbitrary", "arbitrary", "arbitrary")),
        name="moba_attn",
    )(proj3, proj3, proj3, e_blk)


def _merge_proj_kernel(oa_ref, ob_ref, wa_ref, wb_ref, ga_ref, gb_ref, o_ref):
    a = _dot(oa_ref[...], wa_ref[...])
    b = _dot(ob_ref[...], wb_ref[...])
    o_ref[...] = (ga_ref[...].astype(F32) * a + gb_ref[...].astype(F32) * b).astype(BF16)


def _merge_proj(oa, ob, wa, wb, proj, tm):
    T = oa.shape[0]
    D = wa.shape[1]
    tn = min(COL_TILE, D)
    g0 = PLAIN_END // tn
    return pl.pallas_call(
        _merge_proj_kernel,
        grid=(T // tm, D // tn),
        in_specs=[pl.BlockSpec((tm, NSA_WIDTH), lambda i, j: (i, 0)),
                  pl.BlockSpec((tm, MOBA_WIDTH), lambda i, j: (i, 0)),
                  pl.BlockSpec((NSA_WIDTH, tn), lambda i, j: (0, j)),
                  pl.BlockSpec((MOBA_WIDTH, tn), lambda i, j: (0, j)),
                  pl.BlockSpec((tm, tn), lambda i, j: (i, g0 + j)),
                  pl.BlockSpec((tm, tn), lambda i, j: (i, g0 + D // tn + j))],
        out_specs=pl.BlockSpec((tm, tn), lambda i, j: (i, j)),
        out_shape=jax.ShapeDtypeStruct((T, D), BF16),
        compiler_params=_cparams(("arbitrary", "arbitrary")),
        name="merge_proj",
    )(oa, ob, wa, wb, proj, proj)


def _out_router_kernel(x_ref, mix_ref, wo_ref, g_ref, wrh_ref, wrl_ref, br_ref, x1_ref, h2_ref, te_ref, tw_ref):
    tm = x_ref.shape[0]
    E = wrh_ref.shape[0]
    x1 = x_ref[...] + _dot(mix_ref[...], wo_ref[...])
    x1_ref[...] = x1
    h2 = x1 * lax.rsqrt(jnp.mean(x1 * x1, axis=-1, keepdims=True) + NORM_EPS) * g_ref[...]
    h2_ref[...] = h2
    h_hi, h_lo = _split_hi_lo(h2)
    logits = (_dot_nt(wrh_ref[...], h_hi) + _dot_nt(wrh_ref[...], h_lo) + _dot_nt(wrl_ref[...], h_hi)
              + br_ref[...])
    e_idx = lax.broadcasted_iota(I32, (E, tm), 0)
    picks = _topk_rows(logits, e_idx, E, TOP_K)
    vals = [p[0] for p in picks]
    exps = [jnp.exp(v - vals[0]) for v in vals]
    inv = 1.0 / (exps[0] + exps[1] + exps[2] + exps[3])
    te_ref[...] = jnp.concatenate([p[1] for p in picks] + [jnp.zeros((8 - TOP_K, tm), I32)], axis=0)
    w_t = jnp.concatenate([e * inv for e in exps] + [jnp.zeros((HEAD_DIM - TOP_K, tm), F32)], axis=0)
    tw_ref[...] = w_t.T


def _out_router(x2, mix, wo, g, wrh, wrl, br, tm):
    T, D = x2.shape
    E = wrh.shape[0]
    full = lambda a: pl.BlockSpec(a.shape, lambda i: (0,) * a.ndim)
    return pl.pallas_call(
        _out_router_kernel,
        grid=(T // tm,),
        in_specs=[pl.BlockSpec((tm, D), lambda i: (i, 0)), pl.BlockSpec((tm, D), lambda i: (i, 0)),
                  full(wo), full(g), full(wrh), full(wrl), full(br)],
        out_specs=[pl.BlockSpec((tm, D), lambda i: (i, 0)), pl.BlockSpec((tm, D), lambda i: (i, 0)),
                   pl.BlockSpec((8, tm), lambda i: (0, i)), pl.BlockSpec((tm, HEAD_DIM), lambda i: (i, 0))],
        out_shape=[jax.ShapeDtypeStruct((T, D), F32), jax.ShapeDtypeStruct((T, D), F32),
                   jax.ShapeDtypeStruct((8, T), I32), jax.ShapeDtypeStruct((T, HEAD_DIM), F32)],
        compiler_params=_cparams(("arbitrary",)),
        name="out_router",
    )(x2, mix, wo, g, wrh, wrl, br)


def _route_kernel(te_ref, dest_ref, blke_ref, padlo_ref, padhi_ref, run_ref, pstart_ref):
    ph = pl.program_id(0)
    i = pl.program_id(1)
    E = run_ref.shape[0]
    tm = te_ref.shape[1]
    nbp = blke_ref.shape[1]
    te = te_ref[...]
    e_idx = lax.broadcasted_iota(I32, (E, tm), 0)
    ohs = [te[k:k + 1, :] == e_idx for k in range(TOP_K)]
    oh = jnp.where(ohs[0] | ohs[1] | ohs[2] | ohs[3], 1.0, 0.0)
    tile_cnt = jnp.sum(oh, axis=1, keepdims=True)

    @pl.when((ph == 0) & (i == 0))
    def _():
        run_ref[...] = jnp.zeros(run_ref.shape, F32)

    @pl.when(ph == 0)
    def _():
        run_ref[...] = run_ref[...] + tile_cnt
        dest_ref[...] = jnp.zeros(dest_ref.shape, I32)

    @pl.when((ph == 1) & (i == 0))
    def _():
        counts = run_ref[...]
        padded = jnp.floor((counts + (ROW_BLOCK - 1)) * (1.0 / ROW_BLOCK)) * ROW_BLOCK
        row = lax.broadcasted_iota(I32, counts.shape, 0)
        incl = padded
        sh = 1
        while sh < E:
            incl = incl + jnp.where(row >= sh, pltpu.roll(incl, sh, 0), 0.0)
            sh *= 2
        pstart_ref[...] = incl - padded
        padlo_ref[...] = (incl - padded + counts).astype(I32)
        padhi_ref[...] = incl.astype(I32)
        blk_start = (lax.broadcasted_iota(I32, (E, nbp), 1) * ROW_BLOCK).astype(F32)
        blke_ref[...] = jnp.sum(jnp.where(incl[:, 0:1] <= blk_start, 1, 0), axis=0, keepdims=True).astype(I32)
        run_ref[...] = jnp.zeros(run_ref.shape, F32)

    @pl.when(ph == 1)
    def _():
        upper = jnp.where(lax.broadcasted_iota(I32, (tm, tm), 0) < lax.broadcasted_iota(I32, (tm, tm), 1), 1.0, 0.0)
        before = _dot(oh.astype(BF16), upper.astype(BF16))
        val = before + pstart_ref[:, 0:1] + run_ref[:, 0:1]
        rows = [jnp.sum(jnp.where(ohs[k], val, 0.0), axis=0, keepdims=True) for k in range(TOP_K)]
        dest_ref[...] = jnp.concatenate(rows + [jnp.zeros((8 - TOP_K, tm), F32)], axis=0).astype(I32)
        run_ref[...] = run_ref[...] + tile_cnt


def _route(top_e, n_experts, n_blk_pad, tm):
    T = top_e.shape[1]
    return pl.pallas_call(
        _route_kernel,
        grid=(2, T // tm),
        in_specs=[pl.BlockSpec((8, tm), lambda ph, i: (0, i))],
        out_specs=[pl.BlockSpec((8, tm), lambda ph, i: (0, i * ph)),
                   pl.BlockSpec((1, n_blk_pad), lambda ph, i: (0, 0)),
                   pl.BlockSpec((n_experts, HEAD_DIM), lambda ph, i: (0, 0)),
                   pl.BlockSpec((n_experts, HEAD_DIM), lambda ph, i: (0, 0))],
        out_shape=[jax.ShapeDtypeStruct((8, T), I32), jax.ShapeDtypeStruct((1, n_blk_pad), I32),
                   jax.ShapeDtypeStruct((n_experts, HEAD_DIM), I32), jax.ShapeDtypeStruct((n_experts, HEAD_DIM), I32)],
        scratch_shapes=[pltpu.VMEM((n_experts, HEAD_DIM), F32), pltpu.VMEM((n_experts, HEAD_DIM), F32)],
        compiler_params=_cparams(("arbitrary", "arbitrary")),
        name="route",
    )(top_e)


def _row_copy(src_ref, src_row, dst_ref, dst_row, sem):
    return pltpu.make_async_copy(src_ref.at[pl.ds(src_row, 1), :], dst_ref.at[pl.ds(dst_row, 1), :], sem)


def _dispatch_kernel(pad_ref, dest_ref, h_ref, xpad_ref, zero_ref, sem, zsem):
    tm = h_ref.shape[0]

    @pl.when(pl.program_id(0) == 0)
    def _():
        zero_ref[...] = jnp.zeros(zero_ref.shape, F32)

        def per_expert(e, _):
            lo, hi = pad_ref[0, e], pad_ref[1, e]

            def issue(r, _):
                _row_copy(zero_ref, 0, xpad_ref, r, zsem).start()
                return 0
            lax.fori_loop(lo, hi, issue, 0)

            def drain(r, _):
                _row_copy(zero_ref, 0, xpad_ref, 0, zsem).wait()
                return 0
            lax.fori_loop(lo, hi, drain, 0)
            return 0
        lax.fori_loop(0, pad_ref.shape[1], per_expert, 0)

    def issue(t, _):
        for k in range(TOP_K):
            _row_copy(h_ref, t, xpad_ref, dest_ref[k, t], sem).start(priority=k % 2)
        return 0
    lax.fori_loop(0, tm, issue, 0)

    def drain(t, _):
        for k in range(TOP_K):
            _row_copy(h_ref, 0, xpad_ref, 0, sem).wait()
        return 0
    lax.fori_loop(0, tm, drain, 0)


def _dispatch(pads, dest, h2, n_rows, tm):
    T, D = h2.shape
    grid_spec = pltpu.PrefetchScalarGridSpec(
        num_scalar_prefetch=1,
        grid=(T // tm,),
        in_specs=[pl.BlockSpec((8, tm), lambda i, pads: (0, i), memory_space=pltpu.SMEM),
                  pl.BlockSpec((tm, D), lambda i, pads: (i, 0))],
        out_specs=pl.BlockSpec(memory_space=pl.ANY),
        scratch_shapes=[pltpu.VMEM((8, D), F32), pltpu.SemaphoreType.DMA(()), pltpu.SemaphoreType.DMA(())])
    return pl.pallas_call(
        _dispatch_kernel,
        grid_spec=grid_spec,
        out_shape=jax.ShapeDtypeStruct((n_rows, D), F32),
        compiler_params=_cparams(("arbitrary",)),
        name="dispatch",
    )(pads, dest, h2)


def _next_expert(blk_e, n_experts):
    after = jnp.searchsorted(blk_e, blk_e, side="right")
    nxt = blk_e[jnp.minimum(after, blk_e.shape[0] - 1)]
    return jnp.where((after < blk_e.shape[0]) & (nxt < n_experts), nxt, -1).astype(I32)


def _stream_expert_weights(be_ref, nxt_ref, w_hbm, col_offsets, wbuf_ref, wb_refs, sem, cnt_ref, n_experts):
    n, m = pl.program_id(0), pl.program_id(1)
    tn = wb_refs[0].shape[1]
    e = be_ref[m]
    live = e < n_experts
    first = live & ((m == 0) | (be_ref[jnp.maximum(m - 1, 0)] != e))

    def copies(ex, col_tile, slot):
        c0 = pl.multiple_of(col_tile * tn, tn)
        return [pltpu.make_async_copy(w_hbm.at[ex, :, pl.ds(off + c0, tn)], wbuf_ref.at[slot, g], sem.at[slot, g])
                for g, off in enumerate(col_offsets)]

    @pl.when((n == 0) & (m == 0))
    def _():
        cnt_ref[0] = 0
        for c in copies(e, 0, 0):
            c.start()

    @pl.when(first)
    def _():
        slot = cnt_ref[0] % 2
        for c in copies(e, n, slot):
            c.wait()
        nxt = nxt_ref[m]

        @pl.when(nxt >= 0)
        def _():
            for c in copies(nxt, n, 1 - slot):
                c.start()

        @pl.when((nxt < 0) & (n + 1 < pl.num_programs(0)))
        def _():
            for c in copies(be_ref[0], n + 1, 1 - slot):
                c.start()

        for g, wb_ref in enumerate(wb_refs):
            wb_ref[...] = wbuf_ref[slot, g].astype(BF16)
        cnt_ref[0] = cnt_ref[0] + 1
    return live


def _expert_up_kernel(be_ref, nxt_ref, x_ref, w_hbm, bg_ref, bl_ref, act_ref, wbuf_ref, wgb_ref, wlb_ref, sem, cnt_ref,
                      *, n_experts, d_ff):
    live = _stream_expert_weights(be_ref, nxt_ref, w_hbm, (0, d_ff), wbuf_ref, (wgb_ref, wlb_ref), sem, cnt_ref,
                                  n_experts)

    @pl.when(live)
    def _():
        x = x_ref[...].astype(BF16)
        gate = jnp.minimum(_dot(x, wgb_ref[...]) + bg_ref[...], SWIGLU_LIMIT)
        lin = jnp.clip(_dot(x, wlb_ref[...]) + bl_ref[...], -SWIGLU_LIMIT, SWIGLU_LIMIT)
        act_ref[...] = (gate * _sigmoid(SWIGLU_ALPHA * gate) * (lin + 1.0)).astype(BF16)

    @pl.when(jnp.logical_not(live))
    def _():
        act_ref[...] = jnp.zeros(act_ref.shape, BF16)


def _expert_up(blk_e, nxt_e, x_pad, w_gu, b_gu, n_blk, tn):
    P, D = x_pad.shape
    E = w_gu.shape[0]
    F = w_gu.shape[2] // 2
    nf = F // tn
    ex = lambda be, m: jnp.minimum(be[m], E - 1)
    grid_spec = pltpu.PrefetchScalarGridSpec(
        num_scalar_prefetch=2,
        grid=(nf, n_blk),
        in_specs=[pl.BlockSpec((ROW_BLOCK, D), lambda n, m, be, nx: (m, 0)),
                  pl.BlockSpec(memory_space=pl.ANY),
                  pl.BlockSpec((None, 1, tn), lambda n, m, be, nx: (ex(be, m), 0, n)),
                  pl.BlockSpec((None, 1, tn), lambda n, m, be, nx: (ex(be, m), 0, nf + n))],
        out_specs=pl.BlockSpec((ROW_BLOCK, tn), lambda n, m, be, nx: (m, n)),
        scratch_shapes=[pltpu.VMEM((2, 2, D, tn), F32), pltpu.VMEM((D, tn), BF16), pltpu.VMEM((D, tn), BF16),
                        pltpu.SemaphoreType.DMA((2, 2)), pltpu.SMEM((1,), I32)])
    return pl.pallas_call(
        functools.partial(_expert_up_kernel, n_experts=E, d_ff=F),
        grid_spec=grid_spec,
        out_shape=jax.ShapeDtypeStruct((P, F), BF16),
        compiler_params=_cparams(("arbitrary", "arbitrary")),
        name="expert_up",
    )(blk_e, nxt_e, x_pad, w_gu, b_gu, b_gu)


def _expert_down_kernel(be_ref, nxt_ref, a_ref, w_hbm, b_ref, y_ref, wbuf_ref, wb_ref, sem, cnt_ref, *, n_experts):
    live = _stream_expert_weights(be_ref, nxt_ref, w_hbm, (0,), wbuf_ref, (wb_ref,), sem, cnt_ref, n_experts)

    @pl.when(live)
    def _():
        y_ref[...] = _dot(a_ref[...], wb_ref[...]) + b_ref[...]

    @pl.when(jnp.logical_not(live))
    def _():
        y_ref[...] = jnp.zeros(y_ref.shape, F32)


def _expert_down(blk_e, nxt_e, act, w_d, b_d, n_blk, tn):
    P, F = act.shape
    E, _, D = w_d.shape
    ex = lambda be, m: jnp.minimum(be[m], E - 1)
    grid_spec = pltpu.PrefetchScalarGridSpec(
        num_scalar_prefetch=2,
        grid=(D // tn, n_blk),
        in_specs=[pl.BlockSpec((ROW_BLOCK, F), lambda n, m, be, nx: (m, 0)),
                  pl.BlockSpec(memory_space=pl.ANY),
                  pl.BlockSpec((None, 1, tn), lambda n, m, be, nx: (ex(be, m), 0, n))],
        out_specs=pl.BlockSpec((ROW_BLOCK, tn), lambda n, m, be, nx: (m, n)),
        scratch_shapes=[pltpu.VMEM((2, 1, F, tn), F32), pltpu.VMEM((F, tn), BF16),
                        pltpu.SemaphoreType.DMA((2, 1)), pltpu.SMEM((1,), I32)])
    return pl.pallas_call(
        functools.partial(_expert_down_kernel, n_experts=E),
        grid_spec=grid_spec,
        out_shape=jax.ShapeDtypeStruct((P, D), F32),
        compiler_params=_cparams(("arbitrary", "arbitrary")),
        name="expert_down",
    )(blk_e, nxt_e, act, w_d, b_d)


def _combine_kernel(dest_ref, dest_next_ref, x1_ref, tw_ref, g_ref, ypad_ref, o_ref, ybuf_ref, sem):
    tm = x1_ref.shape[0]
    i = pl.program_id(0)
    slot = i % 2

    def gather(d_ref, s):
        def issue(t, _):
            for k in range(TOP_K):
                _row_copy(ypad_ref, d_ref[k, t], ybuf_ref.at[s, k], t, sem.at[s]).start(priority=k % 2)
            return 0
        lax.fori_loop(0, tm, issue, 0)

    @pl.when(i == 0)
    def _():
        gather(dest_ref, 0)

    @pl.when(i + 1 < pl.num_programs(0))
    def _():
        gather(dest_next_ref, 1 - slot)

    def drain(t, _):
        for k in range(TOP_K):
            _row_copy(ypad_ref, 0, ybuf_ref.at[slot, k], 0, sem.at[slot]).wait()
        return 0
    lax.fori_loop(0, tm, drain, 0)

    tw = tw_ref[...]
    acc = x1_ref[...]
    for k in range(TOP_K):
        acc = acc + tw[:, k:k + 1] * ybuf_ref[slot, k]
    o_ref[...] = acc * lax.rsqrt(jnp.mean(acc * acc, axis=-1, keepdims=True) + NORM_EPS) * g_ref[...]


def _combine(dest, x1, top_w, g, y_pad, tm):
    T, D = x1.shape
    n = T // tm
    return pl.pallas_call(
        _combine_kernel,
        grid=(n,),
        in_specs=[pl.BlockSpec((8, tm), lambda i: (0, i), memory_space=pltpu.SMEM),
                  pl.BlockSpec((8, tm), lambda i: (0, jnp.minimum(i + 1, n - 1)), memory_space=pltpu.SMEM),
                  pl.BlockSpec((tm, D), lambda i: (i, 0)),
                  pl.BlockSpec((tm, HEAD_DIM), lambda i: (i, 0)),
                  pl.BlockSpec((1, D), lambda i: (0, 0)),
                  pl.BlockSpec(memory_space=pl.ANY)],
        out_specs=pl.BlockSpec((tm, D), lambda i: (i, 0)),
        out_shape=jax.ShapeDtypeStruct((T, D), F32),
        scratch_shapes=[pltpu.VMEM((2, TOP_K, tm, D), F32), pltpu.SemaphoreType.DMA((2,))],
        compiler_params=_cparams(("arbitrary",)),
        name="combine",
    )(dest, dest, x1, top_w, g, y_pad)


def _rope_tables(pos):
    inv_freq = ROPE_THETA ** (-jnp.arange(0, ROPE_DIM, 2, dtype=F32) / ROPE_DIM)
    ang = pos.astype(F32)[..., None] * inv_freq
    cos, sin = jnp.cos(ang), jnp.sin(ang)
    rest = HEAD_DIM - ROPE_DIM
    cos_t = jnp.concatenate([cos, cos, jnp.ones(cos.shape[:-1] + (rest,), F32)], axis=-1)
    sin_t = jnp.concatenate([-sin, sin, jnp.zeros(sin.shape[:-1] + (rest,), F32)], axis=-1)
    return cos_t, sin_t


def _cmp_to_slc_t(seq, nc_pad):
    n_cmp = (seq - CMP_BLOCK) // CMP_STRIDE + 1
    n_slc = seq // SEL_BLOCK
    cmp_start = np.arange(n_cmp) * CMP_STRIDE
    slc_start = np.arange(n_slc) * SEL_BLOCK
    overlap = np.clip(np.minimum(cmp_start[:, None] + CMP_BLOCK, slc_start[None, :] + SEL_BLOCK)
                      - np.maximum(cmp_start[:, None], slc_start[None, :]), 0, None)
    out = np.zeros((n_slc, nc_pad), np.float32)
    out[:, :n_cmp] = (overlap / CMP_BLOCK).T
    return jnp.asarray(out, BF16)


def _regroup_w_in(w):
    D = w.shape[0]
    o_kv = NSA_WIDTH
    o_gn = o_kv + 6 * KV_WIDTH
    o_b = o_gn + 3 * NSA_HEADS
    o_gm = o_b + 3 * MOBA_WIDTH
    kv = lambda s: w[:, o_kv + s * KV_WIDTH:o_kv + (s + 1) * KV_WIDTH]
    mb = lambda s: w[:, o_b + s * MOBA_WIDTH:o_b + (s + 1) * MOBA_WIDTH]
    cols = [w[:, :NSA_WIDTH], kv(2), kv(4), mb(0), mb(1), kv(0), kv(1), kv(3), kv(5), mb(2), w[:, o_gm:o_gm + 2 * D]]
    w_main = jnp.concatenate(cols, axis=1).astype(BF16)
    wg = w[:, o_gn:o_gn + 3 * NSA_HEADS]
    wg = wg.reshape(D, 3, NSA_GROUPS, NSA_REP).transpose(2, 1, 3, 0).reshape(NSA_GROUPS, 3 * NSA_REP, D)
    wg = jnp.pad(wg, ((0, 0), (0, GATE_ROWS - 3 * NSA_REP), (0, 0))).reshape(NSA_GROUPS * GATE_ROWS, D)
    return w_main, wg.astype(BF16)


def kernel(x, positions, g_attn_norm, w_in, pe_cmp_k, w_cmp_k1, w_cmp_k2, pe_cmp_v, w_cmp_v1, w_cmp_v2, w_proj_nsa, w_proj_moba, w_out, g_ffn_norm, w_router, b_router, w_gate_up, b_gate_up, w_down, b_down, g_final_norm):
    B, S, D = x.shape
    T = B * S
    E = w_router.shape[-1]
    F = w_down.shape[-2]
    depth = w_in.shape[0]
    assert depth == 1, "the final norm is fused into the single layer's combine step"
    assert S % MOBA_BLOCK == 0 and S >= WINDOW + NSA_TQ and (2 * D) % COL_TILE == 0 and T % 512 == 0
    nc = S // CMP_STRIDE
    tm_big = min(1024, T)

    cos_t, sin_t = _rope_tables(positions)
    cmp_end = np.arange(nc) * CMP_STRIDE + CMP_BLOCK - 1
    cmp_end = np.minimum(cmp_end, S - 1)
    cos_c, sin_c = cos_t[:, cmp_end], sin_t[:, cmp_end]
    c2st = _cmp_to_slc_t(S, nc)

    x2 = x.reshape(T, D)
    for layer in range(depth):
        w_main, w_gates_t = _regroup_w_in(w_in[layer])
        h, gates_t = _norm_gates(x2, g_attn_norm[layer].reshape(1, D), w_gates_t, tm_big)
        proj = _in_proj(h, w_main, cos_t.reshape(T, HEAD_DIM), sin_t.reshape(T, HEAD_DIM), tm_big)
        proj3 = proj.reshape(B, S, proj.shape[1])

        xc = proj3[:, :, OFF_KCMP:OFF_KCMP + 2 * KV_WIDTH].reshape(B, nc, CMP_STRIDE * 2 * KV_WIDTH)
        kc, vc = _compress(xc, pe_cmp_k[layer], w_cmp_k1[layer].astype(BF16), w_cmp_k2[layer].astype(BF16),
                           pe_cmp_v[layer], w_cmp_v1[layer].astype(BF16), w_cmp_v2[layer].astype(BF16), cos_c, sin_c)
        o_a = _nsa_attention(proj3, kc, vc, gates_t, c2st)
        o_b = _moba_attention(proj3)

        mix = _merge_proj(o_a.reshape(T, NSA_WIDTH), o_b.reshape(T, MOBA_WIDTH),
                          w_proj_nsa[layer].astype(BF16), w_proj_moba[layer].astype(BF16), proj, tm_big)
        wr_t = w_router[layer].T
        wr_hi = wr_t.astype(BF16)
        wr_lo = (wr_t - wr_hi.astype(F32)).astype(BF16)
        x1, h2, top_e, top_w = _out_router(x2, mix, w_out[layer].astype(BF16), g_ffn_norm[layer].reshape(1, D),
                                           wr_hi, wr_lo, b_router[layer].reshape(E, 1), 512)

        n_blk = (T * TOP_K) // ROW_BLOCK + E
        n_blk_pad = -(-n_blk // HEAD_DIM) * HEAD_DIM
        dest, blk_e, pad_lo, pad_hi = _route(top_e, E, n_blk_pad, 512)
        pads = jnp.stack([pad_lo[:, 0], pad_hi[:, 0]])
        x_pad = _dispatch(pads, dest, h2, n_blk * ROW_BLOCK, 512)
        blk_e = blk_e.reshape(n_blk_pad)
        nxt_e = _next_expert(blk_e, E)
        act = _expert_up(blk_e, nxt_e, x_pad, w_gate_up[layer], b_gate_up[layer].reshape(E, 1, 2 * F), n_blk,
                         min(1024, F))
        y_pad = _expert_down(blk_e, nxt_e, act, w_down[layer], b_down[layer].reshape(E, 1, D), n_blk, D)
        x2 = _combine(dest, x1, top_w, g_final_norm.reshape(1, D), y_pad, 256)
    return x2.reshape(B, S, D)
```

```python
import functools

import jax
import jax.numpy as jnp
import numpy as np
from jax import lax
from jax.experimental import pallas as pl
from jax.experimental.pallas import tpu as pltpu

F32 = jnp.float32
BF16 = jnp.bfloat16
I32 = jnp.int32

HEAD_DIM = 128
ROPE_DIM = HEAD_DIM // 4
ROPE_HALF = ROPE_DIM // 2
ROPE_THETA = 500000.0
NORM_EPS = 1e-5
NEG_INF = -1e30
REMOVED = -3e38
SEL_FORCED = 1e9

NSA_HEADS = 8
NSA_GROUPS = 2
NSA_REP = NSA_HEADS // NSA_GROUPS
CMP_BLOCK = 32
CMP_STRIDE = 16
CMP_HIDDEN = 256
SEL_BLOCK = 64
SEL_TOPN = 16
WINDOW = 512
MOBA_HEADS = 8
MOBA_BLOCK = 256
MOBA_TOPK = 3
TOP_K = 4
SWIGLU_LIMIT = 7.0
SWIGLU_ALPHA = 1.702
ROW_BLOCK = 256

NSA_WIDTH = NSA_HEADS * HEAD_DIM
MOBA_WIDTH = MOBA_HEADS * HEAD_DIM
KV_WIDTH = NSA_GROUPS * HEAD_DIM
SCALE = HEAD_DIM ** -0.5
Q_SCALE = SCALE * 1.4426950408889634

COL_TILE = 512
OFF_QA = 0
OFF_KSLC = NSA_WIDTH
OFF_KWIN = OFF_KSLC + KV_WIDTH
OFF_QB = OFF_KWIN + KV_WIDTH
OFF_KB = OFF_QB + MOBA_WIDTH
ROPE_COLS = OFF_KB + MOBA_WIDTH
OFF_KCMP = ROPE_COLS
OFF_VCMP = OFF_KCMP + KV_WIDTH
OFF_VSLC = OFF_VCMP + KV_WIDTH
OFF_VWIN = OFF_VSLC + KV_WIDTH
OFF_VB = OFF_VWIN + KV_WIDTH
PLAIN_END = OFF_VB + MOBA_WIDTH
N_ROPE_TILES = ROPE_COLS // COL_TILE
N_PLAIN_TILES = (PLAIN_END - ROPE_COLS) // COL_TILE
Q_TILES = tuple(range(OFF_QA // COL_TILE, OFF_KSLC // COL_TILE)) + tuple(range(OFF_QB // COL_TILE, OFF_KB // COL_TILE))

VMEM_LIMIT = 56 * 1024 * 1024
NSA_TQ = 128
SEL_KT = 512
GATE_ROWS = 16
IN_PROJ_CHUNK = 256
NSA_SUB = 4
MOBA_HEADS_PER_STEP = 4


def _cparams(sem):
    return pltpu.CompilerParams(dimension_semantics=sem, vmem_limit_bytes=VMEM_LIMIT)


def _sigmoid(z):
    return 1.0 / (1.0 + jnp.exp(-z))


def _dot(a, b):
    return jnp.dot(a, b, preferred_element_type=F32)


def _dot_nt(a, b):
    return lax.dot_general(a, b, (((1,), (1,)), ((), ())), preferred_element_type=F32)


def _split_hi_lo(v):
    hi = v.astype(BF16)
    lo = (v - hi.astype(F32)).astype(BF16)
    return hi, lo


def _pack_bf16_pairs(v):
    half = v.shape[1] // 2
    lo = pltpu.bitcast(v[:, :half].astype(BF16).astype(F32), jnp.uint32)
    hi = pltpu.bitcast(v[:, half:].astype(BF16).astype(F32), jnp.uint32)
    return lax.shift_right_logical(lo, jnp.uint32(16)) | hi


def _unpack_bf16_pairs(w):
    lo = pltpu.bitcast(lax.shift_left(w, jnp.uint32(16)), F32)
    hi = pltpu.bitcast(w & jnp.uint32(0xFFFF0000), F32)
    return lo, hi


def _topk_rows(work, row_idx, n_rows, k):
    picks = []
    for _ in range(k):
        m = jnp.max(work, axis=0, keepdims=True)
        first = jnp.min(jnp.where(work == m, row_idx, n_rows), axis=0, keepdims=True)
        pick = row_idx == first
        picks.append((m, first, pick))
        work = jnp.where(pick, REMOVED, work)
    return picks


def _norm_gates_kernel(x_ref, g_ref, wgt_ref, h_ref, gt_ref):
    x = x_ref[...]
    h = x * lax.rsqrt(jnp.mean(x * x, axis=-1, keepdims=True) + NORM_EPS) * g_ref[...]
    hb = h.astype(BF16)
    h_ref[...] = hb
    gt_ref[...] = _sigmoid(_dot_nt(wgt_ref[...], hb))


def _norm_gates(x2, g, wgt, tm):
    T, D = x2.shape
    R = wgt.shape[0]
    return pl.pallas_call(
        _norm_gates_kernel,
        grid=(T // tm,),
        in_specs=[pl.BlockSpec((tm, D), lambda i: (i, 0)),
                  pl.BlockSpec((1, D), lambda i: (0, 0)),
                  pl.BlockSpec((R, D), lambda i: (0, 0))],
        out_specs=[pl.BlockSpec((tm, D), lambda i: (i, 0)),
                   pl.BlockSpec((R, tm), lambda i: (0, i))],
        out_shape=[jax.ShapeDtypeStruct((T, D), BF16), jax.ShapeDtypeStruct((R, T), F32)],
        compiler_params=_cparams(("arbitrary",)),
        name="norm_gates",
    )(x2, g, wgt)


def _rope_tile(xh, c, s, lane):
    rot = jnp.where(lane < ROPE_HALF, pltpu.roll(xh, HEAD_DIM - ROPE_HALF, 1), pltpu.roll(xh, ROPE_HALF, 1))
    return xh * c + rot * s


def _in_proj_kernel(h_ref, w_ref, c_ref, s_ref, o_ref):
    j = pl.program_id(1)
    tm = h_ref.shape[0]
    chunk = min(IN_PROJ_CHUNK, tm)

    def by_chunks(epilogue):
        for r0 in range(0, tm, chunk):
            rows = slice(r0, r0 + chunk)
            epilogue(rows, _dot(h_ref[rows, :], w_ref[...]))

    @pl.when(j < N_ROPE_TILES)
    def _():
        is_q = j == Q_TILES[0]
        for qt in Q_TILES[1:]:
            is_q = is_q | (j == qt)
        f = jnp.where(is_q, Q_SCALE, 1.0).astype(F32)
        lane = lax.broadcasted_iota(I32, (chunk, HEAD_DIM), 1)

        def rope(rows, acc):
            c = c_ref[rows, :] * f
            s = s_ref[rows, :] * f
            for hh in range(COL_TILE // HEAD_DIM):
                sl = slice(hh * HEAD_DIM, (hh + 1) * HEAD_DIM)
                o_ref[rows, sl] = _rope_tile(acc[:, sl], c, s, lane).astype(BF16)
        by_chunks(rope)

    @pl.when((j >= N_ROPE_TILES) & (j < N_ROPE_TILES + N_PLAIN_TILES))
    def _():
        def plain(rows, acc):
            o_ref[rows, :] = acc.astype(BF16)
        by_chunks(plain)

    @pl.when(j >= N_ROPE_TILES + N_PLAIN_TILES)
    def _():
        def gate(rows, acc):
            o_ref[rows, :] = _sigmoid(acc).astype(BF16)
        by_chunks(gate)


def _in_proj(h, w, cos_t, sin_t, tm):
    T, D = h.shape
    N = w.shape[1]
    return pl.pallas_call(
        _in_proj_kernel,
        grid=(T // tm, N // COL_TILE),
        in_specs=[pl.BlockSpec((tm, D), lambda i, j: (i, 0)),
                  pl.BlockSpec((D, COL_TILE), lambda i, j: (0, j)),
                  pl.BlockSpec((tm, HEAD_DIM), lambda i, j: (i, 0)),
                  pl.BlockSpec((tm, HEAD_DIM), lambda i, j: (i, 0))],
        out_specs=pl.BlockSpec((tm, COL_TILE), lambda i, j: (i, j)),
        out_shape=jax.ShapeDtypeStruct((T, N), BF16),
        compiler_params=_cparams(("arbitrary", "arbitrary")),
        name="in_proj",
    )(h, w, cos_t, sin_t)


def _gelu_tanh(x):
    return 0.5 * x * (1.0 + jnp.tanh(0.7978845608028654 * (x + 0.044715 * x * x * x)))


def _compress_kernel(x_ref, pek_ref, w1k_ref, w2k_ref, pev_ref, w1v_ref, w2v_ref, c_ref, s_ref, kc_ref, vc_ref):
    nc = x_ref.shape[0]
    half = CMP_STRIDE * HEAD_DIM
    tok_w = 2 * KV_WIDTH
    for which, (pe_ref, w1_ref, w2_ref, out_ref) in enumerate(
            ((pek_ref, w1k_ref, w2k_ref, kc_ref), (pev_ref, w1v_ref, w2v_ref, vc_ref))):
        for g in range(NSA_GROUPS):
            acc_a = jnp.zeros((nc, CMP_HIDDEN), F32)
            acc_b = jnp.zeros((nc, CMP_HIDDEN), F32)
            for l in range(CMP_STRIDE):
                off = l * tok_w + which * KV_WIDTH + g * HEAD_DIM
                xl = x_ref[:, off:off + HEAD_DIM].astype(F32)
                xa = (xl + pe_ref[l:l + 1, :]).astype(BF16)
                xb = (xl + pe_ref[CMP_STRIDE + l:CMP_STRIDE + l + 1, :]).astype(BF16)
                acc_a = acc_a + _dot(xa, w1_ref[l * HEAD_DIM:(l + 1) * HEAD_DIM, :])
                acc_b = acc_b + _dot(xb, w1_ref[half + l * HEAD_DIM:half + (l + 1) * HEAD_DIM, :])
            hid = _gelu_tanh(acc_a + pltpu.roll(acc_b, nc - 1, 0))
            out = _dot(hid.astype(BF16), w2_ref[...])
            if which == 0:
                lane = lax.broadcasted_iota(I32, out.shape, 1)
                out = _rope_tile(out, c_ref[...], s_ref[...], lane)
            out_ref[g] = out.astype(BF16)


def _compress(xc, pek, w1k, w2k, pev, w1v, w2v, cos_c, sin_c):
    B, nc, W = xc.shape
    full = lambda a: pl.BlockSpec(a.shape, lambda b: (0,) * a.ndim)
    out_sds = jax.ShapeDtypeStruct((B, NSA_GROUPS, nc, HEAD_DIM), BF16)
    out_spec = pl.BlockSpec((None, NSA_GROUPS, nc, HEAD_DIM), lambda b: (b, 0, 0, 0))
    return pl.pallas_call(
        _compress_kernel,
        grid=(B,),
        in_specs=[pl.BlockSpec((None, nc, W), lambda b: (b, 0, 0)),
                  full(pek), full(w1k), full(w2k), full(pev), full(w1v), full(w2v),
                  pl.BlockSpec((None, nc, HEAD_DIM), lambda b: (b, 0, 0)),
                  pl.BlockSpec((None, nc, HEAD_DIM), lambda b: (b, 0, 0))],
        out_specs=[out_spec, out_spec],
        out_shape=[out_sds, out_sds],
        compiler_params=_cparams(("arbitrary",)),
        name="compress",
    )(xc, pek, w1k, w2k, pev, w1v, w2v, cos_c, sin_c)


def _transpose_into(src_ref, col0, dst_ref, n_rows):
    def body(b, _):
        r0 = pl.multiple_of(b * HEAD_DIM, HEAD_DIM)
        blk = src_ref[pl.ds(r0, HEAD_DIM), col0:col0 + HEAD_DIM].astype(F32)
        dst_ref[:, pl.ds(r0, HEAD_DIM)] = blk.T.astype(BF16)
        return 0
    lax.fori_loop(0, n_rows // HEAD_DIM, body, 0)


def _topk_mask(score, k):
    n, L = score.shape
    sub = 8
    assert n % sub == 0
    groups = n // sub
    segs = [score[sub * g:sub * (g + 1), :] for g in range(groups)]
    sub_idx = lax.broadcasted_iota(I32, (sub, L), 0)
    later = [jnp.where(sub_idx > r, 1.0, 0.0) for r in range(sub)]
    ranks = [jnp.zeros((sub, L), F32) for _ in range(groups)]
    for i in range(n):
        gi, ri = divmod(i, sub)
        row = score[i:i + 1, :]
        for g in range(groups):
            if g > gi:
                inc = jnp.where(row >= segs[g], 1.0, 0.0)
            elif g < gi:
                inc = jnp.where(row > segs[g], 1.0, 0.0)
            else:
                inc = jnp.where(row > segs[g], 1.0, jnp.where(row == segs[g], later[ri], 0.0))
            ranks[g] = ranks[g] + inc
    return jnp.concatenate(ranks, axis=0) < k


def _bias_rows_to_cols(sel, lanes_out=HEAD_DIM):
    n, L = sel.shape
    bias_t = jnp.where(sel, 0.0, NEG_INF)
    if n < lanes_out:
        bias_t = jnp.concatenate([bias_t, jnp.zeros((lanes_out - n, L), F32)], axis=0)
    return bias_t.T.astype(BF16)


def _softmax_cols(s, live=None):
    m = jnp.max(s, axis=0, keepdims=True)
    p = jnp.exp2(s - m)
    inv = 1.0 / jnp.sum(p, axis=0, keepdims=True)
    if live is not None:
        inv = jnp.where(live, inv, 0.0)
    return p * inv


def _online_update(s, vt, carry):
    m, l, acc = carry
    m_new = jnp.maximum(m, jnp.max(s, axis=0, keepdims=True))
    alpha = jnp.exp2(m - m_new)
    p = jnp.exp2(s - m_new)
    l = alpha * l + jnp.sum(p, axis=0, keepdims=True)
    acc = alpha * acc + _dot(vt, p.astype(BF16))
    return m_new, l, acc


def _nsa_kernel(q_ref, kc_ref, vc_ref, ks_ref, vs_ref, kw_ref, vw_ref, gt_ref, c2st_ref, e_ref, o_ref,
                ksa_ref, vst_ref, vwt_ref, vct_ref, *, seq):
    tq = NSA_TQ
    U = NSA_SUB
    R = NSA_REP
    i = pl.program_id(2)
    base = i * (U * tq)
    nc = kc_ref.shape[0]
    n_slc = c2st_ref.shape[0]
    lanes = R * tq
    band = WINDOW + tq
    tile_r = lambda a: jnp.concatenate([a] * R, axis=1)
    rowi = lambda n: lax.broadcasted_iota(I32, (n, tq), 0)

    @pl.when(i == 0)
    def _():
        _transpose_into(vs_ref, 0, vst_ref, seq)
        _transpose_into(vw_ref, 0, vwt_ref, seq)
        _transpose_into(vc_ref, 0, vct_ref, nc)
        ksa_ref[:, 0:HEAD_DIM] = ks_ref[...]
        ksa_ref[:, HEAD_DIM:2 * HEAD_DIM] = e_ref[...]

    subs = range(U)
    heads = [[q_ref[u * tq:(u + 1) * tq, r * HEAD_DIM:(r + 1) * HEAD_DIM] for r in range(R)] for u in subs]
    q4 = [jnp.concatenate(heads[u], axis=0) for u in subs]
    tcols = [lambda n, q0=base + u * tq: q0 + lax.broadcasted_iota(I32, (n, tq), 1) for u in subs]
    starts = [pl.multiple_of(jnp.maximum(base + u * tq - WINDOW, 0), HEAD_DIM) for u in subs]

    s_c = [_dot_nt(kc_ref[...], q4[u]) for u in subs]
    s_w = [_dot_nt(kw_ref[pl.ds(starts[u], band), :], q4[u]) for u in subs]

    p_c = []
    for u in subs:
        bias_c = jnp.where(rowi(nc) * CMP_STRIDE + (CMP_BLOCK - 1) <= tcols[u](nc), 0.0, NEG_INF)
        live_c = tcols[u](1) >= CMP_BLOCK - 1
        p_c.append(_softmax_cols(s_c[u] + tile_r(bias_c), tile_r(live_c)))
    o_cmp = [_dot(vct_ref[...], p_c[u].astype(BF16)) for u in subs]

    imp = []
    for u in subs:
        p_sum = p_c[u][:, 0:tq]
        for r in range(1, R):
            p_sum = p_sum + p_c[u][:, r * tq:(r + 1) * tq]
        p_hi, p_lo = _split_hi_lo(p_sum)
        imp.append(_dot(c2st_ref[...], p_hi) + _dot(c2st_ref[...], p_lo))

    p_w = []
    for u in subs:
        dpos = tcols[u](band) - (starts[u] + rowi(band))
        bias_w = jnp.where((dpos >= 0) & (dpos < WINDOW), 0.0, NEG_INF)
        p_w.append(_softmax_cols(s_w[u] + tile_r(bias_w)))
    o_win = [_dot(vwt_ref[:, pl.ds(starts[u], band)], p_w[u].astype(BF16)) for u in subs]

    q4a = []
    for u in subs:
        jj = rowi(n_slc)
        cur = tcols[u](n_slc) // SEL_BLOCK
        valid = jj <= cur
        forced = (jj == 0) | (jj == cur) | (jj == cur - 1)
        score = jnp.where(forced, SEL_FORCED, jnp.where(valid, imp[u], -SEL_FORCED))
        sel = _topk_mask(score, min(SEL_TOPN, n_slc)) & valid
        bias = _bias_rows_to_cols(sel)
        q4a.append(jnp.concatenate([jnp.concatenate([h, bias], axis=1) for h in heads[u]], axis=0))

    def sel_body(kt, carry):
        k0 = pl.multiple_of(kt * SEL_KT, SEL_KT)
        s = [_dot_nt(ksa_ref[pl.ds(k0, SEL_KT), :], q4a[u]) for u in range(U)]
        return tuple(_online_update(s[u], vst_ref[:, pl.ds(k0, SEL_KT)], carry[u]) for u in range(U))

    n_full = base // SEL_KT
    init = (jnp.full((1, lanes), NEG_INF, F32), jnp.zeros((1, lanes), F32), jnp.zeros((HEAD_DIM, lanes), F32))
    carry = lax.fori_loop(0, n_full, sel_body, (init,) * U)
    k0 = pl.multiple_of(n_full * SEL_KT, SEL_KT)
    gt = gt_ref[...]
    assert U * tq == SEL_KT
    n_diag = [(u + 1) * tq for u in subs]
    s_d = [_dot_nt(ksa_ref[pl.ds(k0, n_diag[u]), :], q4a[u]) for u in subs]
    for u in subs:
        causal = jnp.where(k0 + rowi(n_diag[u]) <= tcols[u](n_diag[u]), 0.0, NEG_INF)
        _, l_s, acc_s = _online_update(s_d[u] + tile_r(causal), vst_ref[:, pl.ds(k0, n_diag[u])], carry[u])
        o_slc = acc_s * (1.0 / l_s)

        g = gt[:, u * tq:(u + 1) * tq]
        for r in range(R):
            sl = slice(r * tq, (r + 1) * tq)
            o_r = (g[r:r + 1, :] * o_cmp[u][:, sl] + g[R + r:R + r + 1, :] * o_slc[:, sl]
                   + g[2 * R + r:2 * R + r + 1, :] * o_win[u][:, sl])
            o_ref[u * tq:(u + 1) * tq, r * HEAD_DIM:(r + 1) * HEAD_DIM] = o_r.T.astype(BF16)


def _block_onehot(seq, block):
    assert seq // block <= HEAD_DIM
    e = np.zeros((seq, HEAD_DIM), np.float32)
    e[np.arange(seq), np.arange(seq) // block] = 1.0
    return jnp.asarray(e, BF16)


def _nsa_attention(proj3, kc, vc, gates_t, c2st):
    B, S, _ = proj3.shape
    G = NSA_GROUPS
    nc = kc.shape[2]
    tq = NSA_TQ * NSA_SUB
    assert S % tq == 0 and SEL_KT % tq == 0
    nq = S // tq
    hb = HEAD_DIM
    gw = NSA_REP * HEAD_DIM
    e_sel = _block_onehot(S, SEL_BLOCK)
    seq_spec = lambda off: pl.BlockSpec((None, S, hb), lambda b, g, i: (b, 0, off // hb + g))
    cmp_spec = pl.BlockSpec((None, None, nc, hb), lambda b, g, i: (b, g, 0, 0))
    return pl.pallas_call(
        functools.partial(_nsa_kernel, seq=S),
        grid=(B, G, nq),
        in_specs=[pl.BlockSpec((None, tq, gw), lambda b, g, i: (b, i, OFF_QA // gw + g)),
                  cmp_spec, cmp_spec,
                  seq_spec(OFF_KSLC), seq_spec(OFF_VSLC), seq_spec(OFF_KWIN), seq_spec(OFF_VWIN),
                  pl.BlockSpec((GATE_ROWS, tq), lambda b, g, i: (g, b * nq + i)),
                  pl.BlockSpec(c2st.shape, lambda b, g, i: (0, 0)),
                  pl.BlockSpec(e_sel.shape, lambda b, g, i: (0, 0))],
        out_specs=pl.BlockSpec((None, tq, gw), lambda b, g, i: (b, i, g)),
        out_shape=jax.ShapeDtypeStruct((B, S, NSA_WIDTH), BF16),
        scratch_shapes=[pltpu.VMEM((S, 2 * hb), BF16), pltpu.VMEM((hb, S), BF16), pltpu.VMEM((hb, S), BF16),
                        pltpu.VMEM((hb, nc), BF16)],
        compiler_params=_cparams(("arbitrary", "arbitrary", "arbitrary")),
        name="nsa_attn",
    )(proj3, kc, vc, proj3, proj3, proj3, proj3, gates_t, c2st, e_sel)


def _moba_kernel(q_ref, k_ref, v_ref, e_ref, o_ref, ka_ref, vt_ref, kmh_ref, kml_ref, *, seq):
    blk = MOBA_BLOCK
    HP = MOBA_HEADS_PER_STEP
    i = pl.program_id(2)
    nb = seq // blk
    nbp = kmh_ref.shape[1]
    hsl = lambda h: slice(h * HEAD_DIM, (h + 1) * HEAD_DIM)

    @pl.when(i == 0)
    def _():
        for h in range(HP):
            _transpose_into(v_ref, h * HEAD_DIM, vt_ref.at[h], seq)
            ka_ref[h, :, 0:HEAD_DIM] = k_ref[:, hsl(h)]
            ka_ref[h, :, HEAD_DIM:2 * HEAD_DIM] = e_ref[...]
            means = [jnp.sum(k_ref[n * blk:(n + 1) * blk, hsl(h)].astype(F32), axis=0, keepdims=True) * (1.0 / blk)
                     for n in range(nb)]
            if nbp > nb:
                means.append(jnp.zeros((nbp - nb, HEAD_DIM), F32))
            hi, lo = _split_hi_lo(jnp.concatenate(means, axis=0))
            kmh_ref[h] = hi
            kml_ref[h] = lo

    own0 = pl.multiple_of(i * blk, blk)
    past = lax.broadcasted_iota(I32, (nbp, blk), 0) < i
    causal = jnp.where(lax.broadcasted_iota(I32, (blk, blk), 0) <= lax.broadcasted_iota(I32, (blk, blk), 1), 0.0, NEG_INF)
    hs = range(HP)
    q = [q_ref[:, hsl(h)] for h in hs]
    gate = [_dot_nt(kmh_ref[h], q[h]) + _dot_nt(kml_ref[h], q[h]) for h in hs]
    s_own = [_dot_nt(k_ref[pl.ds(own0, blk), hsl(h)], q[h]) for h in hs]
    m_own = [jnp.max(s_own[h] + causal, axis=0, keepdims=True) for h in hs]
    p_own = [jnp.exp2(s_own[h] + causal - m_own[h]) for h in hs]
    carry = [(m_own[h], jnp.sum(p_own[h], axis=0, keepdims=True),
              _dot(vt_ref[h, :, pl.ds(own0, blk)], p_own[h].astype(BF16))) for h in hs]
    qa = []
    for h in hs:
        sel = _topk_mask(jnp.where(past, gate[h], -SEL_FORCED), min(MOBA_TOPK, nb)) & past
        qa.append(jnp.concatenate([q[h], _bias_rows_to_cols(sel)], axis=1))

    kt = 2 * blk

    def body(j, carry):
        k0 = pl.multiple_of(j * kt, kt)
        s = [_dot_nt(ka_ref[h, pl.ds(k0, kt), :], qa[h]) for h in range(HP)]
        return tuple(_online_update(s[h], vt_ref[h, :, pl.ds(k0, kt)], carry[h]) for h in range(HP))

    carry = lax.fori_loop(0, (i + 1) // 2, body, tuple(carry))
    for h in range(HP):
        _, l, acc = carry[h]
        o_ref[:, hsl(h)] = (acc * (1.0 / l)).T.astype(BF16)


def _moba_attention(proj3):
    B, S, _ = proj3.shape
    HP = MOBA_HEADS_PER_STEP
    blk = MOBA_BLOCK
    hb = HEAD_DIM
    assert (S // blk) % 2 == 0 and MOBA_HEADS % HP == 0
    nbp = max(8, -(-(S // blk) // 8) * 8)
    e_blk = _block_onehot(S, blk)
    wide = HP * hb
    seq_spec = lambda off: pl.BlockSpec((None, S, wide), lambda b, h, i: (b, 0, off // wide + h))
    return pl.pallas_call(
        functools.partial(_moba_kernel, seq=S),
        grid=(B, MOBA_HEADS // HP, S // blk),
        in_specs=[pl.BlockSpec((None, blk, wide), lambda b, h, i: (b, i, OFF_QB // wide + h)),
                  seq_spec(OFF_KB), seq_spec(OFF_VB),
                  pl.BlockSpec(e_blk.shape, lambda b, h, i: (0, 0))],
        out_specs=pl.BlockSpec((None, blk, wide), lambda b, h, i: (b, i, h)),
        out_shape=jax.ShapeDtypeStruct((B, S, MOBA_WIDTH), BF16),
        scratch_shapes=[pltpu.VMEM((HP, S, 2 * hb), BF16), pltpu.VMEM((HP, hb, S), BF16),
                        pltpu.VMEM((HP, nbp, hb), BF16), pltpu.VMEM((HP, nbp, hb), BF16)],
        compiler_params=_cparams(("arbitrary", "arbitrary", "arbitrary")),
        name="moba_attn",
    )(proj3, proj3, proj3, e_blk)


def _merge_proj_kernel(oa_ref, ob_ref, wa_ref, wb_ref, ga_ref, gb_ref, o_ref):
    a = _dot(oa_ref[...], wa_ref[...])
    b = _dot(ob_ref[...], wb_ref[...])
    o_ref[...] = (ga_ref[...].astype(F32) * a + gb_ref[...].astype(F32) * b).astype(BF16)


def _merge_proj(oa, ob, wa, wb, proj, tm):
    T = oa.shape[0]
    D = wa.shape[1]
    tn = min(COL_TILE, D)
    g0 = PLAIN_END // tn
    return pl.pallas_call(
        _merge_proj_kernel,
        grid=(T // tm, D // tn),
        in_specs=[pl.BlockSpec((tm, NSA_WIDTH), lambda i, j: (i, 0)),
                  pl.BlockSpec((tm, MOBA_WIDTH), lambda i, j: (i, 0)),
                  pl.BlockSpec((NSA_WIDTH, tn), lambda i, j: (0, j)),
                  pl.BlockSpec((MOBA_WIDTH, tn), lambda i, j: (0, j)),
                  pl.BlockSpec((tm, tn), lambda i, j: (i, g0 + j)),
                  pl.BlockSpec((tm, tn), lambda i, j: (i, g0 + D // tn + j))],
        out_specs=pl.BlockSpec((tm, tn), lambda i, j: (i, j)),
        out_shape=jax.ShapeDtypeStruct((T, D), BF16),
        compiler_params=_cparams(("arbitrary", "arbitrary")),
        name="merge_proj",
    )(oa, ob, wa, wb, proj, proj)


def _out_router_kernel(x_ref, mix_ref, wo_ref, g_ref, wrh_ref, wrl_ref, br_ref, x1_ref, h2_ref, te_ref, tw_ref):
    tm = x_ref.shape[0]
    E = wrh_ref.shape[0]
    x1 = x_ref[...] + _dot(mix_ref[...], wo_ref[...])
    x1_ref[...] = x1
    h2 = x1 * lax.rsqrt(jnp.mean(x1 * x1, axis=-1, keepdims=True) + NORM_EPS) * g_ref[...]
    h2_ref[...] = _pack_bf16_pairs(h2)
    h_hi, h_lo = _split_hi_lo(h2)
    logits = (_dot_nt(wrh_ref[...], h_hi) + _dot_nt(wrh_ref[...], h_lo) + _dot_nt(wrl_ref[...], h_hi)
              + br_ref[...])
    e_idx = lax.broadcasted_iota(I32, (E, tm), 0)
    picks = _topk_rows(logits, e_idx, E, TOP_K)
    vals = [p[0] for p in picks]
    exps = [jnp.exp(v - vals[0]) for v in vals]
    inv = 1.0 / (exps[0] + exps[1] + exps[2] + exps[3])
    te_ref[...] = jnp.concatenate([p[1] for p in picks] + [jnp.zeros((8 - TOP_K, tm), I32)], axis=0)
    w_t = jnp.concatenate([e * inv for e in exps] + [jnp.zeros((HEAD_DIM - TOP_K, tm), F32)], axis=0)
    tw_ref[...] = w_t.T


def _out_router(x2, mix, wo, g, wrh, wrl, br, tm):
    T, D = x2.shape
    E = wrh.shape[0]
    full = lambda a: pl.BlockSpec(a.shape, lambda i: (0,) * a.ndim)
    return pl.pallas_call(
        _out_router_kernel,
        grid=(T // tm,),
        in_specs=[pl.BlockSpec((tm, D), lambda i: (i, 0)), pl.BlockSpec((tm, D), lambda i: (i, 0)),
                  full(wo), full(g), full(wrh), full(wrl), full(br)],
        out_specs=[pl.BlockSpec((tm, D), lambda i: (i, 0)), pl.BlockSpec((tm, D // 2), lambda i: (i, 0)),
                   pl.BlockSpec((8, tm), lambda i: (0, i)), pl.BlockSpec((tm, HEAD_DIM), lambda i: (i, 0))],
        out_shape=[jax.ShapeDtypeStruct((T, D), F32), jax.ShapeDtypeStruct((T, D // 2), jnp.uint32),
                   jax.ShapeDtypeStruct((8, T), I32), jax.ShapeDtypeStruct((T, HEAD_DIM), F32)],
        compiler_params=_cparams(("arbitrary",)),
        name="out_router",
    )(x2, mix, wo, g, wrh, wrl, br)


def _route_kernel(te_ref, dest_ref, blke_ref, padlo_ref, padhi_ref, run_ref, pstart_ref):
    ph = pl.program_id(0)
    i = pl.program_id(1)
    E = run_ref.shape[0]
    tm = te_ref.shape[1]
    nbp = blke_ref.shape[1]
    te = te_ref[...]
    e_idx = lax.broadcasted_iota(I32, (E, tm), 0)
    ohs = [te[k:k + 1, :] == e_idx for k in range(TOP_K)]
    oh = jnp.where(ohs[0] | ohs[1] | ohs[2] | ohs[3], 1.0, 0.0)
    tile_cnt = jnp.sum(oh, axis=1, keepdims=True)

    @pl.when((ph == 0) & (i == 0))
    def _():
        run_ref[...] = jnp.zeros(run_ref.shape, F32)

    @pl.when(ph == 0)
    def _():
        run_ref[...] = run_ref[...] + tile_cnt
        dest_ref[...] = jnp.zeros(dest_ref.shape, I32)

    @pl.when((ph == 1) & (i == 0))
    def _():
        counts = run_ref[...]
        padded = jnp.floor((counts + (ROW_BLOCK - 1)) * (1.0 / ROW_BLOCK)) * ROW_BLOCK
        row = lax.broadcasted_iota(I32, counts.shape, 0)
        incl = padded
        sh = 1
        while sh < E:
            incl = incl + jnp.where(row >= sh, pltpu.roll(incl, sh, 0), 0.0)
            sh *= 2
        pstart_ref[...] = incl - padded
        padlo_ref[...] = (incl - padded + counts).astype(I32)
        padhi_ref[...] = incl.astype(I32)
        blk_start = (lax.broadcasted_iota(I32, (E, nbp), 1) * ROW_BLOCK).astype(F32)
        blke_ref[...] = jnp.sum(jnp.where(incl[:, 0:1] <= blk_start, 1, 0), axis=0, keepdims=True).astype(I32)
        run_ref[...] = jnp.zeros(run_ref.shape, F32)

    @pl.when(ph == 1)
    def _():
        upper = jnp.where(lax.broadcasted_iota(I32, (tm, tm), 0) < lax.broadcasted_iota(I32, (tm, tm), 1), 1.0, 0.0)
        before = _dot(oh.astype(BF16), upper.astype(BF16))
        val = before + pstart_ref[:, 0:1] + run_ref[:, 0:1]
        rows = [jnp.sum(jnp.where(ohs[k], val, 0.0), axis=0, keepdims=True) for k in range(TOP_K)]
        dest_ref[...] = jnp.concatenate(rows + [jnp.zeros((8 - TOP_K, tm), F32)], axis=0).astype(I32)
        run_ref[...] = run_ref[...] + tile_cnt


def _route(top_e, n_experts, n_blk_pad, tm):
    T = top_e.shape[1]
    return pl.pallas_call(
        _route_kernel,
        grid=(2, T // tm),
        in_specs=[pl.BlockSpec((8, tm), lambda ph, i: (0, i))],
        out_specs=[pl.BlockSpec((8, tm), lambda ph, i: (0, i * ph)),
                   pl.BlockSpec((1, n_blk_pad), lambda ph, i: (0, 0)),
                   pl.BlockSpec((n_experts, HEAD_DIM), lambda ph, i: (0, 0)),
                   pl.BlockSpec((n_experts, HEAD_DIM), lambda ph, i: (0, 0))],
        out_shape=[jax.ShapeDtypeStruct((8, T), I32), jax.ShapeDtypeStruct((1, n_blk_pad), I32),
                   jax.ShapeDtypeStruct((n_experts, HEAD_DIM), I32), jax.ShapeDtypeStruct((n_experts, HEAD_DIM), I32)],
        scratch_shapes=[pltpu.VMEM((n_experts, HEAD_DIM), F32), pltpu.VMEM((n_experts, HEAD_DIM), F32)],
        compiler_params=_cparams(("arbitrary", "arbitrary")),
        name="route",
    )(top_e)


def _row_copy(src_ref, src_row, dst_ref, dst_row, sem):
    return pltpu.make_async_copy(src_ref.at[pl.ds(src_row, 1), :], dst_ref.at[pl.ds(dst_row, 1), :], sem)


def _dispatch_kernel(pad_ref, dest_ref, h_ref, xpad_ref, zero_ref, sem, zsem):
    tm = h_ref.shape[0]

    @pl.when(pl.program_id(0) == 0)
    def _():
        zero_ref[...] = jnp.zeros(zero_ref.shape, zero_ref.dtype)

        def per_expert(e, _):
            lo, hi = pad_ref[0, e], pad_ref[1, e]

            def issue(r, _):
                _row_copy(zero_ref, 0, xpad_ref, r, zsem).start()
                return 0
            lax.fori_loop(lo, hi, issue, 0)

            def drain(r, _):
                _row_copy(zero_ref, 0, xpad_ref, 0, zsem).wait()
                return 0
            lax.fori_loop(lo, hi, drain, 0)
            return 0
        lax.fori_loop(0, pad_ref.shape[1], per_expert, 0)

    def issue(t, _):
        for k in range(TOP_K):
            _row_copy(h_ref, t, xpad_ref, dest_ref[k, t], sem).start(priority=k % 2)
        return 0
    lax.fori_loop(0, tm, issue, 0)

    def drain(t, _):
        for k in range(TOP_K):
            _row_copy(h_ref, 0, xpad_ref, 0, sem).wait()
        return 0
    lax.fori_loop(0, tm, drain, 0)


def _dispatch(pads, dest, h2, n_rows, tm):
    T, D = h2.shape
    grid_spec = pltpu.PrefetchScalarGridSpec(
        num_scalar_prefetch=1,
        grid=(T // tm,),
        in_specs=[pl.BlockSpec((8, tm), lambda i, pads: (0, i), memory_space=pltpu.SMEM),
                  pl.BlockSpec((tm, D), lambda i, pads: (i, 0))],
        out_specs=pl.BlockSpec(memory_space=pl.ANY),
        scratch_shapes=[pltpu.VMEM((8, D), h2.dtype), pltpu.SemaphoreType.DMA(()), pltpu.SemaphoreType.DMA(())])
    return pl.pallas_call(
        _dispatch_kernel,
        grid_spec=grid_spec,
        out_shape=jax.ShapeDtypeStruct((n_rows, D), h2.dtype),
        compiler_params=_cparams(("arbitrary",)),
        name="dispatch",
    )(pads, dest, h2)


def _next_expert(blk_e, n_experts):
    nxt = jnp.min(jnp.where(blk_e[None, :] > blk_e[:, None], blk_e[None, :], n_experts), axis=1)
    return jnp.where(nxt < n_experts, nxt, -1).astype(I32)


def _stream_expert_weights(be_ref, nxt_ref, w_hbm, col_offsets, wbuf_ref, wb_refs, sem, cnt_ref, n_experts):
    n, m = pl.program_id(0), pl.program_id(1)
    tn = wb_refs[0].shape[1]
    e = be_ref[m]
    live = e < n_experts
    first = live & ((m == 0) | (be_ref[jnp.maximum(m - 1, 0)] != e))

    def copies(ex, col_tile, slot):
        c0 = pl.multiple_of(col_tile * tn, tn)
        return [pltpu.make_async_copy(w_hbm.at[ex, :, pl.ds(off + c0, tn)], wbuf_ref.at[slot, g], sem.at[slot, g])
                for g, off in enumerate(col_offsets)]

    @pl.when((n == 0) & (m == 0))
    def _():
        cnt_ref[0] = 0
        for c in copies(e, 0, 0):
            c.start()

    @pl.when(first)
    def _():
        slot = cnt_ref[0] % 2
        for c in copies(e, n, slot):
            c.wait()
        nxt = nxt_ref[m]

        @pl.when(nxt >= 0)
        def _():
            for c in copies(nxt, n, 1 - slot):
                c.start()

        @pl.when((nxt < 0) & (n + 1 < pl.num_programs(0)))
        def _():
            for c in copies(be_ref[0], n + 1, 1 - slot):
                c.start()

        for g, wb_ref in enumerate(wb_refs):
            wb_ref[...] = wbuf_ref[slot, g].astype(BF16)
        cnt_ref[0] = cnt_ref[0] + 1
    return live


def _expert_up_kernel(be_ref, nxt_ref, x_ref, w_hbm, bg_ref, bl_ref, act_ref, wbuf_ref, wgb_ref, wlb_ref, sem, cnt_ref,
                      *, n_experts, d_ff):
    live = _stream_expert_weights(be_ref, nxt_ref, w_hbm, (0, d_ff), wbuf_ref, (wgb_ref, wlb_ref), sem, cnt_ref,
                                  n_experts)

    @pl.when(live)
    def _():
        x_lo, x_hi = _unpack_bf16_pairs(x_ref[...])
        x = jnp.concatenate([x_lo.astype(BF16), x_hi.astype(BF16)], axis=1)
        gate =jnp.minimum(_dot(x, wgb_ref[...]) + bg_ref[...], SWIGLU_LIMIT)
        lin = jnp.clip(_dot(x, wlb_ref[...]) + bl_ref[...], -SWIGLU_LIMIT, SWIGLU_LIMIT)
        act_ref[...] = (gate * _sigmoid(SWIGLU_ALPHA * gate) * (lin + 1.0)).astype(BF16)

    @pl.when(jnp.logical_not(live))
    def _():
        act_ref[...] = jnp.zeros(act_ref.shape, BF16)


def _expert_up(blk_e, nxt_e, x_pad, w_gu, b_gu, n_blk, tn):
    P = x_pad.shape[0]
    E, D = w_gu.shape[0], w_gu.shape[1]
    F = w_gu.shape[2] // 2
    nf = F // tn
    ex = lambda be, m: jnp.minimum(be[m], E - 1)
    grid_spec = pltpu.PrefetchScalarGridSpec(
        num_scalar_prefetch=2,
        grid=(nf, n_blk),
        in_specs=[pl.BlockSpec((ROW_BLOCK, D // 2), lambda n, m, be, nx: (m, 0)),
                  pl.BlockSpec(memory_space=pl.ANY),
                  pl.BlockSpec((None, 1, tn), lambda n, m, be, nx: (ex(be, m), 0, n)),
                  pl.BlockSpec((None, 1, tn), lambda n, m, be, nx: (ex(be, m), 0, nf + n))],
        out_specs=pl.BlockSpec((ROW_BLOCK, tn), lambda n, m, be, nx: (m, n)),
        scratch_shapes=[pltpu.VMEM((2, 2, D, tn), F32), pltpu.VMEM((D, tn), BF16), pltpu.VMEM((D, tn), BF16),
                        pltpu.SemaphoreType.DMA((2, 2)), pltpu.SMEM((1,), I32)])
    return pl.pallas_call(
        functools.partial(_expert_up_kernel, n_experts=E, d_ff=F),
        grid_spec=grid_spec,
        out_shape=jax.ShapeDtypeStruct((P, F), BF16),
        compiler_params=_cparams(("arbitrary", "arbitrary")),
        name="expert_up",
    )(blk_e, nxt_e, x_pad, w_gu, b_gu, b_gu)


def _expert_down_kernel(be_ref, nxt_ref, a_ref, w_hbm, b_ref, y_ref, wbuf_ref, wb_ref, sem, cnt_ref, *, n_experts):
    live = _stream_expert_weights(be_ref, nxt_ref, w_hbm, (0,), wbuf_ref, (wb_ref,), sem, cnt_ref, n_experts)

    @pl.when(live)
    def _():
        y_ref[...] = _pack_bf16_pairs(_dot(a_ref[...], wb_ref[...]) + b_ref[...])

    @pl.when(jnp.logical_not(live))
    def _():
        y_ref[...] = jnp.zeros(y_ref.shape, y_ref.dtype)


def _expert_down(blk_e, nxt_e, act, w_d, b_d, n_blk):
    P, F = act.shape
    E, _, D = w_d.shape
    tn = D
    ex = lambda be, m: jnp.minimum(be[m], E - 1)
    grid_spec = pltpu.PrefetchScalarGridSpec(
        num_scalar_prefetch=2,
        grid=(1, n_blk),
        in_specs=[pl.BlockSpec((ROW_BLOCK, F), lambda n, m, be, nx: (m, 0)),
                  pl.BlockSpec(memory_space=pl.ANY),
                  pl.BlockSpec((None, 1, tn), lambda n, m, be, nx: (ex(be, m), 0, n))],
        out_specs=pl.BlockSpec((ROW_BLOCK, tn // 2), lambda n, m, be, nx: (m, n)),
        scratch_shapes=[pltpu.VMEM((2, 1, F, tn), F32), pltpu.VMEM((F, tn), BF16),
                        pltpu.SemaphoreType.DMA((2, 1)), pltpu.SMEM((1,), I32)])
    return pl.pallas_call(
        functools.partial(_expert_down_kernel, n_experts=E),
        grid_spec=grid_spec,
        out_shape=jax.ShapeDtypeStruct((P, D // 2), jnp.uint32),
        compiler_params=_cparams(("arbitrary", "arbitrary")),
        name="expert_down",
    )(blk_e, nxt_e, act, w_d, b_d)


def _combine_kernel(dest_ref, dest_next_ref, x1_ref, tw_ref, g_ref, ypad_ref, o_ref, ybuf_ref, sem):
    tm = x1_ref.shape[0]
    i = pl.program_id(0)
    slot = i % 2

    def gather(d_ref, s):
        def issue(t, _):
            for k in range(TOP_K):
                _row_copy(ypad_ref, d_ref[k, t], ybuf_ref.at[s, k], t, sem.at[s]).start(priority=k % 2)
            return 0
        lax.fori_loop(0, tm, issue, 0)

    @pl.when(i == 0)
    def _():
        gather(dest_ref, 0)

    @pl.when(i + 1 < pl.num_programs(0))
    def _():
        gather(dest_next_ref, 1 - slot)

    def drain(t, _):
        for k in range(TOP_K):
            _row_copy(ypad_ref, 0, ybuf_ref.at[slot, k], 0, sem.at[slot]).wait()
        return 0
    lax.fori_loop(0, tm, drain, 0)

    tw = tw_ref[...]
    half = ybuf_ref.shape[-1]
    acc_lo = x1_ref[:, :half]
    acc_hi = x1_ref[:, half:]
    for k in range(TOP_K):
        y_lo, y_hi = _unpack_bf16_pairs(ybuf_ref[slot, k])
        acc_lo = acc_lo + tw[:, k:k + 1] * y_lo
        acc_hi = acc_hi + tw[:, k:k + 1] * y_hi
    acc = jnp.concatenate([acc_lo, acc_hi], axis=1)
    o_ref[...] = acc * lax.rsqrt(jnp.mean(acc * acc, axis=-1, keepdims=True) + NORM_EPS) * g_ref[...]


def _combine(dest, x1, top_w, g, y_pad, tm):
    T, D = x1.shape
    n = T // tm
    return pl.pallas_call(
        _combine_kernel,
        grid=(n,),
        in_specs=[pl.BlockSpec((8, tm), lambda i: (0, i), memory_space=pltpu.SMEM),
                  pl.BlockSpec((8, tm), lambda i: (0, jnp.minimum(i + 1, n - 1)), memory_space=pltpu.SMEM),
                  pl.BlockSpec((tm, D), lambda i: (i, 0)),
                  pl.BlockSpec((tm, HEAD_DIM), lambda i: (i, 0)),
                  pl.BlockSpec((1, D), lambda i: (0, 0)),
                  pl.BlockSpec(memory_space=pl.ANY)],
        out_specs=pl.BlockSpec((tm, D), lambda i: (i, 0)),
        out_shape=jax.ShapeDtypeStruct((T, D), F32),
        scratch_shapes=[pltpu.VMEM((2, TOP_K, tm) + y_pad.shape[1:], y_pad.dtype), pltpu.SemaphoreType.DMA((2,))],
        compiler_params=_cparams(("arbitrary",)),
        name="combine",
    )(dest, dest, x1, top_w, g, y_pad)


def _rope_tables(pos):
    inv_freq = ROPE_THETA ** (-jnp.arange(0, ROPE_DIM, 2, dtype=F32) / ROPE_DIM)
    ang = pos.astype(F32)[..., None] * inv_freq
    cos, sin = jnp.cos(ang), jnp.sin(ang)
    rest = HEAD_DIM - ROPE_DIM
    cos_t = jnp.concatenate([cos, cos, jnp.ones(cos.shape[:-1] + (rest,), F32)], axis=-1)
    sin_t = jnp.concatenate([-sin, sin, jnp.zeros(sin.shape[:-1] + (rest,), F32)], axis=-1)
    return cos_t, sin_t


def _cmp_to_slc_t(seq, nc_pad):
    n_cmp = (seq - CMP_BLOCK) // CMP_STRIDE + 1
    n_slc = seq // SEL_BLOCK
    cmp_start = np.arange(n_cmp) * CMP_STRIDE
    slc_start = np.arange(n_slc) * SEL_BLOCK
    overlap = np.clip(np.minimum(cmp_start[:, None] + CMP_BLOCK, slc_start[None, :] + SEL_BLOCK)
                      - np.maximum(cmp_start[:, None], slc_start[None, :]), 0, None)
    out = np.zeros((n_slc, nc_pad), np.float32)
    out[:, :n_cmp] = (overlap / CMP_BLOCK).T
    return jnp.asarray(out, BF16)


def _regroup_w_in(w):
    D = w.shape[0]
    o_kv = NSA_WIDTH
    o_gn = o_kv + 6 * KV_WIDTH
    o_b = o_gn + 3 * NSA_HEADS
    o_gm = o_b + 3 * MOBA_WIDTH
    kv = lambda s: w[:, o_kv + s * KV_WIDTH:o_kv + (s + 1) * KV_WIDTH]
    mb = lambda s: w[:, o_b + s * MOBA_WIDTH:o_b + (s + 1) * MOBA_WIDTH]
    cols = [w[:, :NSA_WIDTH], kv(2), kv(4), mb(0), mb(1), kv(0), kv(1), kv(3), kv(5), mb(2), w[:, o_gm:o_gm + 2 * D]]
    w_main = jnp.concatenate(cols, axis=1).astype(BF16)
    wg = w[:, o_gn:o_gn + 3 * NSA_HEADS]
    wg = wg.reshape(D, 3, NSA_GROUPS, NSA_REP).transpose(2, 1, 3, 0).reshape(NSA_GROUPS, 3 * NSA_REP, D)
    wg = jnp.pad(wg, ((0, 0), (0, GATE_ROWS - 3 * NSA_REP), (0, 0))).reshape(NSA_GROUPS * GATE_ROWS, D)
    return w_main, wg.astype(BF16)


def kernel(x, positions, g_attn_norm, w_in, pe_cmp_k, w_cmp_k1, w_cmp_k2, pe_cmp_v, w_cmp_v1, w_cmp_v2, w_proj_nsa, w_proj_moba, w_out, g_ffn_norm, w_router, b_router, w_gate_up, b_gate_up, w_down, b_down, g_final_norm):
    B, S, D = x.shape
    T = B * S
    E = w_router.shape[-1]
    F = w_down.shape[-2]
    depth = w_in.shape[0]
    assert depth == 1, "the final norm is fused into the single layer's combine step"
    assert S % MOBA_BLOCK == 0 and S >= WINDOW + NSA_TQ and (2 * D) % COL_TILE == 0 and T % 512 == 0
    nc = S // CMP_STRIDE
    tm_big = min(1024, T)

    cos_t, sin_t = _rope_tables(positions)
    cmp_end = np.arange(nc) * CMP_STRIDE + CMP_BLOCK - 1
    cmp_end = np.minimum(cmp_end, S - 1)
    cos_c, sin_c = cos_t[:, cmp_end], sin_t[:, cmp_end]
    c2st = _cmp_to_slc_t(S, nc)

    x2 = x.reshape(T, D)
    for layer in range(depth):
        w_main, w_gates_t = _regroup_w_in(w_in[layer])
        h, gates_t = _norm_gates(x2, g_attn_norm[layer].reshape(1, D), w_gates_t, tm_big)
        proj = _in_proj(h, w_main, cos_t.reshape(T, HEAD_DIM), sin_t.reshape(T, HEAD_DIM), min(2048, T))
        proj3 = proj.reshape(B, S, proj.shape[1])

        xc = proj3[:, :, OFF_KCMP:OFF_KCMP + 2 * KV_WIDTH].reshape(B, nc, CMP_STRIDE * 2 * KV_WIDTH)
        kc, vc = _compress(xc, pe_cmp_k[layer], w_cmp_k1[layer].astype(BF16), w_cmp_k2[layer].astype(BF16),
                           pe_cmp_v[layer], w_cmp_v1[layer].astype(BF16), w_cmp_v2[layer].astype(BF16), cos_c, sin_c)
        o_a = _nsa_attention(proj3, kc, vc, gates_t, c2st)
        o_b = _moba_attention(proj3)

        mix = _merge_proj(o_a.reshape(T, NSA_WIDTH), o_b.reshape(T, MOBA_WIDTH),
                          w_proj_nsa[layer].astype(BF16), w_proj_moba[layer].astype(BF16), proj, tm_big)
        wr_t = w_router[layer].T
        wr_hi = wr_t.astype(BF16)
        wr_lo = (wr_t - wr_hi.astype(F32)).astype(BF16)
        x1, h2, top_e, top_w = _out_router(x2, mix, w_out[layer].astype(BF16), g_ffn_norm[layer].reshape(1, D),
                                           wr_hi, wr_lo, b_router[layer].reshape(E, 1), 512)

        n_blk = (T * TOP_K) // ROW_BLOCK + E
        n_blk_pad = -(-n_blk // HEAD_DIM) * HEAD_DIM
        dest, blk_e, pad_lo, pad_hi = _route(top_e, E, n_blk_pad, 512)
        pads = jnp.stack([pad_lo[:, 0], pad_hi[:, 0]])
        x_pad = _dispatch(pads, dest, h2, n_blk * ROW_BLOCK, 512)
        blk_e = blk_e.reshape(n_blk_pad)
        nxt_e = _next_expert(blk_e, E)
        act = _expert_up(blk_e, nxt_e, x_pad, w_gate_up[layer], b_gate_up[layer].reshape(E, 1, 2 * F), n_blk,
                         min(1024, F))
        y_pad = _expert_down(blk_e, nxt_e, act, w_down[layer], b_down[layer].reshape(E, 1, D), n_blk)
        x2 = _combine(dest, x1, top_w, g_final_norm.reshape(1, D), y_pad, 256)
    return x2.reshape(B, S, D)
```

```python
import functools

import jax
import jax.numpy as jnp
import numpy as np
from jax import lax
from jax.experimental import pallas as pl
from jax.experimental.pallas import tpu as pltpu

F32 = jnp.float32
BF16 = jnp.bfloat16
I32 = jnp.int32

HEAD_DIM = 128
ROPE_DIM = HEAD_DIM // 4
ROPE_HALF = ROPE_DIM // 2
ROPE_THETA = 500000.0
NORM_EPS = 1e-5
NEG_INF = -1e30
REMOVED = -3e38
SEL_FORCED = 1e9

NSA_HEADS = 8
NSA_GROUPS = 2
NSA_REP = NSA_HEADS // NSA_GROUPS
CMP_BLOCK = 32
CMP_STRIDE = 16
CMP_HIDDEN = 256
SEL_BLOCK = 64
SEL_TOPN = 16
WINDOW = 512
MOBA_HEADS = 8
MOBA_BLOCK = 256
MOBA_TOPK = 3
TOP_K = 4
SWIGLU_LIMIT = 7.0
SWIGLU_ALPHA = 1.702
ROW_BLOCK = 256

NSA_WIDTH = NSA_HEADS * HEAD_DIM
MOBA_WIDTH = MOBA_HEADS * HEAD_DIM
KV_WIDTH = NSA_GROUPS * HEAD_DIM
SCALE = HEAD_DIM ** -0.5
Q_SCALE = SCALE * 1.4426950408889634

COL_TILE = 512
OFF_QA = 0
OFF_KSLC = NSA_WIDTH
OFF_KWIN = OFF_KSLC + KV_WIDTH
OFF_QB = OFF_KWIN + KV_WIDTH
OFF_KB = OFF_QB + MOBA_WIDTH
ROPE_COLS = OFF_KB + MOBA_WIDTH
OFF_KCMP = ROPE_COLS
OFF_VCMP = OFF_KCMP + KV_WIDTH
OFF_VSLC = OFF_VCMP + KV_WIDTH
OFF_VWIN = OFF_VSLC + KV_WIDTH
OFF_VB = OFF_VWIN + KV_WIDTH
PLAIN_END = OFF_VB + MOBA_WIDTH
N_ROPE_TILES = ROPE_COLS // COL_TILE
N_PLAIN_TILES = (PLAIN_END - ROPE_COLS) // COL_TILE
Q_TILES = tuple(range(OFF_QA // COL_TILE, OFF_KSLC // COL_TILE)) + tuple(range(OFF_QB // COL_TILE, OFF_KB // COL_TILE))

VMEM_LIMIT = 56 * 1024 * 1024
NSA_TQ = 128
SEL_KT = 512
GATE_ROWS = 16
IN_PROJ_CHUNK = 256
NSA_SUB = 4
MOBA_HEADS_PER_STEP = 4
MOBA_BLOCKS_PER_ITER = 4
WAIT_ROWS = 256


def _cparams(sem):
    return pltpu.CompilerParams(dimension_semantics=sem, vmem_limit_bytes=VMEM_LIMIT)


def _sigmoid(z):
    return 1.0 / (1.0 + jnp.exp(-z))


def _dot(a, b):
    return jnp.dot(a, b, preferred_element_type=F32)


def _dot_nt(a, b):
    return lax.dot_general(a, b, (((1,), (1,)), ((), ())), preferred_element_type=F32)


def _split_hi_lo(v):
    hi = v.astype(BF16)
    lo = (v - hi.astype(F32)).astype(BF16)
    return hi, lo


def _pack_bf16_pairs(v):
    half = v.shape[1] // 2
    lo = pltpu.bitcast(v[:, :half].astype(BF16).astype(F32), jnp.uint32)
    hi = pltpu.bitcast(v[:, half:].astype(BF16).astype(F32), jnp.uint32)
    return lax.shift_right_logical(lo, jnp.uint32(16)) | hi


def _unpack_bf16_pairs(w):
    lo = pltpu.bitcast(lax.shift_left(w, jnp.uint32(16)), F32)
    hi = pltpu.bitcast(w & jnp.uint32(0xFFFF0000), F32)
    return lo, hi


def _topk_rows(work, row_idx, n_rows, k):
    picks = []
    for _ in range(k):
        m = jnp.max(work, axis=0, keepdims=True)
        first = jnp.min(jnp.where(work == m, row_idx, n_rows), axis=0, keepdims=True)
        pick = row_idx == first
        picks.append((m, first, pick))
        work = jnp.where(pick, REMOVED, work)
    return picks


def _norm_gates_kernel(x_ref, g_ref, wgt_ref, h_ref, gt_ref):
    x = x_ref[...]
    h = x * lax.rsqrt(jnp.mean(x * x, axis=-1, keepdims=True) + NORM_EPS) * g_ref[...]
    hb = h.astype(BF16)
    h_ref[...] = hb
    gt_ref[...] = _sigmoid(_dot_nt(wgt_ref[...], hb))


def _norm_gates(x2, g, wgt, tm):
    T, D = x2.shape
    R = wgt.shape[0]
    return pl.pallas_call(
        _norm_gates_kernel,
        grid=(T // tm,),
        in_specs=[pl.BlockSpec((tm, D), lambda i: (i, 0)),
                  pl.BlockSpec((1, D), lambda i: (0, 0)),
                  pl.BlockSpec((R, D), lambda i: (0, 0))],
        out_specs=[pl.BlockSpec((tm, D), lambda i: (i, 0)),
                   pl.BlockSpec((R, tm), lambda i: (0, i))],
        out_shape=[jax.ShapeDtypeStruct((T, D), BF16), jax.ShapeDtypeStruct((R, T), F32)],
        compiler_params=_cparams(("arbitrary",)),
        name="norm_gates",
    )(x2, g, wgt)


def _rope_tile(xh, c, s, lane):
    rot = jnp.where(lane < ROPE_HALF, pltpu.roll(xh, HEAD_DIM - ROPE_HALF, 1), pltpu.roll(xh, ROPE_HALF, 1))
    return xh * c + rot * s


def _in_proj_kernel(h_ref, w_ref, c_ref, s_ref, o_ref):
    j = pl.program_id(1)
    tm = h_ref.shape[0]
    chunk = min(IN_PROJ_CHUNK, tm)

    def by_chunks(epilogue):
        for r0 in range(0, tm, chunk):
            rows = slice(r0, r0 + chunk)
            epilogue(rows, _dot(h_ref[rows, :], w_ref[...]))

    @pl.when(j < N_ROPE_TILES)
    def _():
        is_q = j == Q_TILES[0]
        for qt in Q_TILES[1:]:
            is_q = is_q | (j == qt)
        f = jnp.where(is_q, Q_SCALE, 1.0).astype(F32)
        lane = lax.broadcasted_iota(I32, (chunk, HEAD_DIM), 1)

        def rope(rows, acc):
            c = c_ref[rows, :] * f
            s = s_ref[rows, :] * f
            for hh in range(COL_TILE // HEAD_DIM):
                sl = slice(hh * HEAD_DIM, (hh + 1) * HEAD_DIM)
                o_ref[rows, sl] = _rope_tile(acc[:, sl], c, s, lane).astype(BF16)
        by_chunks(rope)

    @pl.when((j >= N_ROPE_TILES) & (j < N_ROPE_TILES + N_PLAIN_TILES))
    def _():
        def plain(rows, acc):
            o_ref[rows, :] = acc.astype(BF16)
        by_chunks(plain)

    @pl.when(j >= N_ROPE_TILES + N_PLAIN_TILES)
    def _():
        def gate(rows, acc):
            o_ref[rows, :] = _sigmoid(acc).astype(BF16)
        by_chunks(gate)


def _in_proj(h, w, cos_t, sin_t, tm):
    T, D = h.shape
    N = w.shape[1]
    return pl.pallas_call(
        _in_proj_kernel,
        grid=(T // tm, N // COL_TILE),
        in_specs=[pl.BlockSpec((tm, D), lambda i, j: (i, 0)),
                  pl.BlockSpec((D, COL_TILE), lambda i, j: (0, j)),
                  pl.BlockSpec((tm, HEAD_DIM), lambda i, j: (i, 0)),
                  pl.BlockSpec((tm, HEAD_DIM), lambda i, j: (i, 0))],
        out_specs=pl.BlockSpec((tm, COL_TILE), lambda i, j: (i, j)),
        out_shape=jax.ShapeDtypeStruct((T, N), BF16),
        compiler_params=_cparams(("arbitrary", "arbitrary")),
        name="in_proj",
    )(h, w, cos_t, sin_t)


def _gelu_tanh(x):
    return 0.5 * x * (1.0 + jnp.tanh(0.7978845608028654 * (x + 0.044715 * x * x * x)))


def _compress_kernel(x_ref, pek_ref, w1k_ref, w2k_ref, pev_ref, w1v_ref, w2v_ref, c_ref, s_ref, kc_ref, vc_ref):
    nc = x_ref.shape[0]
    half = CMP_STRIDE * HEAD_DIM
    tok_w = 2 * KV_WIDTH
    for which, (pe_ref, w1_ref, w2_ref, out_ref) in enumerate(
            ((pek_ref, w1k_ref, w2k_ref, kc_ref), (pev_ref, w1v_ref, w2v_ref, vc_ref))):
        for g in range(NSA_GROUPS):
            acc_a = jnp.zeros((nc, CMP_HIDDEN), F32)
            acc_b = jnp.zeros((nc, CMP_HIDDEN), F32)
            for l in range(CMP_STRIDE):
                off = l * tok_w + which * KV_WIDTH + g * HEAD_DIM
                xl = x_ref[:, off:off + HEAD_DIM].astype(F32)
                xa = (xl + pe_ref[l:l + 1, :]).astype(BF16)
                xb = (xl + pe_ref[CMP_STRIDE + l:CMP_STRIDE + l + 1, :]).astype(BF16)
                acc_a = acc_a + _dot(xa, w1_ref[l * HEAD_DIM:(l + 1) * HEAD_DIM, :])
                acc_b = acc_b + _dot(xb, w1_ref[half + l * HEAD_DIM:half + (l + 1) * HEAD_DIM, :])
            hid = _gelu_tanh(acc_a + pltpu.roll(acc_b, nc - 1, 0))
            out = _dot(hid.astype(BF16), w2_ref[...])
            if which == 0:
                lane = lax.broadcasted_iota(I32, out.shape, 1)
                out = _rope_tile(out, c_ref[...], s_ref[...], lane)
            out_ref[g] = out.astype(BF16)


def _compress(xc, pek, w1k, w2k, pev, w1v, w2v, cos_c, sin_c):
    B, nc, W = xc.shape
    full = lambda a: pl.BlockSpec(a.shape, lambda b: (0,) * a.ndim)
    out_sds = jax.ShapeDtypeStruct((B, NSA_GROUPS, nc, HEAD_DIM), BF16)
    out_spec = pl.BlockSpec((None, NSA_GROUPS, nc, HEAD_DIM), lambda b: (b, 0, 0, 0))
    return pl.pallas_call(
        _compress_kernel,
        grid=(B,),
        in_specs=[pl.BlockSpec((None, nc, W), lambda b: (b, 0, 0)),
                  full(pek), full(w1k), full(w2k), full(pev), full(w1v), full(w2v),
                  pl.BlockSpec((None, nc, HEAD_DIM), lambda b: (b, 0, 0)),
                  pl.BlockSpec((None, nc, HEAD_DIM), lambda b: (b, 0, 0))],
        out_specs=[out_spec, out_spec],
        out_shape=[out_sds, out_sds],
        compiler_params=_cparams(("arbitrary",)),
        name="compress",
    )(xc, pek, w1k, w2k, pev, w1v, w2v, cos_c, sin_c)


def _transpose_into(src_ref, col0, dst_ref, n_rows):
    def body(b, _):
        r0 = pl.multiple_of(b * HEAD_DIM, HEAD_DIM)
        blk = src_ref[pl.ds(r0, HEAD_DIM), col0:col0 + HEAD_DIM].astype(F32)
        dst_ref[:, pl.ds(r0, HEAD_DIM)] = blk.T.astype(BF16)
        return 0
    lax.fori_loop(0, n_rows // HEAD_DIM, body, 0)


def _topk_mask(score, k):
    n, L = score.shape
    sub = 8
    assert n % sub == 0
    groups = n // sub
    segs = [score[sub * g:sub * (g + 1), :] for g in range(groups)]
    sub_idx = lax.broadcasted_iota(I32, (sub, L), 0)
    later = [jnp.where(sub_idx > r, 1.0, 0.0) for r in range(sub)]
    ranks = [jnp.zeros((sub, L), F32) for _ in range(groups)]
    for i in range(n):
        gi, ri = divmod(i, sub)
        row = score[i:i + 1, :]
        for g in range(groups):
            if g > gi:
                inc = jnp.where(row >= segs[g], 1.0, 0.0)
            elif g < gi:
                inc = jnp.where(row > segs[g], 1.0, 0.0)
            else:
                inc = jnp.where(row > segs[g], 1.0, jnp.where(row == segs[g], later[ri], 0.0))
            ranks[g] = ranks[g] + inc
    return jnp.concatenate(ranks, axis=0) < k


def _bias_rows_to_cols(sel, lanes_out=HEAD_DIM):
    n, L = sel.shape
    bias_t = jnp.where(sel, 0.0, NEG_INF)
    if n < lanes_out:
        bias_t = jnp.concatenate([bias_t, jnp.zeros((lanes_out - n, L), F32)], axis=0)
    return bias_t.T.astype(BF16)


def _softmax_cols(s, live=None):
    m = jnp.max(s, axis=0, keepdims=True)
    p = jnp.exp2(s - m)
    inv = 1.0 / jnp.sum(p, axis=0, keepdims=True)
    if live is not None:
        inv = jnp.where(live, inv, 0.0)
    return p * inv


def _online_update(s, vt, carry):
    m, l, acc = carry
    m_new = jnp.maximum(m, jnp.max(s, axis=0, keepdims=True))
    alpha = jnp.exp2(m - m_new)
    p = jnp.exp2(s - m_new)
    l = alpha * l + jnp.sum(p, axis=0, keepdims=True)
    acc = alpha * acc + _dot(vt, p.astype(BF16))
    return m_new, l, acc


def _nsa_kernel(q_ref, kc_ref, vc_ref, ks_ref, vs_ref, kw_ref, vw_ref, gt_ref, c2st_ref, e_ref, o_ref,
                ksa_ref, vst_ref, vwt_ref, vct_ref, *, seq):
    tq = NSA_TQ
    U = NSA_SUB
    R = NSA_REP
    i = pl.program_id(2)
    base = i * (U * tq)
    nc = kc_ref.shape[0]
    n_slc = c2st_ref.shape[0]
    lanes = R * tq
    band = WINDOW + tq
    tile_r = lambda a: jnp.concatenate([a] * R, axis=1)
    rowi = lambda n: lax.broadcasted_iota(I32, (n, tq), 0)

    @pl.when(i == 0)
    def _():
        _transpose_into(vs_ref, 0, vst_ref, seq)
        _transpose_into(vw_ref, 0, vwt_ref, seq)
        _transpose_into(vc_ref, 0, vct_ref, nc)
        ksa_ref[:, 0:HEAD_DIM] = ks_ref[...]
        ksa_ref[:, HEAD_DIM:2 * HEAD_DIM] = e_ref[...]

    subs = range(U)
    heads = [[q_ref[u * tq:(u + 1) * tq, r * HEAD_DIM:(r + 1) * HEAD_DIM] for r in range(R)] for u in subs]
    q4 = [jnp.concatenate(heads[u], axis=0) for u in subs]
    tcols = [lambda n, q0=base + u * tq: q0 + lax.broadcasted_iota(I32, (n, tq), 1) for u in subs]
    starts = [pl.multiple_of(jnp.maximum(base + u * tq - WINDOW, 0), HEAD_DIM) for u in subs]

    s_c = [_dot_nt(kc_ref[...], q4[u]) for u in subs]
    s_w = [_dot_nt(kw_ref[pl.ds(starts[u], band), :], q4[u]) for u in subs]

    p_c = []
    for u in subs:
        bias_c = jnp.where(rowi(nc) * CMP_STRIDE + (CMP_BLOCK - 1) <= tcols[u](nc), 0.0, NEG_INF)
        live_c = tcols[u](1) >= CMP_BLOCK - 1
        p_c.append(_softmax_cols(s_c[u] + tile_r(bias_c), tile_r(live_c)))
    o_cmp = [_dot(vct_ref[...], p_c[u].astype(BF16)) for u in subs]

    imp = []
    for u in subs:
        p_sum = p_c[u][:, 0:tq]
        for r in range(1, R):
            p_sum = p_sum + p_c[u][:, r * tq:(r + 1) * tq]
        p_hi, p_lo = _split_hi_lo(p_sum)
        imp.append(_dot(c2st_ref[...], p_hi) + _dot(c2st_ref[...], p_lo))

    p_w = []
    for u in subs:
        dpos = tcols[u](band) - (starts[u] + rowi(band))
        bias_w = jnp.where((dpos >= 0) & (dpos < WINDOW), 0.0, NEG_INF)
        p_w.append(_softmax_cols(s_w[u] + tile_r(bias_w)))
    o_win = [_dot(vwt_ref[:, pl.ds(starts[u], band)], p_w[u].astype(BF16)) for u in subs]

    q4a = []
    for u in subs:
        jj = rowi(n_slc)
        cur = tcols[u](n_slc) // SEL_BLOCK
        valid = jj <= cur
        forced = (jj == 0) | (jj == cur) | (jj == cur - 1)
        score = jnp.where(forced, SEL_FORCED, jnp.where(valid, imp[u], -SEL_FORCED))
        sel = _topk_mask(score, min(SEL_TOPN, n_slc)) & valid
        bias = _bias_rows_to_cols(sel)
        q4a.append(jnp.concatenate([jnp.concatenate([h, bias], axis=1) for h in heads[u]], axis=0))

    def sel_body(kt, carry):
        k0 = pl.multiple_of(kt * SEL_KT, SEL_KT)
        s = [_dot_nt(ksa_ref[pl.ds(k0, SEL_KT), :], q4a[u]) for u in range(U)]
        return tuple(_online_update(s[u], vst_ref[:, pl.ds(k0, SEL_KT)], carry[u]) for u in range(U))

    n_full = base // SEL_KT
    init = (jnp.full((1, lanes), NEG_INF, F32), jnp.zeros((1, lanes), F32), jnp.zeros((HEAD_DIM, lanes), F32))
    carry = lax.fori_loop(0, n_full, sel_body, (init,) * U)
    k0 = pl.multiple_of(n_full * SEL_KT, SEL_KT)
    gt = gt_ref[...]
    assert U * tq == SEL_KT
    n_diag = [(u + 1) * tq for u in subs]
    s_d = [_dot_nt(ksa_ref[pl.ds(k0, n_diag[u]), :], q4a[u]) for u in subs]
    for u in subs:
        causal = jnp.where(k0 + rowi(n_diag[u]) <= tcols[u](n_diag[u]), 0.0, NEG_INF)
        _, l_s, acc_s = _online_update(s_d[u] + tile_r(causal), vst_ref[:, pl.ds(k0, n_diag[u])], carry[u])
        o_slc = acc_s * (1.0 / l_s)

        g = gt[:, u * tq:(u + 1) * tq]
        for r in range(R):
            sl = slice(r * tq, (r + 1) * tq)
            o_r = (g[r:r + 1, :] * o_cmp[u][:, sl] + g[R + r:R + r + 1, :] * o_slc[:, sl]
                   + g[2 * R + r:2 * R + r + 1, :] * o_win[u][:, sl])
            o_ref[u * tq:(u + 1) * tq, r * HEAD_DIM:(r + 1) * HEAD_DIM] = o_r.T.astype(BF16)


def _block_onehot(seq, block):
    assert seq // block <= HEAD_DIM
    e = np.zeros((seq, HEAD_DIM), np.float32)
    e[np.arange(seq), np.arange(seq) // block] = 1.0
    return jnp.asarray(e, BF16)


def _nsa_attention(proj3, kc, vc, gates_t, c2st):
    B, S, _ = proj3.shape
    G = NSA_GROUPS
    nc = kc.shape[2]
    tq = NSA_TQ * NSA_SUB
    assert S % tq == 0 and SEL_KT % tq == 0
    nq = S // tq
    hb = HEAD_DIM
    gw = NSA_REP * HEAD_DIM
    e_sel = _block_onehot(S, SEL_BLOCK)
    seq_spec = lambda off: pl.BlockSpec((None, S, hb), lambda b, g, i: (b, 0, off // hb + g))
    cmp_spec = pl.BlockSpec((None, None, nc, hb), lambda b, g, i: (b, g, 0, 0))
    return pl.pallas_call(
        functools.partial(_nsa_kernel, seq=S),
        grid=(B, G, nq),
        in_specs=[pl.BlockSpec((None, tq, gw), lambda b, g, i: (b, i, OFF_QA // gw + g)),
                  cmp_spec, cmp_spec,
                  seq_spec(OFF_KSLC), seq_spec(OFF_VSLC), seq_spec(OFF_KWIN), seq_spec(OFF_VWIN),
                  pl.BlockSpec((GATE_ROWS, tq), lambda b, g, i: (g, b * nq + i)),
                  pl.BlockSpec(c2st.shape, lambda b, g, i: (0, 0)),
                  pl.BlockSpec(e_sel.shape, lambda b, g, i: (0, 0))],
        out_specs=pl.BlockSpec((None, tq, gw), lambda b, g, i: (b, i, g)),
        out_shape=jax.ShapeDtypeStruct((B, S, NSA_WIDTH), BF16),
        scratch_shapes=[pltpu.VMEM((S, 2 * hb), BF16), pltpu.VMEM((hb, S), BF16), pltpu.VMEM((hb, S), BF16),
                        pltpu.VMEM((hb, nc), BF16)],
        compiler_params=_cparams(("arbitrary", "arbitrary", "arbitrary")),
        name="nsa_attn",
    )(proj3, kc, vc, proj3, proj3, proj3, proj3, gates_t, c2st, e_sel)


def _moba_kernel(q_ref, k_ref, v_ref, e_ref, o_ref, ka_ref, vt_ref, kmh_ref, kml_ref, *, seq):
    blk = MOBA_BLOCK
    HP = MOBA_HEADS_PER_STEP
    i = pl.program_id(2)
    nb = seq // blk
    nbp = kmh_ref.shape[1]
    hsl = lambda h: slice(h * HEAD_DIM, (h + 1) * HEAD_DIM)

    @pl.when(i == 0)
    def _():
        for h in range(HP):
            _transpose_into(v_ref, h * HEAD_DIM, vt_ref.at[h], seq)
            ka_ref[h, :, 0:HEAD_DIM] = k_ref[:, hsl(h)]
            ka_ref[h, :, HEAD_DIM:2 * HEAD_DIM] = e_ref[...]
            means = [jnp.sum(k_ref[n * blk:(n + 1) * blk, hsl(h)].astype(F32), axis=0, keepdims=True) * (1.0 / blk)
                     for n in range(nb)]
            if nbp > nb:
                means.append(jnp.zeros((nbp - nb, HEAD_DIM), F32))
            hi, lo = _split_hi_lo(jnp.concatenate(means, axis=0))
            kmh_ref[h] = hi
            kml_ref[h] = lo

    own0 = pl.multiple_of(i * blk, blk)
    past = lax.broadcasted_iota(I32, (nbp, blk), 0) < i
    causal = jnp.where(lax.broadcasted_iota(I32, (blk, blk), 0) <= lax.broadcasted_iota(I32, (blk, blk), 1), 0.0, NEG_INF)
    hs = range(HP)
    q = [q_ref[:, hsl(h)] for h in hs]
    gate = [_dot_nt(kmh_ref[h], q[h]) + _dot_nt(kml_ref[h], q[h]) for h in hs]
    s_own = [_dot_nt(k_ref[pl.ds(own0, blk), hsl(h)], q[h]) for h in hs]
    m_own = [jnp.max(s_own[h] + causal, axis=0, keepdims=True) for h in hs]
    p_own = [jnp.exp2(s_own[h] + causal - m_own[h]) for h in hs]
    carry = [(m_own[h], jnp.sum(p_own[h], axis=0, keepdims=True),
              _dot(vt_ref[h, :, pl.ds(own0, blk)], p_own[h].astype(BF16))) for h in hs]
    qa = []
    for h in hs:
        sel = _topk_mask(jnp.where(past, gate[h], -SEL_FORCED), min(MOBA_TOPK, nb)) & past
        qa.append(jnp.concatenate([q[h], _bias_rows_to_cols(sel)], axis=1))

    per_iter = MOBA_BLOCKS_PER_ITER
    kt = per_iter * blk

    def body(j, carry):
        k0 = pl.multiple_of(j * kt, kt)
        s = [_dot_nt(ka_ref[h, pl.ds(k0, kt), :], qa[h]) for h in range(HP)]
        return tuple(_online_update(s[h], vt_ref[h, :, pl.ds(k0, kt)], carry[h]) for h in range(HP))

    carry = lax.fori_loop(0, (i + per_iter - 1) // per_iter, body, tuple(carry))
    for h in range(HP):
        _, l, acc = carry[h]
        o_ref[:, hsl(h)] = (acc * (1.0 / l)).T.astype(BF16)


def _moba_attention(proj3):
    B, S, _ = proj3.shape
    HP = MOBA_HEADS_PER_STEP
    blk = MOBA_BLOCK
    hb = HEAD_DIM
    assert (S // blk) % MOBA_BLOCKS_PER_ITER == 0 and MOBA_HEADS % HP == 0
    nbp = max(8, -(-(S // blk) // 8) * 8)
    e_blk = _block_onehot(S, blk)
    wide = HP * hb
    seq_spec = lambda off: pl.BlockSpec((None, S, wide), lambda b, h, i: (b, 0, off // wide + h))
    return pl.pallas_call(
        functools.partial(_moba_kernel, seq=S),
        grid=(B, MOBA_HEADS // HP, S // blk),
        in_specs=[pl.BlockSpec((None, blk, wide), lambda b, h, i: (b, i, OFF_QB // wide + h)),
                  seq_spec(OFF_KB), seq_spec(OFF_VB),
                  pl.BlockSpec(e_blk.shape, lambda b, h, i: (0, 0))],
        out_specs=pl.BlockSpec((None, blk, wide), lambda b, h, i: (b, i, h)),
        out_shape=jax.ShapeDtypeStruct((B, S, MOBA_WIDTH), BF16),
        scratch_shapes=[pltpu.VMEM((HP, S, 2 * hb), BF16), pltpu.VMEM((HP, hb, S), BF16),
                        pltpu.VMEM((HP, nbp, hb), BF16), pltpu.VMEM((HP, nbp, hb), BF16)],
        compiler_params=_cparams(("arbitrary", "arbitrary", "arbitrary")),
        name="moba_attn",
    )(proj3, proj3, proj3, e_blk)


def _merge_proj_kernel(oa_ref, ob_ref, wa_ref, wb_ref, ga_ref, gb_ref, o_ref):
    a = _dot(oa_ref[...], wa_ref[...])
    b = _dot(ob_ref[...], wb_ref[...])
    o_ref[...] = (ga_ref[...].astype(F32) * a + gb_ref[...].astype(F32) * b).astype(BF16)


def _merge_proj(oa, ob, wa, wb, proj, tm):
    T = oa.shape[0]
    D = wa.shape[1]
    tn = min(COL_TILE, D)
    g0 = PLAIN_END // tn
    return pl.pallas_call(
        _merge_proj_kernel,
        grid=(T // tm, D // tn),
        in_specs=[pl.BlockSpec((tm, NSA_WIDTH), lambda i, j: (i, 0)),
                  pl.BlockSpec((tm, MOBA_WIDTH), lambda i, j: (i, 0)),
                  pl.BlockSpec((NSA_WIDTH, tn), lambda i, j: (0, j)),
                  pl.BlockSpec((MOBA_WIDTH, tn), lambda i, j: (0, j)),
                  pl.BlockSpec((tm, tn), lambda i, j: (i, g0 + j)),
                  pl.BlockSpec((tm, tn), lambda i, j: (i, g0 + D // tn + j))],
        out_specs=pl.BlockSpec((tm, tn), lambda i, j: (i, j)),
        out_shape=jax.ShapeDtypeStruct((T, D), BF16),
        compiler_params=_cparams(("arbitrary", "arbitrary")),
        name="merge_proj",
    )(oa, ob, wa, wb, proj, proj)


def _out_router_kernel(x_ref, mix_ref, wo_ref, g_ref, wrh_ref, wrl_ref, br_ref, x1_ref, h2_ref, te_ref, tw_ref):
    tm = x_ref.shape[0]
    E = wrh_ref.shape[0]
    x1 = x_ref[...] + _dot(mix_ref[...], wo_ref[...])
    x1_ref[...] = x1
    h2 = x1 * lax.rsqrt(jnp.mean(x1 * x1, axis=-1, keepdims=True) + NORM_EPS) * g_ref[...]
    h2_ref[...] = _pack_bf16_pairs(h2)
    h_hi, h_lo = _split_hi_lo(h2)
    logits = (_dot_nt(wrh_ref[...], h_hi) + _dot_nt(wrh_ref[...], h_lo) + _dot_nt(wrl_ref[...], h_hi)
              + br_ref[...])
    e_idx = lax.broadcasted_iota(I32, (E, tm), 0)
    picks = _topk_rows(logits, e_idx, E, TOP_K)
    vals = [p[0] for p in picks]
    exps = [jnp.exp(v - vals[0]) for v in vals]
    inv = 1.0 / (exps[0] + exps[1] + exps[2] + exps[3])
    te_ref[...] = jnp.concatenate([p[1] for p in picks] + [jnp.zeros((8 - TOP_K, tm), I32)], axis=0)
    w_t = jnp.concatenate([e * inv for e in exps] + [jnp.zeros((HEAD_DIM - TOP_K, tm), F32)], axis=0)
    tw_ref[...] = w_t.T


def _out_router(x2, mix, wo, g, wrh, wrl, br, tm):
    T, D = x2.shape
    E = wrh.shape[0]
    full = lambda a: pl.BlockSpec(a.shape, lambda i: (0,) * a.ndim)
    return pl.pallas_call(
        _out_router_kernel,
        grid=(T // tm,),
        in_specs=[pl.BlockSpec((tm, D), lambda i: (i, 0)), pl.BlockSpec((tm, D), lambda i: (i, 0)),
                  full(wo), full(g), full(wrh), full(wrl), full(br)],
        out_specs=[pl.BlockSpec((tm, D), lambda i: (i, 0)), pl.BlockSpec((tm, D // 2), lambda i: (i, 0)),
                   pl.BlockSpec((8, tm), lambda i: (0, i)), pl.BlockSpec((tm, HEAD_DIM), lambda i: (i, 0))],
        out_shape=[jax.ShapeDtypeStruct((T, D), F32), jax.ShapeDtypeStruct((T, D // 2), jnp.uint32),
                   jax.ShapeDtypeStruct((8, T), I32), jax.ShapeDtypeStruct((T, HEAD_DIM), F32)],
        compiler_params=_cparams(("arbitrary",)),
        name="out_router",
    )(x2, mix, wo, g, wrh, wrl, br)


def _route_kernel(te_ref, dest_ref, blke_ref, padlo_ref, padhi_ref, run_ref, pstart_ref):
    ph = pl.program_id(0)
    i = pl.program_id(1)
    E = run_ref.shape[0]
    tm = te_ref.shape[1]
    nbp = blke_ref.shape[1]
    te = te_ref[...]
    e_idx = lax.broadcasted_iota(I32, (E, tm), 0)
    ohs = [te[k:k + 1, :] == e_idx for k in range(TOP_K)]
    oh = jnp.where(ohs[0] | ohs[1] | ohs[2] | ohs[3], 1.0, 0.0)
    tile_cnt = jnp.sum(oh, axis=1, keepdims=True)

    @pl.when((ph == 0) & (i == 0))
    def _():
        run_ref[...] = jnp.zeros(run_ref.shape, F32)

    @pl.when(ph == 0)
    def _():
        run_ref[...] = run_ref[...] + tile_cnt
        dest_ref[...] = jnp.zeros(dest_ref.shape, I32)

    @pl.when((ph == 1) & (i == 0))
    def _():
        counts = run_ref[...]
        padded = jnp.floor((counts + (ROW_BLOCK - 1)) * (1.0 / ROW_BLOCK)) * ROW_BLOCK
        row = lax.broadcasted_iota(I32, counts.shape, 0)
        incl = padded
        sh = 1
        while sh < E:
            incl = incl + jnp.where(row >= sh, pltpu.roll(incl, sh, 0), 0.0)
            sh *= 2
        pstart_ref[...] = incl - padded
        padlo_ref[...] = (incl - padded + counts).astype(I32)
        padhi_ref[...] = incl.astype(I32)
        blk_start = (lax.broadcasted_iota(I32, (E, nbp), 1) * ROW_BLOCK).astype(F32)
        blke_ref[...] = jnp.sum(jnp.where(incl[:, 0:1] <= blk_start, 1, 0), axis=0, keepdims=True).astype(I32)
        run_ref[...] = jnp.zeros(run_ref.shape, F32)

    @pl.when(ph == 1)
    def _():
        upper = jnp.where(lax.broadcasted_iota(I32, (tm, tm), 0) < lax.broadcasted_iota(I32, (tm, tm), 1), 1.0, 0.0)
        before = _dot(oh.astype(BF16), upper.astype(BF16))
        val = before + pstart_ref[:, 0:1] + run_ref[:, 0:1]
        rows = [jnp.sum(jnp.where(ohs[k], val, 0.0), axis=0, keepdims=True) for k in range(TOP_K)]
        dest_ref[...] = jnp.concatenate(rows + [jnp.zeros((8 - TOP_K, tm), F32)], axis=0).astype(I32)
        run_ref[...] = run_ref[...] + tile_cnt


def _route(top_e, n_experts, n_blk_pad, tm):
    T = top_e.shape[1]
    return pl.pallas_call(
        _route_kernel,
        grid=(2, T // tm),
        in_specs=[pl.BlockSpec((8, tm), lambda ph, i: (0, i))],
        out_specs=[pl.BlockSpec((8, tm), lambda ph, i: (0, i * ph)),
                   pl.BlockSpec((1, n_blk_pad), lambda ph, i: (0, 0)),
                   pl.BlockSpec((n_experts, HEAD_DIM), lambda ph, i: (0, 0)),
                   pl.BlockSpec((n_experts, HEAD_DIM), lambda ph, i: (0, 0))],
        out_shape=[jax.ShapeDtypeStruct((8, T), I32), jax.ShapeDtypeStruct((1, n_blk_pad), I32),
                   jax.ShapeDtypeStruct((n_experts, HEAD_DIM), I32), jax.ShapeDtypeStruct((n_experts, HEAD_DIM), I32)],
        scratch_shapes=[pltpu.VMEM((n_experts, HEAD_DIM), F32), pltpu.VMEM((n_experts, HEAD_DIM), F32)],
        compiler_params=_cparams(("arbitrary", "arbitrary")),
        name="route",
    )(top_e)


def _row_copy(src_ref, src_row, dst_ref, dst_row, sem):
    return pltpu.make_async_copy(src_ref.at[pl.ds(src_row, 1), :], dst_ref.at[pl.ds(dst_row, 1), :], sem)


def _wait_rows(rows_hbm_ref, n_rows, sem):
    assert n_rows % WAIT_ROWS == 0
    chunk = rows_hbm_ref.at[pl.ds(0, WAIT_ROWS), :]
    for _ in range(n_rows // WAIT_ROWS):
        pltpu.make_async_copy(chunk, chunk, sem).wait()


def _dispatch_kernel(pad_ref, dest_ref, h_ref, xpad_ref, zero_ref, sem, zsem):
    tm = h_ref.shape[0]

    @pl.when(pl.program_id(0) == 0)
    def _():
        zero_ref[...] = jnp.zeros(zero_ref.shape, zero_ref.dtype)

        def per_expert(e, _):
            lo, hi = pad_ref[0, e], pad_ref[1, e]

            def issue(r, _):
                _row_copy(zero_ref, 0, xpad_ref, r, zsem).start()
                return 0
            lax.fori_loop(lo, hi, issue, 0)

            def drain(r, _):
                _row_copy(zero_ref, 0, xpad_ref, 0, zsem).wait()
                return 0
            lax.fori_loop(lo, hi, drain, 0)
            return 0
        lax.fori_loop(0, pad_ref.shape[1], per_expert, 0)

    def issue(t, _):
        for k in range(TOP_K):
            _row_copy(h_ref, t, xpad_ref, dest_ref[k, t], sem).start(priority=k % 2)
        return 0
    lax.fori_loop(0, tm, issue, 0)

    _wait_rows(xpad_ref, TOP_K * tm, sem)


def _dispatch(pads, dest, h2, n_rows, tm):
    T, D = h2.shape
    grid_spec = pltpu.PrefetchScalarGridSpec(
        num_scalar_prefetch=1,
        grid=(T // tm,),
        in_specs=[pl.BlockSpec((8, tm), lambda i, pads: (0, i), memory_space=pltpu.SMEM),
                  pl.BlockSpec((tm, D), lambda i, pads: (i, 0))],
        out_specs=pl.BlockSpec(memory_space=pl.ANY),
        scratch_shapes=[pltpu.VMEM((8, D), h2.dtype), pltpu.SemaphoreType.DMA(()), pltpu.SemaphoreType.DMA(())])
    return pl.pallas_call(
        _dispatch_kernel,
        grid_spec=grid_spec,
        out_shape=jax.ShapeDtypeStruct((n_rows, D), h2.dtype),
        compiler_params=_cparams(("arbitrary",)),
        name="dispatch",
    )(pads, dest, h2)


def _next_expert(blk_e, n_experts):
    nxt = jnp.min(jnp.where(blk_e[None, :] > blk_e[:, None], blk_e[None, :], n_experts), axis=1)
    return jnp.where(nxt < n_experts, nxt, -1).astype(I32)


def _stream_expert_weights(be_ref, nxt_ref, w_hbm, col_offsets, wbuf_ref, wb_refs, sem, cnt_ref, n_experts):
    n, m = pl.program_id(0), pl.program_id(1)
    tn = wb_refs[0].shape[1]
    e = be_ref[m]
    live = e < n_experts
    first = live & ((m == 0) | (be_ref[jnp.maximum(m - 1, 0)] != e))

    def copies(ex, col_tile, slot):
        c0 = pl.multiple_of(col_tile * tn, tn)
        return [pltpu.make_async_copy(w_hbm.at[ex, :, pl.ds(off + c0, tn)], wbuf_ref.at[slot, g], sem.at[slot, g])
                for g, off in enumerate(col_offsets)]

    @pl.when((n == 0) & (m == 0))
    def _():
        cnt_ref[0] = 0
        for c in copies(e, 0, 0):
            c.start()

    @pl.when(first)
    def _():
        slot = cnt_ref[0] % 2
        for c in copies(e, n, slot):
            c.wait()
        nxt = nxt_ref[m]

        @pl.when(nxt >= 0)
        def _():
            for c in copies(nxt, n, 1 - slot):
                c.start()

        @pl.when((nxt < 0) & (n + 1 < pl.num_programs(0)))
        def _():
            for c in copies(be_ref[0], n + 1, 1 - slot):
                c.start()

        for g, wb_ref in enumerate(wb_refs):
            wb_ref[...] = wbuf_ref[slot, g].astype(BF16)
        cnt_ref[0] = cnt_ref[0] + 1
    return live


def _expert_up_kernel(be_ref, nxt_ref, x_ref, w_hbm, bg_ref, bl_ref, act_ref, wbuf_ref, wgb_ref, wlb_ref, sem, cnt_ref,
                      *, n_experts, d_ff):
    live = _stream_expert_weights(be_ref, nxt_ref, w_hbm, (0, d_ff), wbuf_ref, (wgb_ref, wlb_ref), sem, cnt_ref,
                                  n_experts)

    @pl.when(live)
    def _():
        x_lo, x_hi = _unpack_bf16_pairs(x_ref[...])
        x = jnp.concatenate([x_lo.astype(BF16), x_hi.astype(BF16)], axis=1)
        gate =jnp.minimum(_dot(x, wgb_ref[...]) + bg_ref[...], SWIGLU_LIMIT)
        lin = jnp.clip(_dot(x, wlb_ref[...]) + bl_ref[...], -SWIGLU_LIMIT, SWIGLU_LIMIT)
        act_ref[...] = (gate * _sigmoid(SWIGLU_ALPHA * gate) * (lin + 1.0)).astype(BF16)

    @pl.when(jnp.logical_not(live))
    def _():
        act_ref[...] = jnp.zeros(act_ref.shape, BF16)


def _expert_up(blk_e, nxt_e, x_pad, w_gu, b_gu, n_blk, tn):
    P = x_pad.shape[0]
    E, D = w_gu.shape[0], w_gu.shape[1]
    F = w_gu.shape[2] // 2
    nf = F // tn
    ex = lambda be, m: jnp.minimum(be[m], E - 1)
    grid_spec = pltpu.PrefetchScalarGridSpec(
        num_scalar_prefetch=2,
        grid=(nf, n_blk),
        in_specs=[pl.BlockSpec((ROW_BLOCK, D // 2), lambda n, m, be, nx: (m, 0)),
                  pl.BlockSpec(memory_space=pl.ANY),
                  pl.BlockSpec((None, 1, tn), lambda n, m, be, nx: (ex(be, m), 0, n)),
                  pl.BlockSpec((None, 1, tn), lambda n, m, be, nx: (ex(be, m), 0, nf + n))],
        out_specs=pl.BlockSpec((ROW_BLOCK, tn), lambda n, m, be, nx: (m, n)),
        scratch_shapes=[pltpu.VMEM((2, 2, D, tn), F32), pltpu.VMEM((D, tn), BF16), pltpu.VMEM((D, tn), BF16),
                        pltpu.SemaphoreType.DMA((2, 2)), pltpu.SMEM((1,), I32)])
    return pl.pallas_call(
        functools.partial(_expert_up_kernel, n_experts=E, d_ff=F),
        grid_spec=grid_spec,
        out_shape=jax.ShapeDtypeStruct((P, F), BF16),
        compiler_params=_cparams(("arbitrary", "arbitrary")),
        name="expert_up",
    )(blk_e, nxt_e, x_pad, w_gu, b_gu, b_gu)


def _expert_down_kernel(be_ref, nxt_ref, a_ref, w_hbm, b_ref, y_ref, wbuf_ref, wb_ref, sem, cnt_ref, *, n_experts):
    live = _stream_expert_weights(be_ref, nxt_ref, w_hbm, (0,), wbuf_ref, (wb_ref,), sem, cnt_ref, n_experts)

    @pl.when(live)
    def _():
        y_ref[...] = _pack_bf16_pairs(_dot(a_ref[...], wb_ref[...]) + b_ref[...])

    @pl.when(jnp.logical_not(live))
    def _():
        y_ref[...] = jnp.zeros(y_ref.shape, y_ref.dtype)


def _expert_down(blk_e, nxt_e, act, w_d, b_d, n_blk):
    P, F = act.shape
    E, _, D = w_d.shape
    tn = D
    ex = lambda be, m: jnp.minimum(be[m], E - 1)
    grid_spec = pltpu.PrefetchScalarGridSpec(
        num_scalar_prefetch=2,
        grid=(1, n_blk),
        in_specs=[pl.BlockSpec((ROW_BLOCK, F), lambda n, m, be, nx: (m, 0)),
                  pl.BlockSpec(memory_space=pl.ANY),
                  pl.BlockSpec((None, 1, tn), lambda n, m, be, nx: (ex(be, m), 0, n))],
        out_specs=pl.BlockSpec((ROW_BLOCK, tn // 2), lambda n, m, be, nx: (m, n)),
        scratch_shapes=[pltpu.VMEM((2, 1, F, tn), F32), pltpu.VMEM((F, tn), BF16),
                        pltpu.SemaphoreType.DMA((2, 1)), pltpu.SMEM((1,), I32)])
    return pl.pallas_call(
        functools.partial(_expert_down_kernel, n_experts=E),
        grid_spec=grid_spec,
        out_shape=jax.ShapeDtypeStruct((P, D // 2), jnp.uint32),
        compiler_params=_cparams(("arbitrary", "arbitrary")),
        name="expert_down",
    )(blk_e, nxt_e, act, w_d, b_d)


def _combine_kernel(dest_ref, dest_next_ref, x1_ref, tw_ref, g_ref, ypad_ref, o_ref, ybuf_ref, sem):
    tm = x1_ref.shape[0]
    i = pl.program_id(0)
    slot = i % 2

    def gather(d_ref, s):
        def issue(t, _):
            for k in range(TOP_K):
                _row_copy(ypad_ref, d_ref[k, t], ybuf_ref.at[s, k], t, sem.at[s]).start(priority=k % 2)
            return 0
        lax.fori_loop(0, tm, issue, 0)

    @pl.when(i == 0)
    def _():
        gather(dest_ref, 0)

    @pl.when(i + 1 < pl.num_programs(0))
    def _():
        gather(dest_next_ref, 1 - slot)

    _wait_rows(ypad_ref, TOP_K * tm, sem.at[slot])

    tw = tw_ref[...]
    half = ybuf_ref.shape[-1]
    acc_lo = x1_ref[:, :half]
    acc_hi = x1_ref[:, half:]
    for k in range(TOP_K):
        y_lo, y_hi = _unpack_bf16_pairs(ybuf_ref[slot, k])
        acc_lo = acc_lo + tw[:, k:k + 1] * y_lo
        acc_hi = acc_hi + tw[:, k:k + 1] * y_hi
    acc = jnp.concatenate([acc_lo, acc_hi], axis=1)
    o_ref[...] = acc * lax.rsqrt(jnp.mean(acc * acc, axis=-1, keepdims=True) + NORM_EPS) * g_ref[...]


def _combine(dest, x1, top_w, g, y_pad, tm):
    T, D = x1.shape
    n = T // tm
    return pl.pallas_call(
        _combine_kernel,
        grid=(n,),
        in_specs=[pl.BlockSpec((8, tm), lambda i: (0, i), memory_space=pltpu.SMEM),
                  pl.BlockSpec((8, tm), lambda i: (0, jnp.minimum(i + 1, n - 1)), memory_space=pltpu.SMEM),
                  pl.BlockSpec((tm, D), lambda i: (i, 0)),
                  pl.BlockSpec((tm, HEAD_DIM), lambda i: (i, 0)),
                  pl.BlockSpec((1, D), lambda i: (0, 0)),
                  pl.BlockSpec(memory_space=pl.ANY)],
        out_specs=pl.BlockSpec((tm, D), lambda i: (i, 0)),
        out_shape=jax.ShapeDtypeStruct((T, D), F32),
        scratch_shapes=[pltpu.VMEM((2, TOP_K, tm) + y_pad.shape[1:], y_pad.dtype), pltpu.SemaphoreType.DMA((2,))],
        compiler_params=_cparams(("arbitrary",)),
        name="combine",
    )(dest, dest, x1, top_w, g, y_pad)


def _rope_tables(pos):
    inv_freq = ROPE_THETA ** (-jnp.arange(0, ROPE_DIM, 2, dtype=F32) / ROPE_DIM)
    ang = pos.astype(F32)[..., None] * inv_freq
    cos, sin = jnp.cos(ang), jnp.sin(ang)
    rest = HEAD_DIM - ROPE_DIM
    cos_t = jnp.concatenate([cos, cos, jnp.ones(cos.shape[:-1] + (rest,), F32)], axis=-1)
    sin_t = jnp.concatenate([-sin, sin, jnp.zeros(sin.shape[:-1] + (rest,), F32)], axis=-1)
    return cos_t, sin_t


def _cmp_to_slc_t(seq, nc_pad):
    n_cmp = (seq - CMP_BLOCK) // CMP_STRIDE + 1
    n_slc = seq // SEL_BLOCK
    cmp_start = np.arange(n_cmp) * CMP_STRIDE
    slc_start = np.arange(n_slc) * SEL_BLOCK
    overlap = np.clip(np.minimum(cmp_start[:, None] + CMP_BLOCK, slc_start[None, :] + SEL_BLOCK)
                      - np.maximum(cmp_start[:, None], slc_start[None, :]), 0, None)
    out = np.zeros((n_slc, nc_pad), np.float32)
    out[:, :n_cmp] = (overlap / CMP_BLOCK).T
    return jnp.asarray(out, BF16)


def _regroup_w_in(w):
    D = w.shape[0]
    o_kv = NSA_WIDTH
    o_gn = o_kv + 6 * KV_WIDTH
    o_b = o_gn + 3 * NSA_HEADS
    o_gm = o_b + 3 * MOBA_WIDTH
    kv = lambda s: w[:, o_kv + s * KV_WIDTH:o_kv + (s + 1) * KV_WIDTH]
    mb = lambda s: w[:, o_b + s * MOBA_WIDTH:o_b + (s + 1) * MOBA_WIDTH]
    cols = [w[:, :NSA_WIDTH], kv(2), kv(4), mb(0), mb(1), kv(0), kv(1), kv(3), kv(5), mb(2), w[:, o_gm:o_gm + 2 * D]]
    w_main = jnp.concatenate(cols, axis=1).astype(BF16)
    wg = w[:, o_gn:o_gn + 3 * NSA_HEADS]
    wg = wg.reshape(D, 3, NSA_GROUPS, NSA_REP).transpose(2, 1, 3, 0).reshape(NSA_GROUPS, 3 * NSA_REP, D)
    wg = jnp.pad(wg, ((0, 0), (0, GATE_ROWS - 3 * NSA_REP), (0, 0))).reshape(NSA_GROUPS * GATE_ROWS, D)
    return w_main, wg.astype(BF16)


def kernel(x, positions, g_attn_norm, w_in, pe_cmp_k, w_cmp_k1, w_cmp_k2, pe_cmp_v, w_cmp_v1, w_cmp_v2, w_proj_nsa, w_proj_moba, w_out, g_ffn_norm, w_router, b_router, w_gate_up, b_gate_up, w_down, b_down, g_final_norm):
    B, S, D = x.shape
    T = B * S
    E = w_router.shape[-1]
    F = w_down.shape[-2]
    depth = w_in.shape[0]
    assert depth == 1, "the final norm is fused into the single layer's combine step"
    assert S % MOBA_BLOCK == 0 and S >= WINDOW + NSA_TQ and (2 * D) % COL_TILE == 0 and T % 512 == 0
    nc = S // CMP_STRIDE
    tm_big = min(1024, T)

    cos_t, sin_t = _rope_tables(positions)
    cmp_end = np.arange(nc) * CMP_STRIDE + CMP_BLOCK - 1
    cmp_end = np.minimum(cmp_end, S - 1)
    cos_c, sin_c = cos_t[:, cmp_end], sin_t[:, cmp_end]
    c2st = _cmp_to_slc_t(S, nc)

    x2 = x.reshape(T, D)
    for layer in range(depth):
        w_main, w_gates_t = _regroup_w_in(w_in[layer])
        h, gates_t = _norm_gates(x2, g_attn_norm[layer].reshape(1, D), w_gates_t, tm_big)
        proj = _in_proj(h, w_main, cos_t.reshape(T, HEAD_DIM), sin_t.reshape(T, HEAD_DIM), min(2048, T))
        proj3 = proj.reshape(B, S, proj.shape[1])

        xc = proj3[:, :, OFF_KCMP:OFF_KCMP + 2 * KV_WIDTH].reshape(B, nc, CMP_STRIDE * 2 * KV_WIDTH)
        kc, vc = _compress(xc, pe_cmp_k[layer], w_cmp_k1[layer].astype(BF16), w_cmp_k2[layer].astype(BF16),
                           pe_cmp_v[layer], w_cmp_v1[layer].astype(BF16), w_cmp_v2[layer].astype(BF16), cos_c, sin_c)
        o_a = _nsa_attention(proj3, kc, vc, gates_t, c2st)
        o_b = _moba_attention(proj3)

        mix = _merge_proj(o_a.reshape(T, NSA_WIDTH), o_b.reshape(T, MOBA_WIDTH),
                          w_proj_nsa[layer].astype(BF16), w_proj_moba[layer].astype(BF16), proj, tm_big)
        wr_t = w_router[layer].T
        wr_hi = wr_t.astype(BF16)
        wr_lo = (wr_t - wr_hi.astype(F32)).astype(BF16)
        x1, h2, top_e, top_w = _out_router(x2, mix, w_out[layer].astype(BF16), g_ffn_norm[layer].reshape(1, D),
                                           wr_hi, wr_lo, b_router[layer].reshape(E, 1), 512)

        n_blk = (T * TOP_K) // ROW_BLOCK + E
        n_blk_pad = -(-n_blk // HEAD_DIM) * HEAD_DIM
        dest, blk_e, pad_lo, pad_hi = _route(top_e, E, n_blk_pad, 512)
        pads = jnp.stack([pad_lo[:, 0], pad_hi[:, 0]])
        x_pad = _dispatch(pads, dest, h2, n_blk * ROW_BLOCK, 512)
        blk_e = blk_e.reshape(n_blk_pad)
        nxt_e = _next_expert(blk_e, E)
        act = _expert_up(blk_e, nxt_e, x_pad, w_gate_up[layer], b_gate_up[layer].reshape(E, 1, 2 * F), n_blk,
                         min(1024, F))
        y_pad = _expert_down(blk_e, nxt_e, act, w_down[layer], b_down[layer].reshape(E, 1, D), n_blk)
        x2 = _combine(dest, x1, top_w, g_final_norm.reshape(1, D), y_pad, 256)
    return x2.reshape(B, S, D)
```

```python
import functools

import jax
import jax.numpy as jnp
import numpy as np
from jax import lax
from jax.experimental import pallas as pl
from jax.experimental.pallas import tpu as pltpu

F32 = jnp.float32
BF16 = jnp.bfloat16
I32 = jnp.int32

HEAD_DIM = 128
ROPE_DIM = HEAD_DIM // 4
ROPE_HALF = ROPE_DIM // 2
ROPE_THETA = 500000.0
NORM_EPS = 1e-5
NEG_INF = -1e30
REMOVED = -3e38
SEL_FORCED = 1e9

NSA_HEADS = 8
NSA_GROUPS = 2
NSA_REP = NSA_HEADS // NSA_GROUPS
CMP_BLOCK = 32
CMP_STRIDE = 16
CMP_HIDDEN = 256
SEL_BLOCK = 64
SEL_TOPN = 16
WINDOW = 512
MOBA_HEADS = 8
MOBA_BLOCK = 256
MOBA_TOPK = 3
TOP_K = 4
SWIGLU_LIMIT = 7.0
SWIGLU_ALPHA = 1.702
ROW_BLOCK = 256

NSA_WIDTH = NSA_HEADS * HEAD_DIM
MOBA_WIDTH = MOBA_HEADS * HEAD_DIM
KV_WIDTH = NSA_GROUPS * HEAD_DIM
SCALE = HEAD_DIM ** -0.5
Q_SCALE = SCALE * 1.4426950408889634

COL_TILE = 512
OFF_QA = 0
OFF_KSLC = NSA_WIDTH
OFF_KWIN = OFF_KSLC + KV_WIDTH
OFF_QB = OFF_KWIN + KV_WIDTH
OFF_KB = OFF_QB + MOBA_WIDTH
ROPE_COLS = OFF_KB + MOBA_WIDTH
OFF_KCMP = ROPE_COLS
OFF_VCMP = OFF_KCMP + KV_WIDTH
OFF_VSLC = OFF_VCMP + KV_WIDTH
OFF_VWIN = OFF_VSLC + KV_WIDTH
OFF_VB = OFF_VWIN + KV_WIDTH
PLAIN_END = OFF_VB + MOBA_WIDTH
N_ROPE_TILES = ROPE_COLS // COL_TILE
N_PLAIN_TILES = (PLAIN_END - ROPE_COLS) // COL_TILE
Q_TILES = tuple(range(OFF_QA // COL_TILE, OFF_KSLC // COL_TILE)) + tuple(range(OFF_QB // COL_TILE, OFF_KB // COL_TILE))

VMEM_LIMIT = 56 * 1024 * 1024
NSA_TQ = 128
SEL_KT = 512
GATE_ROWS = 16
IN_PROJ_CHUNK = 256
ROW_CHUNK = 256
NSA_SUB = 4
MOBA_HEADS_PER_STEP = 4
MOBA_BLOCKS_PER_ITER = 4
WAIT_ROWS = 256


def _cparams(sem):
    return pltpu.CompilerParams(dimension_semantics=sem, vmem_limit_bytes=VMEM_LIMIT)


def _sigmoid(z):
    return 1.0 / (1.0 + jnp.exp(-z))


def _dot(a, b):
    return jnp.dot(a, b, preferred_element_type=F32)


def _dot_nt(a, b):
    return lax.dot_general(a, b, (((1,), (1,)), ((), ())), preferred_element_type=F32)


def _split_hi_lo(v):
    hi = v.astype(BF16)
    lo = (v - hi.astype(F32)).astype(BF16)
    return hi, lo


def _pack_bf16_pairs(v):
    half = v.shape[1] // 2
    lo = pltpu.bitcast(v[:, :half].astype(BF16).astype(F32), jnp.uint32)
    hi = pltpu.bitcast(v[:, half:].astype(BF16).astype(F32), jnp.uint32)
    return lax.shift_right_logical(lo, jnp.uint32(16)) | hi


def _unpack_bf16_pairs(w):
    lo = pltpu.bitcast(lax.shift_left(w, jnp.uint32(16)), F32)
    hi = pltpu.bitcast(w & jnp.uint32(0xFFFF0000), F32)
    return lo, hi


def _topk_rows(work, row_idx, n_rows, k):
    picks = []
    for _ in range(k):
        m = jnp.max(work, axis=0, keepdims=True)
        first = jnp.min(jnp.where(work == m, row_idx, n_rows), axis=0, keepdims=True)
        pick = row_idx == first
        picks.append((m, first, pick))
        work = jnp.where(pick, REMOVED, work)
    return picks


def _norm_gates_kernel(x_ref, g_ref, wgt_ref, h_ref, gt_ref):
    x = x_ref[...]
    h = x * lax.rsqrt(jnp.mean(x * x, axis=-1, keepdims=True) + NORM_EPS) * g_ref[...]
    hb = h.astype(BF16)
    h_ref[...] = hb
    gt_ref[...] = _sigmoid(_dot_nt(wgt_ref[...], hb))


def _norm_gates(x2, g, wgt, tm):
    T, D = x2.shape
    R = wgt.shape[0]
    return pl.pallas_call(
        _norm_gates_kernel,
        grid=(T // tm,),
        in_specs=[pl.BlockSpec((tm, D), lambda i: (i, 0)),
                  pl.BlockSpec((1, D), lambda i: (0, 0)),
                  pl.BlockSpec((R, D), lambda i: (0, 0))],
        out_specs=[pl.BlockSpec((tm, D), lambda i: (i, 0)),
                   pl.BlockSpec((R, tm), lambda i: (0, i))],
        out_shape=[jax.ShapeDtypeStruct((T, D), BF16), jax.ShapeDtypeStruct((R, T), F32)],
        compiler_params=_cparams(("arbitrary",)),
        name="norm_gates",
    )(x2, g, wgt)


def _rope_tile(xh, c, s, lane):
    rot = jnp.where(lane < ROPE_HALF, pltpu.roll(xh, HEAD_DIM - ROPE_HALF, 1), pltpu.roll(xh, ROPE_HALF, 1))
    return xh * c + rot * s


def _in_proj_kernel(h_ref, w_ref, c_ref, s_ref, o_ref):
    j = pl.program_id(1)
    tm = h_ref.shape[0]
    chunk = min(IN_PROJ_CHUNK, tm)

    def by_chunks(epilogue):
        for r0 in range(0, tm, chunk):
            rows = slice(r0, r0 + chunk)
            epilogue(rows, _dot(h_ref[rows, :], w_ref[...]))

    @pl.when(j < N_ROPE_TILES)
    def _():
        is_q = j == Q_TILES[0]
        for qt in Q_TILES[1:]:
            is_q = is_q | (j == qt)
        f = jnp.where(is_q, Q_SCALE, 1.0).astype(F32)
        lane = lax.broadcasted_iota(I32, (chunk, HEAD_DIM), 1)

        def rope(rows, acc):
            c = c_ref[rows, :] * f
            s = s_ref[rows, :] * f
            for hh in range(COL_TILE // HEAD_DIM):
                sl = slice(hh * HEAD_DIM, (hh + 1) * HEAD_DIM)
                o_ref[rows, sl] = _rope_tile(acc[:, sl], c, s, lane).astype(BF16)
        by_chunks(rope)

    @pl.when((j >= N_ROPE_TILES) & (j < N_ROPE_TILES + N_PLAIN_TILES))
    def _():
        def plain(rows, acc):
            o_ref[rows, :] = acc.astype(BF16)
        by_chunks(plain)

    @pl.when(j >= N_ROPE_TILES + N_PLAIN_TILES)
    def _():
        def gate(rows, acc):
            o_ref[rows, :] = _sigmoid(acc).astype(BF16)
        by_chunks(gate)


def _in_proj(h, w, cos_t, sin_t, tm):
    T, D = h.shape
    N = w.shape[1]
    return pl.pallas_call(
        _in_proj_kernel,
        grid=(T // tm, N // COL_TILE),
        in_specs=[pl.BlockSpec((tm, D), lambda i, j: (i, 0)),
                  pl.BlockSpec((D, COL_TILE), lambda i, j: (0, j)),
                  pl.BlockSpec((tm, HEAD_DIM), lambda i, j: (i, 0)),
                  pl.BlockSpec((tm, HEAD_DIM), lambda i, j: (i, 0))],
        out_specs=pl.BlockSpec((tm, COL_TILE), lambda i, j: (i, j)),
        out_shape=jax.ShapeDtypeStruct((T, N), BF16),
        compiler_params=_cparams(("arbitrary", "arbitrary")),
        name="in_proj",
    )(h, w, cos_t, sin_t)


def _gelu_tanh(x):
    return 0.5 * x * (1.0 + jnp.tanh(0.7978845608028654 * (x + 0.044715 * x * x * x)))


def _compress_kernel(x_ref, pek_ref, w1k_ref, w2k_ref, pev_ref, w1v_ref, w2v_ref, c_ref, s_ref, kc_ref, vc_ref):
    nc = x_ref.shape[0]
    half = CMP_STRIDE * HEAD_DIM
    tok_w = 2 * KV_WIDTH
    for which, (pe_ref, w1_ref, w2_ref, out_ref) in enumerate(
            ((pek_ref, w1k_ref, w2k_ref, kc_ref), (pev_ref, w1v_ref, w2v_ref, vc_ref))):
        for g in range(NSA_GROUPS):
            acc_a = jnp.zeros((nc, CMP_HIDDEN), F32)
            acc_b = jnp.zeros((nc, CMP_HIDDEN), F32)
            for l in range(CMP_STRIDE):
                off = l * tok_w + which * KV_WIDTH + g * HEAD_DIM
                xl = x_ref[:, off:off + HEAD_DIM].astype(F32)
                xa = (xl + pe_ref[l:l + 1, :]).astype(BF16)
                xb = (xl + pe_ref[CMP_STRIDE + l:CMP_STRIDE + l + 1, :]).astype(BF16)
                acc_a = acc_a + _dot(xa, w1_ref[l * HEAD_DIM:(l + 1) * HEAD_DIM, :])
                acc_b = acc_b + _dot(xb, w1_ref[half + l * HEAD_DIM:half + (l + 1) * HEAD_DIM, :])
            hid = _gelu_tanh(acc_a + pltpu.roll(acc_b, nc - 1, 0))
            out = _dot(hid.astype(BF16), w2_ref[...])
            if which == 0:
                lane = lax.broadcasted_iota(I32, out.shape, 1)
                out = _rope_tile(out, c_ref[...], s_ref[...], lane)
            out_ref[g] = out.astype(BF16)


def _compress(xc, pek, w1k, w2k, pev, w1v, w2v, cos_c, sin_c):
    B, nc, W = xc.shape
    full = lambda a: pl.BlockSpec(a.shape, lambda b: (0,) * a.ndim)
    out_sds = jax.ShapeDtypeStruct((B, NSA_GROUPS, nc, HEAD_DIM), BF16)
    out_spec = pl.BlockSpec((None, NSA_GROUPS, nc, HEAD_DIM), lambda b: (b, 0, 0, 0))
    return pl.pallas_call(
        _compress_kernel,
        grid=(B,),
        in_specs=[pl.BlockSpec((None, nc, W), lambda b: (b, 0, 0)),
                  full(pek), full(w1k), full(w2k), full(pev), full(w1v), full(w2v),
                  pl.BlockSpec((None, nc, HEAD_DIM), lambda b: (b, 0, 0)),
                  pl.BlockSpec((None, nc, HEAD_DIM), lambda b: (b, 0, 0))],
        out_specs=[out_spec, out_spec],
        out_shape=[out_sds, out_sds],
        compiler_params=_cparams(("arbitrary",)),
        name="compress",
    )(xc, pek, w1k, w2k, pev, w1v, w2v, cos_c, sin_c)


def _transpose_into(src_ref, col0, dst_ref, n_rows):
    def body(b, _):
        r0 = pl.multiple_of(b * HEAD_DIM, HEAD_DIM)
        blk = src_ref[pl.ds(r0, HEAD_DIM), col0:col0 + HEAD_DIM].astype(F32)
        dst_ref[:, pl.ds(r0, HEAD_DIM)] = blk.T.astype(BF16)
        return 0
    lax.fori_loop(0, n_rows // HEAD_DIM, body, 0)


def _topk_mask(score, k):
    n, L = score.shape
    sub = 8
    assert n % sub == 0
    groups = n // sub
    segs = [score[sub * g:sub * (g + 1), :] for g in range(groups)]
    sub_idx = lax.broadcasted_iota(I32, (sub, L), 0)
    later = [jnp.where(sub_idx > r, 1.0, 0.0) for r in range(sub)]
    ranks = [jnp.zeros((sub, L), F32) for _ in range(groups)]
    for i in range(n):
        gi, ri = divmod(i, sub)
        row = score[i:i + 1, :]
        for g in range(groups):
            if g > gi:
                inc = jnp.where(row >= segs[g], 1.0, 0.0)
            elif g < gi:
                inc = jnp.where(row > segs[g], 1.0, 0.0)
            else:
                inc = jnp.where(row > segs[g], 1.0, jnp.where(row == segs[g], later[ri], 0.0))
            ranks[g] = ranks[g] + inc
    return jnp.concatenate(ranks, axis=0) < k


def _bias_rows_to_cols(sel, lanes_out=HEAD_DIM):
    n, L = sel.shape
    bias_t = jnp.where(sel, 0.0, NEG_INF)
    if n < lanes_out:
        bias_t = jnp.concatenate([bias_t, jnp.zeros((lanes_out - n, L), F32)], axis=0)
    return bias_t.T.astype(BF16)


def _softmax_cols(s, live=None):
    m = jnp.max(s, axis=0, keepdims=True)
    p = jnp.exp2(s - m)
    inv = 1.0 / jnp.sum(p, axis=0, keepdims=True)
    if live is not None:
        inv = jnp.where(live, inv, 0.0)
    return p * inv


def _online_update(s, vt, carry):
    m, l, acc = carry
    m_new = jnp.maximum(m, jnp.max(s, axis=0, keepdims=True))
    alpha = jnp.exp2(m - m_new)
    p = jnp.exp2(s - m_new)
    l = alpha * l + jnp.sum(p, axis=0, keepdims=True)
    acc = alpha * acc + _dot(vt, p.astype(BF16))
    return m_new, l, acc


def _nsa_kernel(q_ref, kc_ref, vc_ref, ks_ref, vs_ref, kw_ref, vw_ref, gt_ref, c2st_ref, e_ref, o_ref,
                ksa_ref, vst_ref, vwt_ref, vct_ref, *, seq):
    tq = NSA_TQ
    U = NSA_SUB
    R = NSA_REP
    i = pl.program_id(2)
    base = i * (U * tq)
    nc = kc_ref.shape[0]
    n_slc = c2st_ref.shape[0]
    lanes = R * tq
    band = WINDOW + tq
    tile_r = lambda a: jnp.concatenate([a] * R, axis=1)
    rowi = lambda n: lax.broadcasted_iota(I32, (n, tq), 0)

    @pl.when(i == 0)
    def _():
        _transpose_into(vs_ref, 0, vst_ref, seq)
        _transpose_into(vw_ref, 0, vwt_ref, seq)
        _transpose_into(vc_ref, 0, vct_ref, nc)
        ksa_ref[:, 0:HEAD_DIM] = ks_ref[...]
        ksa_ref[:, HEAD_DIM:2 * HEAD_DIM] = e_ref[...]

    subs = range(U)
    heads = [[q_ref[u * tq:(u + 1) * tq, r * HEAD_DIM:(r + 1) * HEAD_DIM] for r in range(R)] for u in subs]
    q4 = [jnp.concatenate(heads[u], axis=0) for u in subs]
    tcols = [lambda n, q0=base + u * tq: q0 + lax.broadcasted_iota(I32, (n, tq), 1) for u in subs]
    starts = [pl.multiple_of(jnp.maximum(base + u * tq - WINDOW, 0), HEAD_DIM) for u in subs]

    s_c = [_dot_nt(kc_ref[...], q4[u]) for u in subs]
    s_w = [_dot_nt(kw_ref[pl.ds(starts[u], band), :], q4[u]) for u in subs]

    p_c = []
    for u in subs:
        bias_c = jnp.where(rowi(nc) * CMP_STRIDE + (CMP_BLOCK - 1) <= tcols[u](nc), 0.0, NEG_INF)
        live_c = tcols[u](1) >= CMP_BLOCK - 1
        p_c.append(_softmax_cols(s_c[u] + tile_r(bias_c), tile_r(live_c)))
    o_cmp = [_dot(vct_ref[...], p_c[u].astype(BF16)) for u in subs]

    imp = []
    for u in subs:
        p_sum = p_c[u][:, 0:tq]
        for r in range(1, R):
            p_sum = p_sum + p_c[u][:, r * tq:(r + 1) * tq]
        p_hi, p_lo = _split_hi_lo(p_sum)
        imp.append(_dot(c2st_ref[...], p_hi) + _dot(c2st_ref[...], p_lo))

    p_w = []
    for u in subs:
        dpos = tcols[u](band) - (starts[u] + rowi(band))
        bias_w = jnp.where((dpos >= 0) & (dpos < WINDOW), 0.0, NEG_INF)
        p_w.append(_softmax_cols(s_w[u] + tile_r(bias_w)))
    o_win = [_dot(vwt_ref[:, pl.ds(starts[u], band)], p_w[u].astype(BF16)) for u in subs]

    q4a = []
    for u in subs:
        jj = rowi(n_slc)
        cur = tcols[u](n_slc) // SEL_BLOCK
        valid = jj <= cur
        forced = (jj == 0) | (jj == cur) | (jj == cur - 1)
        score = jnp.where(forced, SEL_FORCED, jnp.where(valid, imp[u], -SEL_FORCED))
        sel = _topk_mask(score, min(SEL_TOPN, n_slc)) & valid
        bias = _bias_rows_to_cols(sel)
        q4a.append(jnp.concatenate([jnp.concatenate([h, bias], axis=1) for h in heads[u]], axis=0))

    def sel_body(kt, carry):
        k0 = pl.multiple_of(kt * SEL_KT, SEL_KT)
        s = [_dot_nt(ksa_ref[pl.ds(k0, SEL_KT), :], q4a[u]) for u in range(U)]
        return tuple(_online_update(s[u], vst_ref[:, pl.ds(k0, SEL_KT)], carry[u]) for u in range(U))

    n_full = base // SEL_KT
    init = (jnp.full((1, lanes), NEG_INF, F32), jnp.zeros((1, lanes), F32), jnp.zeros((HEAD_DIM, lanes), F32))
    carry = lax.fori_loop(0, n_full, sel_body, (init,) * U)
    k0 = pl.multiple_of(n_full * SEL_KT, SEL_KT)
    gt = gt_ref[...]
    assert U * tq == SEL_KT
    n_diag = [(u + 1) * tq for u in subs]
    s_d = [_dot_nt(ksa_ref[pl.ds(k0, n_diag[u]), :], q4a[u]) for u in subs]
    for u in subs:
        causal = jnp.where(k0 + rowi(n_diag[u]) <= tcols[u](n_diag[u]), 0.0, NEG_INF)
        _, l_s, acc_s = _online_update(s_d[u] + tile_r(causal), vst_ref[:, pl.ds(k0, n_diag[u])], carry[u])
        o_slc = acc_s * (1.0 / l_s)

        g = gt[:, u * tq:(u + 1) * tq]
        for r in range(R):
            sl = slice(r * tq, (r + 1) * tq)
            o_r = (g[r:r + 1, :] * o_cmp[u][:, sl] + g[R + r:R + r + 1, :] * o_slc[:, sl]
                   + g[2 * R + r:2 * R + r + 1, :] * o_win[u][:, sl])
            o_ref[u * tq:(u + 1) * tq, r * HEAD_DIM:(r + 1) * HEAD_DIM] = o_r.T.astype(BF16)


def _block_onehot(seq, block):
    assert seq // block <= HEAD_DIM
    e = np.zeros((seq, HEAD_DIM), np.float32)
    e[np.arange(seq), np.arange(seq) // block] = 1.0
    return jnp.asarray(e, BF16)


def _nsa_attention(proj3, kc, vc, gates_t, c2st):
    B, S, _ = proj3.shape
    G = NSA_GROUPS
    nc = kc.shape[2]
    tq = NSA_TQ * NSA_SUB
    assert S % tq == 0 and SEL_KT % tq == 0
    nq = S // tq
    hb = HEAD_DIM
    gw = NSA_REP * HEAD_DIM
    e_sel = _block_onehot(S, SEL_BLOCK)
    seq_spec = lambda off: pl.BlockSpec((None, S, hb), lambda b, g, i: (b, 0, off // hb + g))
    cmp_spec = pl.BlockSpec((None, None, nc, hb), lambda b, g, i: (b, g, 0, 0))
    return pl.pallas_call(
        functools.partial(_nsa_kernel, seq=S),
        grid=(B, G, nq),
        in_specs=[pl.BlockSpec((None, tq, gw), lambda b, g, i: (b, i, OFF_QA // gw + g)),
                  cmp_spec, cmp_spec,
                  seq_spec(OFF_KSLC), seq_spec(OFF_VSLC), seq_spec(OFF_KWIN), seq_spec(OFF_VWIN),
                  pl.BlockSpec((GATE_ROWS, tq), lambda b, g, i: (g, b * nq + i)),
                  pl.BlockSpec(c2st.shape, lambda b, g, i: (0, 0)),
                  pl.BlockSpec(e_sel.shape, lambda b, g, i: (0, 0))],
        out_specs=pl.BlockSpec((None, tq, gw), lambda b, g, i: (b, i, g)),
        out_shape=jax.ShapeDtypeStruct((B, S, NSA_WIDTH), BF16),
        scratch_shapes=[pltpu.VMEM((S, 2 * hb), BF16), pltpu.VMEM((hb, S), BF16), pltpu.VMEM((hb, S), BF16),
                        pltpu.VMEM((hb, nc), BF16)],
        compiler_params=_cparams(("arbitrary", "arbitrary", "arbitrary")),
        name="nsa_attn",
    )(proj3, kc, vc, proj3, proj3, proj3, proj3, gates_t, c2st, e_sel)


def _moba_kernel(q_ref, k_ref, v_ref, e_ref, o_ref, ka_ref, vt_ref, kmh_ref, kml_ref, *, seq):
    blk = MOBA_BLOCK
    HP = MOBA_HEADS_PER_STEP
    i = pl.program_id(2)
    nb = seq // blk
    nbp = kmh_ref.shape[1]
    hsl = lambda h: slice(h * HEAD_DIM, (h + 1) * HEAD_DIM)

    @pl.when(i == 0)
    def _():
        for h in range(HP):
            _transpose_into(v_ref, h * HEAD_DIM, vt_ref.at[h], seq)
            ka_ref[h, :, 0:HEAD_DIM] = k_ref[:, hsl(h)]
            ka_ref[h, :, HEAD_DIM:2 * HEAD_DIM] = e_ref[...]
            means = [jnp.sum(k_ref[n * blk:(n + 1) * blk, hsl(h)].astype(F32), axis=0, keepdims=True) * (1.0 / blk)
                     for n in range(nb)]
            if nbp > nb:
                means.append(jnp.zeros((nbp - nb, HEAD_DIM), F32))
            hi, lo = _split_hi_lo(jnp.concatenate(means, axis=0))
            kmh_ref[h] = hi
            kml_ref[h] = lo

    own0 = pl.multiple_of(i * blk, blk)
    past = lax.broadcasted_iota(I32, (nbp, blk), 0) < i
    causal = jnp.where(lax.broadcasted_iota(I32, (blk, blk), 0) <= lax.broadcasted_iota(I32, (blk, blk), 1), 0.0, NEG_INF)
    hs = range(HP)
    q = [q_ref[:, hsl(h)] for h in hs]
    gate = [_dot_nt(kmh_ref[h], q[h]) + _dot_nt(kml_ref[h], q[h]) for h in hs]
    s_own = [_dot_nt(k_ref[pl.ds(own0, blk), hsl(h)], q[h]) for h in hs]
    m_own = [jnp.max(s_own[h] + causal, axis=0, keepdims=True) for h in hs]
    p_own = [jnp.exp2(s_own[h] + causal - m_own[h]) for h in hs]
    carry = [(m_own[h], jnp.sum(p_own[h], axis=0, keepdims=True),
              _dot(vt_ref[h, :, pl.ds(own0, blk)], p_own[h].astype(BF16))) for h in hs]
    qa = []
    for h in hs:
        sel = _topk_mask(jnp.where(past, gate[h], -SEL_FORCED), min(MOBA_TOPK, nb)) & past
        qa.append(jnp.concatenate([q[h], _bias_rows_to_cols(sel)], axis=1))

    per_iter = MOBA_BLOCKS_PER_ITER
    kt = per_iter * blk

    def body(j, carry):
        k0 = pl.multiple_of(j * kt, kt)
        s = [_dot_nt(ka_ref[h, pl.ds(k0, kt), :], qa[h]) for h in range(HP)]
        return tuple(_online_update(s[h], vt_ref[h, :, pl.ds(k0, kt)], carry[h]) for h in range(HP))

    carry = lax.fori_loop(0, (i + per_iter - 1) // per_iter, body, tuple(carry))
    for h in range(HP):
        _, l, acc = carry[h]
        o_ref[:, hsl(h)] = (acc * (1.0 / l)).T.astype(BF16)


def _moba_attention(proj3):
    B, S, _ = proj3.shape
    HP = MOBA_HEADS_PER_STEP
    blk = MOBA_BLOCK
    hb = HEAD_DIM
    assert (S // blk) % MOBA_BLOCKS_PER_ITER == 0 and MOBA_HEADS % HP == 0
    nbp = max(8, -(-(S // blk) // 8) * 8)
    e_blk = _block_onehot(S, blk)
    wide = HP * hb
    seq_spec = lambda off: pl.BlockSpec((None, S, wide), lambda b, h, i: (b, 0, off // wide + h))
    return pl.pallas_call(
        functools.partial(_moba_kernel, seq=S),
        grid=(B, MOBA_HEADS // HP, S // blk),
        in_specs=[pl.BlockSpec((None, blk, wide), lambda b, h, i: (b, i, OFF_QB // wide + h)),
                  seq_spec(OFF_KB), seq_spec(OFF_VB),
                  pl.BlockSpec(e_blk.shape, lambda b, h, i: (0, 0))],
        out_specs=pl.BlockSpec((None, blk, wide), lambda b, h, i: (b, i, h)),
        out_shape=jax.ShapeDtypeStruct((B, S, MOBA_WIDTH), BF16),
        scratch_shapes=[pltpu.VMEM((HP, S, 2 * hb), BF16), pltpu.VMEM((HP, hb, S), BF16),
                        pltpu.VMEM((HP, nbp, hb), BF16), pltpu.VMEM((HP, nbp, hb), BF16)],
        compiler_params=_cparams(("arbitrary", "arbitrary", "arbitrary")),
        name="moba_attn",
    )(proj3, proj3, proj3, e_blk)


def _merge_proj_kernel(oa_ref, ob_ref, wa_ref, wb_ref, ga_ref, gb_ref, o_ref):
    tm = oa_ref.shape[0]
    chunk = min(ROW_CHUNK, tm)
    for r0 in range(0, tm, chunk):
        rows = slice(r0, r0 + chunk)
        a = _dot(oa_ref[rows, :], wa_ref[...])
        b = _dot(ob_ref[rows, :], wb_ref[...])
        o_ref[rows, :] = (ga_ref[rows, :].astype(F32) * a + gb_ref[rows, :].astype(F32) * b).astype(BF16)


def _merge_proj(oa, ob, wa, wb, proj, tm):
    T = oa.shape[0]
    D = wa.shape[1]
    tn = min(COL_TILE, D)
    g0 = PLAIN_END // tn
    return pl.pallas_call(
        _merge_proj_kernel,
        grid=(T // tm, D // tn),
        in_specs=[pl.BlockSpec((tm, NSA_WIDTH), lambda i, j: (i, 0)),
                  pl.BlockSpec((tm, MOBA_WIDTH), lambda i, j: (i, 0)),
                  pl.BlockSpec((NSA_WIDTH, tn), lambda i, j: (0, j)),
                  pl.BlockSpec((MOBA_WIDTH, tn), lambda i, j: (0, j)),
                  pl.BlockSpec((tm, tn), lambda i, j: (i, g0 + j)),
                  pl.BlockSpec((tm, tn), lambda i, j: (i, g0 + D // tn + j))],
        out_specs=pl.BlockSpec((tm, tn), lambda i, j: (i, j)),
        out_shape=jax.ShapeDtypeStruct((T, D), BF16),
        compiler_params=_cparams(("arbitrary", "arbitrary")),
        name="merge_proj",
    )(oa, ob, wa, wb, proj, proj)


def _out_router_kernel(x_ref, mix_ref, wo_ref, g_ref, wrh_ref, wrl_ref, br_ref, x1_ref, h2_ref, te_ref, tw_ref):
    tm = x_ref.shape[0]
    E = wrh_ref.shape[0]
    chunk = min(ROW_CHUNK, tm)
    for r0 in range(0, tm, chunk):
        rows = slice(r0, r0 + chunk)
        x1 = x_ref[rows, :] + _dot(mix_ref[rows, :], wo_ref[...])
        x1_ref[rows, :] = x1
        h2 = x1 * lax.rsqrt(jnp.mean(x1 * x1, axis=-1, keepdims=True) + NORM_EPS) * g_ref[...]
        h2_ref[rows, :] = _pack_bf16_pairs(h2)
        h_hi, h_lo = _split_hi_lo(h2)
        logits = (_dot_nt(wrh_ref[...], h_hi) + _dot_nt(wrh_ref[...], h_lo) + _dot_nt(wrl_ref[...], h_hi)
                  + br_ref[...])
        e_idx = lax.broadcasted_iota(I32, (E, chunk), 0)
        picks = _topk_rows(logits, e_idx, E, TOP_K)
        vals = [p[0] for p in picks]
        exps = [jnp.exp(v - vals[0]) for v in vals]
        inv = 1.0 / (exps[0] + exps[1] + exps[2] + exps[3])
        te_ref[:, rows] = jnp.concatenate([p[1] for p in picks] + [jnp.zeros((8 - TOP_K, chunk), I32)], axis=0)
        w_t = jnp.concatenate([e * inv for e in exps] + [jnp.zeros((HEAD_DIM - TOP_K, chunk), F32)], axis=0)
        tw_ref[rows, :] = w_t.T


def _out_router(x2, mix, wo, g, wrh, wrl, br, tm):
    T, D = x2.shape
    E = wrh.shape[0]
    full = lambda a: pl.BlockSpec(a.shape, lambda i: (0,) * a.ndim)
    return pl.pallas_call(
        _out_router_kernel,
        grid=(T // tm,),
        in_specs=[pl.BlockSpec((tm, D), lambda i: (i, 0)), pl.BlockSpec((tm, D), lambda i: (i, 0)),
                  full(wo), full(g), full(wrh), full(wrl), full(br)],
        out_specs=[pl.BlockSpec((tm, D), lambda i: (i, 0)), pl.BlockSpec((tm, D // 2), lambda i: (i, 0)),
                   pl.BlockSpec((8, tm), lambda i: (0, i)), pl.BlockSpec((tm, HEAD_DIM), lambda i: (i, 0))],
        out_shape=[jax.ShapeDtypeStruct((T, D), F32), jax.ShapeDtypeStruct((T, D // 2), jnp.uint32),
                   jax.ShapeDtypeStruct((8, T), I32), jax.ShapeDtypeStruct((T, HEAD_DIM), F32)],
        compiler_params=_cparams(("arbitrary",)),
        name="out_router",
    )(x2, mix, wo, g, wrh, wrl, br)


def _route_kernel(te_ref, dest_ref, blke_ref, padlo_ref, padhi_ref, run_ref, pstart_ref):
    ph = pl.program_id(0)
    i = pl.program_id(1)
    E = run_ref.shape[0]
    tm = te_ref.shape[1]
    nbp = blke_ref.shape[1]
    te = te_ref[...]
    e_idx = lax.broadcasted_iota(I32, (E, tm), 0)
    ohs = [te[k:k + 1, :] == e_idx for k in range(TOP_K)]
    oh = jnp.where(ohs[0] | ohs[1] | ohs[2] | ohs[3], 1.0, 0.0)
    tile_cnt = jnp.sum(oh, axis=1, keepdims=True)

    @pl.when((ph == 0) & (i == 0))
    def _():
        run_ref[...] = jnp.zeros(run_ref.shape, F32)

    @pl.when(ph == 0)
    def _():
        run_ref[...] = run_ref[...] + tile_cnt
        dest_ref[...] = jnp.zeros(dest_ref.shape, I32)

    @pl.when((ph == 1) & (i == 0))
    def _():
        counts = run_ref[...]
        padded = jnp.floor((counts + (ROW_BLOCK - 1)) * (1.0 / ROW_BLOCK)) * ROW_BLOCK
        row = lax.broadcasted_iota(I32, counts.shape, 0)
        incl = padded
        sh = 1
        while sh < E:
            incl = incl + jnp.where(row >= sh, pltpu.roll(incl, sh, 0), 0.0)
            sh *= 2
        pstart_ref[...] = incl - padded
        padlo_ref[...] = (incl - padded + counts).astype(I32)
        padhi_ref[...] = incl.astype(I32)
        blk_start = (lax.broadcasted_iota(I32, (E, nbp), 1) * ROW_BLOCK).astype(F32)
        blke_ref[...] = jnp.sum(jnp.where(incl[:, 0:1] <= blk_start, 1, 0), axis=0, keepdims=True).astype(I32)
        run_ref[...] = jnp.zeros(run_ref.shape, F32)

    @pl.when(ph == 1)
    def _():
        upper = jnp.where(lax.broadcasted_iota(I32, (tm, tm), 0) < lax.broadcasted_iota(I32, (tm, tm), 1), 1.0, 0.0)
        before = _dot(oh.astype(BF16), upper.astype(BF16))
        val = before + pstart_ref[:, 0:1] + run_ref[:, 0:1]
        rows = [jnp.sum(jnp.where(ohs[k], val, 0.0), axis=0, keepdims=True) for k in range(TOP_K)]
        dest_ref[...] = jnp.concatenate(rows + [jnp.zeros((8 - TOP_K, tm), F32)], axis=0).astype(I32)
        run_ref[...] = run_ref[...] + tile_cnt


def _route(top_e, n_experts, n_blk_pad, tm):
    T = top_e.shape[1]
    return pl.pallas_call(
        _route_kernel,
        grid=(2, T // tm),
        in_specs=[pl.BlockSpec((8, tm), lambda ph, i: (0, i))],
        out_specs=[pl.BlockSpec((8, tm), lambda ph, i: (0, i * ph)),
                   pl.BlockSpec((1, n_blk_pad), lambda ph, i: (0, 0)),
                   pl.BlockSpec((n_experts, HEAD_DIM), lambda ph, i: (0, 0)),
                   pl.BlockSpec((n_experts, HEAD_DIM), lambda ph, i: (0, 0))],
        out_shape=[jax.ShapeDtypeStruct((8, T), I32), jax.ShapeDtypeStruct((1, n_blk_pad), I32),
                   jax.ShapeDtypeStruct((n_experts, HEAD_DIM), I32), jax.ShapeDtypeStruct((n_experts, HEAD_DIM), I32)],
        scratch_shapes=[pltpu.VMEM((n_experts, HEAD_DIM), F32), pltpu.VMEM((n_experts, HEAD_DIM), F32)],
        compiler_params=_cparams(("arbitrary", "arbitrary")),
        name="route",
    )(top_e)


def _row_copy(src_ref, src_row, dst_ref, dst_row, sem):
    return pltpu.make_async_copy(src_ref.at[pl.ds(src_row, 1), :], dst_ref.at[pl.ds(dst_row, 1), :], sem)


def _wait_rows(rows_hbm_ref, n_rows, sem):
    assert n_rows % WAIT_ROWS == 0
    chunk = rows_hbm_ref.at[pl.ds(0, WAIT_ROWS), :]
    for _ in range(n_rows // WAIT_ROWS):
        pltpu.make_async_copy(chunk, chunk, sem).wait()


def _dispatch_kernel(pad_ref, dest_ref, h_ref, xpad_ref, zero_ref, sem, zsem):
    tm = h_ref.shape[0]

    @pl.when(pl.program_id(0) == 0)
    def _():
        zero_ref[...] = jnp.zeros(zero_ref.shape, zero_ref.dtype)

        def per_expert(e, _):
            lo, hi = pad_ref[0, e], pad_ref[1, e]

            def issue(r, _):
                _row_copy(zero_ref, 0, xpad_ref, r, zsem).start()
                return 0
            lax.fori_loop(lo, hi, issue, 0)

            def drain(r, _):
                _row_copy(zero_ref, 0, xpad_ref, 0, zsem).wait()
                return 0
            lax.fori_loop(lo, hi, drain, 0)
            return 0
        lax.fori_loop(0, pad_ref.shape[1], per_expert, 0)

    def issue(t, _):
        for k in range(TOP_K):
            _row_copy(h_ref, t, xpad_ref, dest_ref[t * TOP_K + k], sem).start(priority=k % 2)
        return 0
    lax.fori_loop(0, tm, issue, 0, unroll=2)

    _wait_rows(xpad_ref, TOP_K * tm, sem)


def _dispatch(pads, dest, h2, n_rows, tm):
    T, D = h2.shape
    grid_spec = pltpu.PrefetchScalarGridSpec(
        num_scalar_prefetch=1,
        grid=(T // tm,),
        in_specs=[pl.BlockSpec((tm * TOP_K,), lambda i, pads: (i,), memory_space=pltpu.SMEM),
                  pl.BlockSpec((tm, D), lambda i, pads: (i, 0))],
        out_specs=pl.BlockSpec(memory_space=pl.ANY),
        scratch_shapes=[pltpu.VMEM((8, D), h2.dtype), pltpu.SemaphoreType.DMA(()), pltpu.SemaphoreType.DMA(())])
    return pl.pallas_call(
        _dispatch_kernel,
        grid_spec=grid_spec,
        out_shape=jax.ShapeDtypeStruct((n_rows, D), h2.dtype),
        compiler_params=_cparams(("arbitrary",)),
        name="dispatch",
    )(pads, dest, h2)


def _next_expert(blk_e, n_experts):
    nxt = jnp.min(jnp.where(blk_e[None, :] > blk_e[:, None], blk_e[None, :], n_experts), axis=1)
    return jnp.where(nxt < n_experts, nxt, -1).astype(I32)


def _stream_expert_weights(be_ref, nxt_ref, w_hbm, col_offsets, wbuf_ref, wb_refs, sem, cnt_ref, n_experts):
    n, m = pl.program_id(0), pl.program_id(1)
    tn = wb_refs[0].shape[1]
    e = be_ref[m]
    live = e < n_experts
    first = live & ((m == 0) | (be_ref[jnp.maximum(m - 1, 0)] != e))

    def copies(ex, col_tile, slot):
        c0 = pl.multiple_of(col_tile * tn, tn)
        return [pltpu.make_async_copy(w_hbm.at[ex, :, pl.ds(off + c0, tn)], wbuf_ref.at[slot, g], sem.at[slot, g])
                for g, off in enumerate(col_offsets)]

    @pl.when((n == 0) & (m == 0))
    def _():
        cnt_ref[0] = 0
        for c in copies(e, 0, 0):
            c.start()

    @pl.when(first)
    def _():
        slot = cnt_ref[0] % 2
        for c in copies(e, n, slot):
            c.wait()
        nxt = nxt_ref[m]

        @pl.when(nxt >= 0)
        def _():
            for c in copies(nxt, n, 1 - slot):
                c.start()

        @pl.when((nxt < 0) & (n + 1 < pl.num_programs(0)))
        def _():
            for c in copies(be_ref[0], n + 1, 1 - slot):
                c.start()

        for g, wb_ref in enumerate(wb_refs):
            wb_ref[...] = wbuf_ref[slot, g].astype(BF16)
        cnt_ref[0] = cnt_ref[0] + 1
    return live


def _expert_up_kernel(be_ref, nxt_ref, x_ref, w_hbm, bg_ref, bl_ref, act_ref, wbuf_ref, wgb_ref, wlb_ref, sem, cnt_ref,
                      *, n_experts, d_ff):
    live = _stream_expert_weights(be_ref, nxt_ref, w_hbm, (0, d_ff), wbuf_ref, (wgb_ref, wlb_ref), sem, cnt_ref,
                                  n_experts)

    @pl.when(live)
    def _():
        x_lo, x_hi = _unpack_bf16_pairs(x_ref[...])
        x = jnp.concatenate([x_lo.astype(BF16), x_hi.astype(BF16)], axis=1)
        gate =jnp.minimum(_dot(x, wgb_ref[...]) + bg_ref[...], SWIGLU_LIMIT)
        lin = jnp.clip(_dot(x, wlb_ref[...]) + bl_ref[...], -SWIGLU_LIMIT, SWIGLU_LIMIT)
        act_ref[...] = (gate * _sigmoid(SWIGLU_ALPHA * gate) * (lin + 1.0)).astype(BF16)

    @pl.when(jnp.logical_not(live))
    def _():
        act_ref[...] = jnp.zeros(act_ref.shape, BF16)


def _expert_up(blk_e, nxt_e, x_pad, w_gu, b_gu, n_blk, tn):
    P = x_pad.shape[0]
    E, D = w_gu.shape[0], w_gu.shape[1]
    F = w_gu.shape[2] // 2
    nf = F // tn
    ex = lambda be, m: jnp.minimum(be[m], E - 1)
    grid_spec = pltpu.PrefetchScalarGridSpec(
        num_scalar_prefetch=2,
        grid=(nf, n_blk),
        in_specs=[pl.BlockSpec((ROW_BLOCK, D // 2), lambda n, m, be, nx: (m, 0)),
                  pl.BlockSpec(memory_space=pl.ANY),
                  pl.BlockSpec((None, 1, tn), lambda n, m, be, nx: (ex(be, m), 0, n)),
                  pl.BlockSpec((None, 1, tn), lambda n, m, be, nx: (ex(be, m), 0, nf + n))],
        out_specs=pl.BlockSpec((ROW_BLOCK, tn), lambda n, m, be, nx: (m, n)),
        scratch_shapes=[pltpu.VMEM((2, 2, D, tn), F32), pltpu.VMEM((D, tn), BF16), pltpu.VMEM((D, tn), BF16),
                        pltpu.SemaphoreType.DMA((2, 2)), pltpu.SMEM((1,), I32)])
    return pl.pallas_call(
        functools.partial(_expert_up_kernel, n_experts=E, d_ff=F),
        grid_spec=grid_spec,
        out_shape=jax.ShapeDtypeStruct((P, F), BF16),
        compiler_params=_cparams(("arbitrary", "arbitrary")),
        name="expert_up",
    )(blk_e, nxt_e, x_pad, w_gu, b_gu, b_gu)


def _expert_down_kernel(be_ref, nxt_ref, a_ref, w_hbm, b_ref, y_ref, wbuf_ref, wb_ref, sem, cnt_ref, *, n_experts):
    live = _stream_expert_weights(be_ref, nxt_ref, w_hbm, (0,), wbuf_ref, (wb_ref,), sem, cnt_ref, n_experts)

    @pl.when(live)
    def _():
        y_ref[...] = _pack_bf16_pairs(_dot(a_ref[...], wb_ref[...]) + b_ref[...])

    @pl.when(jnp.logical_not(live))
    def _():
        y_ref[...] = jnp.zeros(y_ref.shape, y_ref.dtype)


def _expert_down(blk_e, nxt_e, act, w_d, b_d, n_blk):
    P, F = act.shape
    E, _, D = w_d.shape
    tn = D
    ex = lambda be, m: jnp.minimum(be[m], E - 1)
    grid_spec = pltpu.PrefetchScalarGridSpec(
        num_scalar_prefetch=2,
        grid=(1, n_blk),
        in_specs=[pl.BlockSpec((ROW_BLOCK, F), lambda n, m, be, nx: (m, 0)),
                  pl.BlockSpec(memory_space=pl.ANY),
                  pl.BlockSpec((None, 1, tn), lambda n, m, be, nx: (ex(be, m), 0, n))],
        out_specs=pl.BlockSpec((ROW_BLOCK, tn // 2), lambda n, m, be, nx: (m, n)),
        scratch_shapes=[pltpu.VMEM((2, 1, F, tn), F32), pltpu.VMEM((F, tn), BF16),
                        pltpu.SemaphoreType.DMA((2, 1)), pltpu.SMEM((1,), I32)])
    return pl.pallas_call(
        functools.partial(_expert_down_kernel, n_experts=E),
        grid_spec=grid_spec,
        out_shape=jax.ShapeDtypeStruct((P, D // 2), jnp.uint32),
        compiler_params=_cparams(("arbitrary", "arbitrary")),
        name="expert_down",
    )(blk_e, nxt_e, act, w_d, b_d)


def _combine_kernel(dest_ref, dest_next_ref, x1_ref, tw_ref, g_ref, ypad_ref, o_ref, ybuf_ref, sem):
    tm = x1_ref.shape[0]
    i = pl.program_id(0)
    slot = i % 2

    def gather(d_ref, s):
        def issue(t, _):
            for k in range(TOP_K):
                _row_copy(ypad_ref, d_ref[t * TOP_K + k], ybuf_ref.at[s, k], t, sem.at[s]).start(priority=k % 2)
            return 0
        lax.fori_loop(0, tm, issue, 0, unroll=2)

    @pl.when(i == 0)
    def _():
        gather(dest_ref, 0)

    @pl.when(i + 1 < pl.num_programs(0))
    def _():
        gather(dest_next_ref, 1 - slot)

    _wait_rows(ypad_ref, TOP_K * tm, sem.at[slot])

    tw = tw_ref[...]
    half = ybuf_ref.shape[-1]
    acc_lo = x1_ref[:, :half]
    acc_hi = x1_ref[:, half:]
    for k in range(TOP_K):
        y_lo, y_hi = _unpack_bf16_pairs(ybuf_ref[slot, k])
        acc_lo = acc_lo + tw[:, k:k + 1] * y_lo
        acc_hi = acc_hi + tw[:, k:k + 1] * y_hi
    acc = jnp.concatenate([acc_lo, acc_hi], axis=1)
    o_ref[...] = acc * lax.rsqrt(jnp.mean(acc * acc, axis=-1, keepdims=True) + NORM_EPS) * g_ref[...]


def _combine(dest, x1, top_w, g, y_pad, tm):
    T, D = x1.shape
    n = T // tm
    return pl.pallas_call(
        _combine_kernel,
        grid=(n,),
        in_specs=[pl.BlockSpec((tm * TOP_K,), lambda i: (i,), memory_space=pltpu.SMEM),
                  pl.BlockSpec((tm * TOP_K,), lambda i: (jnp.minimum(i + 1, n - 1),), memory_space=pltpu.SMEM),
                  pl.BlockSpec((tm, D), lambda i: (i, 0)),
                  pl.BlockSpec((tm, HEAD_DIM), lambda i: (i, 0)),
                  pl.BlockSpec((1, D), lambda i: (0, 0)),
                  pl.BlockSpec(memory_space=pl.ANY)],
        out_specs=pl.BlockSpec((tm, D), lambda i: (i, 0)),
        out_shape=jax.ShapeDtypeStruct((T, D), F32),
        scratch_shapes=[pltpu.VMEM((2, TOP_K, tm) + y_pad.shape[1:], y_pad.dtype), pltpu.SemaphoreType.DMA((2,))],
        compiler_params=_cparams(("arbitrary",)),
        name="combine",
    )(dest, dest, x1, top_w, g, y_pad)


def _rope_tables(pos):
    inv_freq = ROPE_THETA ** (-jnp.arange(0, ROPE_DIM, 2, dtype=F32) / ROPE_DIM)
    ang = pos.astype(F32)[..., None] * inv_freq
    cos, sin = jnp.cos(ang), jnp.sin(ang)
    rest = HEAD_DIM - ROPE_DIM
    cos_t = jnp.concatenate([cos, cos, jnp.ones(cos.shape[:-1] + (rest,), F32)], axis=-1)
    sin_t = jnp.concatenate([-sin, sin, jnp.zeros(sin.shape[:-1] + (rest,), F32)], axis=-1)
    return cos_t, sin_t


def _cmp_to_slc_t(seq, nc_pad):
    n_cmp = (seq - CMP_BLOCK) // CMP_STRIDE + 1
    n_slc = seq // SEL_BLOCK
    cmp_start = np.arange(n_cmp) * CMP_STRIDE
    slc_start = np.arange(n_slc) * SEL_BLOCK
    overlap = np.clip(np.minimum(cmp_start[:, None] + CMP_BLOCK, slc_start[None, :] + SEL_BLOCK)
                      - np.maximum(cmp_start[:, None], slc_start[None, :]), 0, None)
    out = np.zeros((n_slc, nc_pad), np.float32)
    out[:, :n_cmp] = (overlap / CMP_BLOCK).T
    return jnp.asarray(out, BF16)


def _regroup_w_in(w):
    D = w.shape[0]
    o_kv = NSA_WIDTH
    o_gn = o_kv + 6 * KV_WIDTH
    o_b = o_gn + 3 * NSA_HEADS
    o_gm = o_b + 3 * MOBA_WIDTH
    kv = lambda s: w[:, o_kv + s * KV_WIDTH:o_kv + (s + 1) * KV_WIDTH]
    mb = lambda s: w[:, o_b + s * MOBA_WIDTH:o_b + (s + 1) * MOBA_WIDTH]
    cols = [w[:, :NSA_WIDTH], kv(2), kv(4), mb(0), mb(1), kv(0), kv(1), kv(3), kv(5), mb(2), w[:, o_gm:o_gm + 2 * D]]
    w_main = jnp.concatenate(cols, axis=1).astype(BF16)
    wg = w[:, o_gn:o_gn + 3 * NSA_HEADS]
    wg = wg.reshape(D, 3, NSA_GROUPS, NSA_REP).transpose(2, 1, 3, 0).reshape(NSA_GROUPS, 3 * NSA_REP, D)
    wg = jnp.pad(wg, ((0, 0), (0, GATE_ROWS - 3 * NSA_REP), (0, 0))).reshape(NSA_GROUPS * GATE_ROWS, D)
    return w_main, wg.astype(BF16)


def kernel(x, positions, g_attn_norm, w_in, pe_cmp_k, w_cmp_k1, w_cmp_k2, pe_cmp_v, w_cmp_v1, w_cmp_v2, w_proj_nsa, w_proj_moba, w_out, g_ffn_norm, w_router, b_router, w_gate_up, b_gate_up, w_down, b_down, g_final_norm):
    B, S, D = x.shape
    T = B * S
    E = w_router.shape[-1]
    F = w_down.shape[-2]
    depth = w_in.shape[0]
    assert depth == 1, "the final norm is fused into the single layer's combine step"
    assert S % MOBA_BLOCK == 0 and S >= WINDOW + NSA_TQ and (2 * D) % COL_TILE == 0 and T % 512 == 0
    nc = S // CMP_STRIDE
    tm_big = min(1024, T)

    cos_t, sin_t = _rope_tables(positions)
    cmp_end = np.arange(nc) * CMP_STRIDE + CMP_BLOCK - 1
    cmp_end = np.minimum(cmp_end, S - 1)
    cos_c, sin_c = cos_t[:, cmp_end], sin_t[:, cmp_end]
    c2st = _cmp_to_slc_t(S, nc)

    x2 = x.reshape(T, D)
    for layer in range(depth):
        w_main, w_gates_t = _regroup_w_in(w_in[layer])
        h, gates_t = _norm_gates(x2, g_attn_norm[layer].reshape(1, D), w_gates_t, tm_big)
        proj = _in_proj(h, w_main, cos_t.reshape(T, HEAD_DIM), sin_t.reshape(T, HEAD_DIM), min(2048, T))
        proj3 = proj.reshape(B, S, proj.shape[1])

        xc = proj3[:, :, OFF_KCMP:OFF_KCMP + 2 * KV_WIDTH].reshape(B, nc, CMP_STRIDE * 2 * KV_WIDTH)
        kc, vc = _compress(xc, pe_cmp_k[layer], w_cmp_k1[layer].astype(BF16), w_cmp_k2[layer].astype(BF16),
                           pe_cmp_v[layer], w_cmp_v1[layer].astype(BF16), w_cmp_v2[layer].astype(BF16), cos_c, sin_c)
        o_a = _nsa_attention(proj3, kc, vc, gates_t, c2st)
        o_b = _moba_attention(proj3)

        mix = _merge_proj(o_a.reshape(T, NSA_WIDTH), o_b.reshape(T, MOBA_WIDTH),
                          w_proj_nsa[layer].astype(BF16), w_proj_moba[layer].astype(BF16), proj, tm_big)
        wr_t = w_router[layer].T
        wr_hi = wr_t.astype(BF16)
        wr_lo = (wr_t - wr_hi.astype(F32)).astype(BF16)
        x1, h2, top_e, top_w = _out_router(x2, mix, w_out[layer].astype(BF16), g_ffn_norm[layer].reshape(1, D),
                                           wr_hi, wr_lo, b_router[layer].reshape(E, 1), 512)

        n_blk = (T * TOP_K) // ROW_BLOCK + E
        n_blk_pad = -(-n_blk // HEAD_DIM) * HEAD_DIM
        dest, blk_e, pad_lo, pad_hi = _route(top_e, E, n_blk_pad, 512)
        pads = jnp.stack([pad_lo[:, 0], pad_hi[:, 0]])
        dest = dest[:TOP_K].T.reshape(T * TOP_K)
        x_pad = _dispatch(pads, dest, h2, n_blk * ROW_BLOCK, 512)
        blk_e = blk_e.reshape(n_blk_pad)
        nxt_e = _next_expert(blk_e, E)
        act = _expert_up(blk_e, nxt_e, x_pad, w_gate_up[layer], b_gate_up[layer].reshape(E, 1, 2 * F), n_blk,
                         min(1024, F))
        y_pad = _expert_down(blk_e, nxt_e, act, w_down[layer], b_down[layer].reshape(E, 1, D), n_blk)
        x2 = _combine(dest, x1, top_w, g_final_norm.reshape(1, D), y_pad, 256)
    return x2.reshape(B, S, D)
```

```python
import functools

import jax
import jax.numpy as jnp
import numpy as np
from jax import lax
from jax.experimental import pallas as pl
from jax.experimental.pallas import tpu as pltpu

F32 = jnp.float32
BF16 = jnp.bfloat16
I32 = jnp.int32

HEAD_DIM = 128
ROPE_DIM = HEAD_DIM // 4
ROPE_HALF = ROPE_DIM // 2
ROPE_THETA = 500000.0
NORM_EPS = 1e-5
NEG_INF = -1e30
REMOVED = -3e38
SEL_FORCED = 1e9

NSA_HEADS = 8
NSA_GROUPS = 2
NSA_REP = NSA_HEADS // NSA_GROUPS
CMP_BLOCK = 32
CMP_STRIDE = 16
CMP_HIDDEN = 256
SEL_BLOCK = 64
SEL_TOPN = 16
WINDOW = 512
MOBA_HEADS = 8
MOBA_BLOCK = 256
MOBA_TOPK = 3
TOP_K = 4
SWIGLU_LIMIT = 7.0
SWIGLU_ALPHA = 1.702
ROW_BLOCK = 256

NSA_WIDTH = NSA_HEADS * HEAD_DIM
MOBA_WIDTH = MOBA_HEADS * HEAD_DIM
KV_WIDTH = NSA_GROUPS * HEAD_DIM
SCALE = HEAD_DIM ** -0.5
Q_SCALE = SCALE * 1.4426950408889634

COL_TILE = 512
OFF_QA = 0
OFF_KSLC = NSA_WIDTH
OFF_KWIN = OFF_KSLC + KV_WIDTH
OFF_QB = OFF_KWIN + KV_WIDTH
OFF_KB = OFF_QB + MOBA_WIDTH
ROPE_COLS = OFF_KB + MOBA_WIDTH
OFF_KCMP = ROPE_COLS
OFF_VCMP = OFF_KCMP + KV_WIDTH
OFF_VSLC = OFF_VCMP + KV_WIDTH
OFF_VWIN = OFF_VSLC + KV_WIDTH
OFF_VB = OFF_VWIN + KV_WIDTH
PLAIN_END = OFF_VB + MOBA_WIDTH
N_ROPE_TILES = ROPE_COLS // COL_TILE
N_PLAIN_TILES = (PLAIN_END - ROPE_COLS) // COL_TILE
Q_TILES = tuple(range(OFF_QA // COL_TILE, OFF_KSLC // COL_TILE)) + tuple(range(OFF_QB // COL_TILE, OFF_KB // COL_TILE))

VMEM_LIMIT = 56 * 1024 * 1024
NSA_TQ = 128
SEL_KT = 512
GATE_ROWS = 16
IN_PROJ_CHUNK = 256
ROW_CHUNK = 256
NSA_SUB = 4
MOBA_HEADS_PER_STEP = 4
MOBA_BLOCKS_PER_ITER = 4
SLAB = 8
WAIT_TOKENS = 256


def _cparams(sem):
    return pltpu.CompilerParams(dimension_semantics=sem, vmem_limit_bytes=VMEM_LIMIT)


def _sigmoid(z):
    return 1.0 / (1.0 + jnp.exp(-z))


def _dot(a, b):
    return jnp.dot(a, b, preferred_element_type=F32)


def _dot_nt(a, b):
    return lax.dot_general(a, b, (((1,), (1,)), ((), ())), preferred_element_type=F32)


def _split_hi_lo(v):
    hi = v.astype(BF16)
    lo = (v - hi.astype(F32)).astype(BF16)
    return hi, lo


def _pack_bf16_pairs(v):
    half = v.shape[1] // 2
    lo = pltpu.bitcast(v[:, :half].astype(BF16).astype(F32), jnp.uint32)
    hi = pltpu.bitcast(v[:, half:].astype(BF16).astype(F32), jnp.uint32)
    return lax.shift_right_logical(lo, jnp.uint32(16)) | hi


def _unpack_bf16_pairs(w):
    lo = pltpu.bitcast(lax.shift_left(w, jnp.uint32(16)), F32)
    hi = pltpu.bitcast(w & jnp.uint32(0xFFFF0000), F32)
    return lo, hi


def _slab_rows(width_words):
    assert width_words == SLAB * HEAD_DIM, "a token's packed words must fill exactly one (8, 128) tile"
    return SLAB


def _store_token_slabs(ref, words):
    m, width = words.shape
    rows = _slab_rows(width)
    for s in range(rows):
        ref[pl.ds(s, m, stride=rows), :] = words[:, s * HEAD_DIM:(s + 1) * HEAD_DIM]


def _load_token_slabs(ref, m, rows):
    return jnp.concatenate([ref[pl.ds(s, m, stride=rows), :] for s in range(rows)], axis=1)


def _topk_rows(work, row_idx, n_rows, k):
    picks = []
    for _ in range(k):
        m = jnp.max(work, axis=0, keepdims=True)
        first = jnp.min(jnp.where(work == m, row_idx, n_rows), axis=0, keepdims=True)
        pick = row_idx == first
        picks.append((m, first, pick))
        work = jnp.where(pick, REMOVED, work)
    return picks


def _norm_gates_kernel(x_ref, g_ref, wgt_ref, h_ref, gt_ref):
    x = x_ref[...]
    h = x * lax.rsqrt(jnp.mean(x * x, axis=-1, keepdims=True) + NORM_EPS) * g_ref[...]
    hb = h.astype(BF16)
    h_ref[...] = hb
    gt_ref[...] = _sigmoid(_dot_nt(wgt_ref[...], hb))


def _norm_gates(x2, g, wgt, tm):
    T, D = x2.shape
    R = wgt.shape[0]
    return pl.pallas_call(
        _norm_gates_kernel,
        grid=(T // tm,),
        in_specs=[pl.BlockSpec((tm, D), lambda i: (i, 0)),
                  pl.BlockSpec((1, D), lambda i: (0, 0)),
                  pl.BlockSpec((R, D), lambda i: (0, 0))],
        out_specs=[pl.BlockSpec((tm, D), lambda i: (i, 0)),
                   pl.BlockSpec((R, tm), lambda i: (0, i))],
        out_shape=[jax.ShapeDtypeStruct((T, D), BF16), jax.ShapeDtypeStruct((R, T), F32)],
        compiler_params=_cparams(("arbitrary",)),
        name="norm_gates",
    )(x2, g, wgt)


def _rope_tile(xh, c, s, lane):
    rot = jnp.where(lane < ROPE_HALF, pltpu.roll(xh, HEAD_DIM - ROPE_HALF, 1), pltpu.roll(xh, ROPE_HALF, 1))
    return xh * c + rot * s


def _in_proj_kernel(h_ref, w_ref, c_ref, s_ref, o_ref):
    j = pl.program_id(1)
    tm = h_ref.shape[0]
    chunk = min(IN_PROJ_CHUNK, tm)

    def by_chunks(epilogue):
        for r0 in range(0, tm, chunk):
            rows = slice(r0, r0 + chunk)
            epilogue(rows, _dot(h_ref[rows, :], w_ref[...]))

    @pl.when(j < N_ROPE_TILES)
    def _():
        is_q = j == Q_TILES[0]
        for qt in Q_TILES[1:]:
            is_q = is_q | (j == qt)
        f = jnp.where(is_q, Q_SCALE, 1.0).astype(F32)
        lane = lax.broadcasted_iota(I32, (chunk, HEAD_DIM), 1)

        def rope(rows, acc):
            c = c_ref[rows, :] * f
            s = s_ref[rows, :] * f
            for hh in range(COL_TILE // HEAD_DIM):
                sl = slice(hh * HEAD_DIM, (hh + 1) * HEAD_DIM)
                o_ref[rows, sl] = _rope_tile(acc[:, sl], c, s, lane).astype(BF16)
        by_chunks(rope)

    @pl.when((j >= N_ROPE_TILES) & (j < N_ROPE_TILES + N_PLAIN_TILES))
    def _():
        def plain(rows, acc):
            o_ref[rows, :] = acc.astype(BF16)
        by_chunks(plain)

    @pl.when(j >= N_ROPE_TILES + N_PLAIN_TILES)
    def _():
        def gate(rows, acc):
            o_ref[rows, :] = _sigmoid(acc).astype(BF16)
        by_chunks(gate)


def _in_proj(h, w, cos_t, sin_t, tm):
    T, D = h.shape
    N = w.shape[1]
    return pl.pallas_call(
        _in_proj_kernel,
        grid=(T // tm, N // COL_TILE),
        in_specs=[pl.BlockSpec((tm, D), lambda i, j: (i, 0)),
                  pl.BlockSpec((D, COL_TILE), lambda i, j: (0, j)),
                  pl.BlockSpec((tm, HEAD_DIM), lambda i, j: (i, 0)),
                  pl.BlockSpec((tm, HEAD_DIM), lambda i, j: (i, 0))],
        out_specs=pl.BlockSpec((tm, COL_TILE), lambda i, j: (i, j)),
        out_shape=jax.ShapeDtypeStruct((T, N), BF16),
        compiler_params=_cparams(("arbitrary", "arbitrary")),
        name="in_proj",
    )(h, w, cos_t, sin_t)


def _gelu_tanh(x):
    return 0.5 * x * (1.0 + jnp.tanh(0.7978845608028654 * (x + 0.044715 * x * x * x)))


def _compress_kernel(x_ref, pek_ref, w1k_ref, w2k_ref, pev_ref, w1v_ref, w2v_ref, c_ref, s_ref, kc_ref, vc_ref):
    nc = x_ref.shape[0]
    half = CMP_STRIDE * HEAD_DIM
    tok_w = 2 * KV_WIDTH
    for which, (pe_ref, w1_ref, w2_ref, out_ref) in enumerate(
            ((pek_ref, w1k_ref, w2k_ref, kc_ref), (pev_ref, w1v_ref, w2v_ref, vc_ref))):
        for g in range(NSA_GROUPS):
            acc_a = jnp.zeros((nc, CMP_HIDDEN), F32)
            acc_b = jnp.zeros((nc, CMP_HIDDEN), F32)
            for l in range(CMP_STRIDE):
                off = l * tok_w + which * KV_WIDTH + g * HEAD_DIM
                xl = x_ref[:, off:off + HEAD_DIM].astype(F32)
                xa = (xl + pe_ref[l:l + 1, :]).astype(BF16)
                xb = (xl + pe_ref[CMP_STRIDE + l:CMP_STRIDE + l + 1, :]).astype(BF16)
                acc_a = acc_a + _dot(xa, w1_ref[l * HEAD_DIM:(l + 1) * HEAD_DIM, :])
                acc_b = acc_b + _dot(xb, w1_ref[half + l * HEAD_DIM:half + (l + 1) * HEAD_DIM, :])
            hid = _gelu_tanh(acc_a + pltpu.roll(acc_b, nc - 1, 0))
            out = _dot(hid.astype(BF16), w2_ref[...])
            if which == 0:
                lane = lax.broadcasted_iota(I32, out.shape, 1)
                out = _rope_tile(out, c_ref[...], s_ref[...], lane)
            out_ref[g] = out.astype(BF16)


def _compress(xc, pek, w1k, w2k, pev, w1v, w2v, cos_c, sin_c):
    B, nc, W = xc.shape
    full = lambda a: pl.BlockSpec(a.shape, lambda b: (0,) * a.ndim)
    out_sds = jax.ShapeDtypeStruct((B, NSA_GROUPS, nc, HEAD_DIM), BF16)
    out_spec = pl.BlockSpec((None, NSA_GROUPS, nc, HEAD_DIM), lambda b: (b, 0, 0, 0))
    return pl.pallas_call(
        _compress_kernel,
        grid=(B,),
        in_specs=[pl.BlockSpec((None, nc, W), lambda b: (b, 0, 0)),
                  full(pek), full(w1k), full(w2k), full(pev), full(w1v), full(w2v),
                  pl.BlockSpec((None, nc, HEAD_DIM), lambda b: (b, 0, 0)),
                  pl.BlockSpec((None, nc, HEAD_DIM), lambda b: (b, 0, 0))],
        out_specs=[out_spec, out_spec],
        out_shape=[out_sds, out_sds],
        compiler_params=_cparams(("arbitrary",)),
        name="compress",
    )(xc, pek, w1k, w2k, pev, w1v, w2v, cos_c, sin_c)


def _transpose_into(src_ref, col0, dst_ref, n_rows):
    def body(b, _):
        r0 = pl.multiple_of(b * HEAD_DIM, HEAD_DIM)
        blk = src_ref[pl.ds(r0, HEAD_DIM), col0:col0 + HEAD_DIM].astype(F32)
        dst_ref[:, pl.ds(r0, HEAD_DIM)] = blk.T.astype(BF16)
        return 0
    lax.fori_loop(0, n_rows // HEAD_DIM, body, 0)


def _topk_mask(score, k):
    n, L = score.shape
    sub = 8
    assert n % sub == 0
    groups = n // sub
    segs = [score[sub * g:sub * (g + 1), :] for g in range(groups)]
    sub_idx = lax.broadcasted_iota(I32, (sub, L), 0)
    later = [jnp.where(sub_idx > r, 1.0, 0.0) for r in range(sub)]
    ranks = [jnp.zeros((sub, L), F32) for _ in range(groups)]
    for i in range(n):
        gi, ri = divmod(i, sub)
        row = score[i:i + 1, :]
        for g in range(groups):
            if g > gi:
                inc = jnp.where(row >= segs[g], 1.0, 0.0)
            elif g < gi:
                inc = jnp.where(row > segs[g], 1.0, 0.0)
            else:
                inc = jnp.where(row > segs[g], 1.0, jnp.where(row == segs[g], later[ri], 0.0))
            ranks[g] = ranks[g] + inc
    return jnp.concatenate(ranks, axis=0) < k


def _bias_rows_to_cols(sel, lanes_out=HEAD_DIM):
    n, L = sel.shape
    bias_t = jnp.where(sel, 0.0, NEG_INF)
    if n < lanes_out:
        bias_t = jnp.concatenate([bias_t, jnp.zeros((lanes_out - n, L), F32)], axis=0)
    return bias_t.T.astype(BF16)


def _softmax_cols(s, live=None):
    m = jnp.max(s, axis=0, keepdims=True)
    p = jnp.exp2(s - m)
    inv = 1.0 / jnp.sum(p, axis=0, keepdims=True)
    if live is not None:
        inv = jnp.where(live, inv, 0.0)
    return p * inv


def _online_update(s, vt, carry):
    m, l, acc = carry
    m_new = jnp.maximum(m, jnp.max(s, axis=0, keepdims=True))
    alpha = jnp.exp2(m - m_new)
    p = jnp.exp2(s - m_new)
    l = alpha * l + jnp.sum(p, axis=0, keepdims=True)
    acc = alpha * acc + _dot(vt, p.astype(BF16))
    return m_new, l, acc


def _nsa_kernel(q_ref, kc_ref, vc_ref, ks_ref, vs_ref, kw_ref, vw_ref, gt_ref, c2st_ref, e_ref, o_ref,
                ksa_ref, vst_ref, vwt_ref, vct_ref, *, seq):
    tq = NSA_TQ
    U = NSA_SUB
    R = NSA_REP
    i = pl.program_id(2)
    base = i * (U * tq)
    nc = kc_ref.shape[0]
    n_slc = c2st_ref.shape[0]
    lanes = R * tq
    band = WINDOW + tq
    tile_r = lambda a: jnp.concatenate([a] * R, axis=1)
    rowi = lambda n: lax.broadcasted_iota(I32, (n, tq), 0)

    @pl.when(i == 0)
    def _():
        _transpose_into(vs_ref, 0, vst_ref, seq)
        _transpose_into(vw_ref, 0, vwt_ref, seq)
        _transpose_into(vc_ref, 0, vct_ref, nc)
        ksa_ref[:, 0:HEAD_DIM] = ks_ref[...]
        ksa_ref[:, HEAD_DIM:2 * HEAD_DIM] = e_ref[...]

    subs = range(U)
    heads = [[q_ref[u * tq:(u + 1) * tq, r * HEAD_DIM:(r + 1) * HEAD_DIM] for r in range(R)] for u in subs]
    q4 = [jnp.concatenate(heads[u], axis=0) for u in subs]
    tcols = [lambda n, q0=base + u * tq: q0 + lax.broadcasted_iota(I32, (n, tq), 1) for u in subs]
    starts = [pl.multiple_of(jnp.maximum(base + u * tq - WINDOW, 0), HEAD_DIM) for u in subs]

    s_c = [_dot_nt(kc_ref[...], q4[u]) for u in subs]
    s_w = [_dot_nt(kw_ref[pl.ds(starts[u], band), :], q4[u]) for u in subs]

    p_c = []
    for u in subs:
        bias_c = jnp.where(rowi(nc) * CMP_STRIDE + (CMP_BLOCK - 1) <= tcols[u](nc), 0.0, NEG_INF)
        live_c = tcols[u](1) >= CMP_BLOCK - 1
        p_c.append(_softmax_cols(s_c[u] + tile_r(bias_c), tile_r(live_c)))
    o_cmp = [_dot(vct_ref[...], p_c[u].astype(BF16)) for u in subs]

    imp = []
    for u in subs:
        p_sum = p_c[u][:, 0:tq]
        for r in range(1, R):
            p_sum = p_sum + p_c[u][:, r * tq:(r + 1) * tq]
        p_hi, p_lo = _split_hi_lo(p_sum)
        imp.append(_dot(c2st_ref[...], p_hi) + _dot(c2st_ref[...], p_lo))

    p_w = []
    for u in subs:
        dpos = tcols[u](band) - (starts[u] + rowi(band))
        bias_w = jnp.where((dpos >= 0) & (dpos < WINDOW), 0.0, NEG_INF)
        p_w.append(_softmax_cols(s_w[u] + tile_r(bias_w)))
    o_win = [_dot(vwt_ref[:, pl.ds(starts[u], band)], p_w[u].astype(BF16)) for u in subs]

    q4a = []
    for u in subs:
        jj = rowi(n_slc)
        cur = tcols[u](n_slc) // SEL_BLOCK
        valid = jj <= cur
        forced = (jj == 0) | (jj == cur) | (jj == cur - 1)
        score = jnp.where(forced, SEL_FORCED, jnp.where(valid, imp[u], -SEL_FORCED))
        sel = _topk_mask(score, min(SEL_TOPN, n_slc)) & valid
        bias = _bias_rows_to_cols(sel)
        q4a.append(jnp.concatenate([jnp.concatenate([h, bias], axis=1) for h in heads[u]], axis=0))

    def sel_body(kt, carry):
        k0 = pl.multiple_of(kt * SEL_KT, SEL_KT)
        s = [_dot_nt(ksa_ref[pl.ds(k0, SEL_KT), :], q4a[u]) for u in range(U)]
        return tuple(_online_update(s[u], vst_ref[:, pl.ds(k0, SEL_KT)], carry[u]) for u in range(U))

    n_full = base // SEL_KT
    init = (jnp.full((1, lanes), NEG_INF, F32), jnp.zeros((1, lanes), F32), jnp.zeros((HEAD_DIM, lanes), F32))
    carry = lax.fori_loop(0, n_full, sel_body, (init,) * U)
    k0 = pl.multiple_of(n_full * SEL_KT, SEL_KT)
    gt = gt_ref[...]
    assert U * tq == SEL_KT
    n_diag = [(u + 1) * tq for u in subs]
    s_d = [_dot_nt(ksa_ref[pl.ds(k0, n_diag[u]), :], q4a[u]) for u in subs]
    for u in subs:
        causal = jnp.where(k0 + rowi(n_diag[u]) <= tcols[u](n_diag[u]), 0.0, NEG_INF)
        _, l_s, acc_s = _online_update(s_d[u] + tile_r(causal), vst_ref[:, pl.ds(k0, n_diag[u])], carry[u])
        o_slc = acc_s * (1.0 / l_s)

        g = gt[:, u * tq:(u + 1) * tq]
        for r in range(R):
            sl = slice(r * tq, (r + 1) * tq)
            o_r = (g[r:r + 1, :] * o_cmp[u][:, sl] + g[R + r:R + r + 1, :] * o_slc[:, sl]
                   + g[2 * R + r:2 * R + r + 1, :] * o_win[u][:, sl])
            o_ref[u * tq:(u + 1) * tq, r * HEAD_DIM:(r + 1) * HEAD_DIM] = o_r.T.astype(BF16)


def _block_onehot(seq, block):
    assert seq // block <= HEAD_DIM
    e = np.zeros((seq, HEAD_DIM), np.float32)
    e[np.arange(seq), np.arange(seq) // block] = 1.0
    return jnp.asarray(e, BF16)


def _nsa_attention(proj3, kc, vc, gates_t, c2st):
    B, S, _ = proj3.shape
    G = NSA_GROUPS
    nc = kc.shape[2]
    tq = NSA_TQ * NSA_SUB
    assert S % tq == 0 and SEL_KT % tq == 0
    nq = S // tq
    hb = HEAD_DIM
    gw = NSA_REP * HEAD_DIM
    e_sel = _block_onehot(S, SEL_BLOCK)
    seq_spec = lambda off: pl.BlockSpec((None, S, hb), lambda b, g, i: (b, 0, off // hb + g))
    cmp_spec = pl.BlockSpec((None, None, nc, hb), lambda b, g, i: (b, g, 0, 0))
    return pl.pallas_call(
        functools.partial(_nsa_kernel, seq=S),
        grid=(B, G, nq),
        in_specs=[pl.BlockSpec((None, tq, gw), lambda b, g, i: (b, i, OFF_QA // gw + g)),
                  cmp_spec, cmp_spec,
                  seq_spec(OFF_KSLC), seq_spec(OFF_VSLC), seq_spec(OFF_KWIN), seq_spec(OFF_VWIN),
                  pl.BlockSpec((GATE_ROWS, tq), lambda b, g, i: (g, b * nq + i)),
                  pl.BlockSpec(c2st.shape, lambda b, g, i: (0, 0)),
                  pl.BlockSpec(e_sel.shape, lambda b, g, i: (0, 0))],
        out_specs=pl.BlockSpec((None, tq, gw), lambda b, g, i: (b, i, g)),
        out_shape=jax.ShapeDtypeStruct((B, S, NSA_WIDTH), BF16),
        scratch_shapes=[pltpu.VMEM((S, 2 * hb), BF16), pltpu.VMEM((hb, S), BF16), pltpu.VMEM((hb, S), BF16),
                        pltpu.VMEM((hb, nc), BF16)],
        compiler_params=_cparams(("arbitrary", "arbitrary", "arbitrary")),
        name="nsa_attn",
    )(proj3, kc, vc, proj3, proj3, proj3, proj3, gates_t, c2st, e_sel)


def _moba_kernel(q_ref, k_ref, v_ref, e_ref, o_ref, ka_ref, vt_ref, kmh_ref, kml_ref, *, seq):
    blk = MOBA_BLOCK
    HP = MOBA_HEADS_PER_STEP
    i = pl.program_id(2)
    nb = seq // blk
    nbp = kmh_ref.shape[1]
    hsl = lambda h: slice(h * HEAD_DIM, (h + 1) * HEAD_DIM)

    @pl.when(i == 0)
    def _():
        for h in range(HP):
            _transpose_into(v_ref, h * HEAD_DIM, vt_ref.at[h], seq)
            ka_ref[h, :, 0:HEAD_DIM] = k_ref[:, hsl(h)]
            ka_ref[h, :, HEAD_DIM:2 * HEAD_DIM] = e_ref[...]
            means = [jnp.sum(k_ref[n * blk:(n + 1) * blk, hsl(h)].astype(F32), axis=0, keepdims=True) * (1.0 / blk)
                     for n in range(nb)]
            if nbp > nb:
                means.append(jnp.zeros((nbp - nb, HEAD_DIM), F32))
            hi, lo = _split_hi_lo(jnp.concatenate(means, axis=0))
            kmh_ref[h] = hi
            kml_ref[h] = lo

    own0 = pl.multiple_of(i * blk, blk)
    past = lax.broadcasted_iota(I32, (nbp, blk), 0) < i
    causal = jnp.where(lax.broadcasted_iota(I32, (blk, blk), 0) <= lax.broadcasted_iota(I32, (blk, blk), 1), 0.0, NEG_INF)
    hs = range(HP)
    q = [q_ref[:, hsl(h)] for h in hs]
    gate = [_dot_nt(kmh_ref[h], q[h]) + _dot_nt(kml_ref[h], q[h]) for h in hs]
    s_own = [_dot_nt(k_ref[pl.ds(own0, blk), hsl(h)], q[h]) for h in hs]
    m_own = [jnp.max(s_own[h] + causal, axis=0, keepdims=True) for h in hs]
    p_own = [jnp.exp2(s_own[h] + causal - m_own[h]) for h in hs]
    carry = [(m_own[h], jnp.sum(p_own[h], axis=0, keepdims=True),
              _dot(vt_ref[h, :, pl.ds(own0, blk)], p_own[h].astype(BF16))) for h in hs]
    qa = []
    for h in hs:
        sel = _topk_mask(jnp.where(past, gate[h], -SEL_FORCED), min(MOBA_TOPK, nb)) & past
        qa.append(jnp.concatenate([q[h], _bias_rows_to_cols(sel)], axis=1))

    per_iter = MOBA_BLOCKS_PER_ITER
    kt = per_iter * blk

    def body(j, carry):
        k0 = pl.multiple_of(j * kt, kt)
        s = [_dot_nt(ka_ref[h, pl.ds(k0, kt), :], qa[h]) for h in range(HP)]
        return tuple(_online_update(s[h], vt_ref[h, :, pl.ds(k0, kt)], carry[h]) for h in range(HP))

    carry = lax.fori_loop(0, (i + per_iter - 1) // per_iter, body, tuple(carry))
    for h in range(HP):
        _, l, acc = carry[h]
        o_ref[:, hsl(h)] = (acc * (1.0 / l)).T.astype(BF16)


def _moba_attention(proj3):
    B, S, _ = proj3.shape
    HP = MOBA_HEADS_PER_STEP
    blk = MOBA_BLOCK
    hb = HEAD_DIM
    assert (S // blk) % MOBA_BLOCKS_PER_ITER == 0 and MOBA_HEADS % HP == 0
    nbp = max(8, -(-(S // blk) // 8) * 8)
    e_blk = _block_onehot(S, blk)
    wide = HP * hb
    seq_spec = lambda off: pl.BlockSpec((None, S, wide), lambda b, h, i: (b, 0, off // wide + h))
    return pl.pallas_call(
        functools.partial(_moba_kernel, seq=S),
        grid=(B, MOBA_HEADS // HP, S // blk),
        in_specs=[pl.BlockSpec((None, blk, wide), lambda b, h, i: (b, i, OFF_QB // wide + h)),
                  seq_spec(OFF_KB), seq_spec(OFF_VB),
                  pl.BlockSpec(e_blk.shape, lambda b, h, i: (0, 0))],
        out_specs=pl.BlockSpec((None, blk, wide), lambda b, h, i: (b, i, h)),
        out_shape=jax.ShapeDtypeStruct((B, S, MOBA_WIDTH), BF16),
        scratch_shapes=[pltpu.VMEM((HP, S, 2 * hb), BF16), pltpu.VMEM((HP, hb, S), BF16),
                        pltpu.VMEM((HP, nbp, hb), BF16), pltpu.VMEM((HP, nbp, hb), BF16)],
        compiler_params=_cparams(("arbitrary", "arbitrary", "arbitrary")),
        name="moba_attn",
    )(proj3, proj3, proj3, e_blk)


def _merge_proj_kernel(oa_ref, ob_ref, wa_ref, wb_ref, ga_ref, gb_ref, o_ref):
    tm = oa_ref.shape[0]
    chunk = min(ROW_CHUNK, tm)
    for r0 in range(0, tm, chunk):
        rows = slice(r0, r0 + chunk)
        a = _dot(oa_ref[rows, :], wa_ref[...])
        b = _dot(ob_ref[rows, :], wb_ref[...])
        o_ref[rows, :] = (ga_ref[rows, :].astype(F32) * a + gb_ref[rows, :].astype(F32) * b).astype(BF16)


def _merge_proj(oa, ob, wa, wb, proj, tm):
    T = oa.shape[0]
    D = wa.shape[1]
    tn = min(COL_TILE, D)
    g0 = PLAIN_END // tn
    return pl.pallas_call(
        _merge_proj_kernel,
        grid=(T // tm, D // tn),
        in_specs=[pl.BlockSpec((tm, NSA_WIDTH), lambda i, j: (i, 0)),
                  pl.BlockSpec((tm, MOBA_WIDTH), lambda i, j: (i, 0)),
                  pl.BlockSpec((NSA_WIDTH, tn), lambda i, j: (0, j)),
                  pl.BlockSpec((MOBA_WIDTH, tn), lambda i, j: (0, j)),
                  pl.BlockSpec((tm, tn), lambda i, j: (i, g0 + j)),
                  pl.BlockSpec((tm, tn), lambda i, j: (i, g0 + D // tn + j))],
        out_specs=pl.BlockSpec((tm, tn), lambda i, j: (i, j)),
        out_shape=jax.ShapeDtypeStruct((T, D), BF16),
        compiler_params=_cparams(("arbitrary", "arbitrary")),
        name="merge_proj",
    )(oa, ob, wa, wb, proj, proj)


def _out_router_kernel(x_ref, mix_ref, wo_ref, g_ref, wrh_ref, wrl_ref, br_ref, x1_ref, h2_ref, te_ref, tw_ref):
    tm = x_ref.shape[0]
    E = wrh_ref.shape[0]
    x1 = x_ref[...] + _dot(mix_ref[...], wo_ref[...])
    x1_ref[...] = x1
    h2 = x1 * lax.rsqrt(jnp.mean(x1 * x1, axis=-1, keepdims=True) + NORM_EPS) * g_ref[...]
    _store_token_slabs(h2_ref, _pack_bf16_pairs(h2))
    h_hi, h_lo = _split_hi_lo(h2)
    logits = (_dot_nt(wrh_ref[...], h_hi) + _dot_nt(wrh_ref[...], h_lo) + _dot_nt(wrl_ref[...], h_hi)
              + br_ref[...])
    e_idx = lax.broadcasted_iota(I32, (E, tm), 0)
    picks = _topk_rows(logits, e_idx, E, TOP_K)
    vals = [p[0] for p in picks]
    exps = [jnp.exp(v - vals[0]) for v in vals]
    inv = 1.0 / (exps[0] + exps[1] + exps[2] + exps[3])
    te_ref[...] = jnp.concatenate([p[1] for p in picks] + [jnp.zeros((8 - TOP_K, tm), I32)], axis=0)
    w_t = jnp.concatenate([e * inv for e in exps] + [jnp.zeros((HEAD_DIM - TOP_K, tm), F32)], axis=0)
    tw_ref[...] = w_t.T


def _out_router(x2, mix, wo, g, wrh, wrl, br, tm):
    T, D = x2.shape
    E = wrh.shape[0]
    slab = _slab_rows(D // 2)
    full = lambda a: pl.BlockSpec(a.shape, lambda i: (0,) * a.ndim)
    return pl.pallas_call(
        _out_router_kernel,
        grid=(T // tm,),
        in_specs=[pl.BlockSpec((tm, D), lambda i: (i, 0)), pl.BlockSpec((tm, D), lambda i: (i, 0)),
                  full(wo), full(g), full(wrh), full(wrl), full(br)],
        out_specs=[pl.BlockSpec((tm, D), lambda i: (i, 0)), pl.BlockSpec((tm * slab, HEAD_DIM), lambda i: (i, 0)),
                   pl.BlockSpec((8, tm), lambda i: (0, i)), pl.BlockSpec((tm, HEAD_DIM), lambda i: (i, 0))],
        out_shape=[jax.ShapeDtypeStruct((T, D), F32), jax.ShapeDtypeStruct((T * slab, HEAD_DIM), jnp.uint32),
                   jax.ShapeDtypeStruct((8, T), I32), jax.ShapeDtypeStruct((T, HEAD_DIM), F32)],
        compiler_params=_cparams(("arbitrary",)),
        name="out_router",
    )(x2, mix, wo, g, wrh, wrl, br)


def _route_kernel(te_ref, dest_ref, blke_ref, padlo_ref, padhi_ref, run_ref, pstart_ref):
    ph = pl.program_id(0)
    i = pl.program_id(1)
    E = run_ref.shape[0]
    tm = te_ref.shape[1]
    nbp = blke_ref.shape[1]
    te = te_ref[...]
    e_idx = lax.broadcasted_iota(I32, (E, tm), 0)
    ohs = [te[k:k + 1, :] == e_idx for k in range(TOP_K)]
    oh = jnp.where(ohs[0] | ohs[1] | ohs[2] | ohs[3], 1.0, 0.0)
    tile_cnt = jnp.sum(oh, axis=1, keepdims=True)

    @pl.when((ph == 0) & (i == 0))
    def _():
        run_ref[...] = jnp.zeros(run_ref.shape, F32)

    @pl.when(ph == 0)
    def _():
        run_ref[...] = run_ref[...] + tile_cnt
        dest_ref[...] = jnp.zeros(dest_ref.shape, I32)

    @pl.when((ph == 1) & (i == 0))
    def _():
        counts = run_ref[...]
        padded = jnp.floor((counts + (ROW_BLOCK - 1)) * (1.0 / ROW_BLOCK)) * ROW_BLOCK
        row = lax.broadcasted_iota(I32, counts.shape, 0)
        incl = padded
        sh = 1
        while sh < E:
            incl = incl + jnp.where(row >= sh, pltpu.roll(incl, sh, 0), 0.0)
            sh *= 2
        pstart_ref[...] = incl - padded
        padlo_ref[...] = (incl - padded + counts).astype(I32)
        padhi_ref[...] = incl.astype(I32)
        blk_start = (lax.broadcasted_iota(I32, (E, nbp), 1) * ROW_BLOCK).astype(F32)
        blke_ref[...] = jnp.sum(jnp.where(incl[:, 0:1] <= blk_start, 1, 0), axis=0, keepdims=True).astype(I32)
        run_ref[...] = jnp.zeros(run_ref.shape, F32)

    @pl.when(ph == 1)
    def _():
        upper = jnp.where(lax.broadcasted_iota(I32, (tm, tm), 0) < lax.broadcasted_iota(I32, (tm, tm), 1), 1.0, 0.0)
        before = _dot(oh.astype(BF16), upper.astype(BF16))
        val = before + pstart_ref[:, 0:1] + run_ref[:, 0:1]
        rows = [jnp.sum(jnp.where(ohs[k], val, 0.0), axis=0, keepdims=True) for k in range(TOP_K)]
        dest_ref[...] = jnp.concatenate(rows + [jnp.zeros((8 - TOP_K, tm), F32)], axis=0).astype(I32)
        run_ref[...] = run_ref[...] + tile_cnt


def _route(top_e, n_experts, n_blk_pad, tm):
    T = top_e.shape[1]
    return pl.pallas_call(
        _route_kernel,
        grid=(2, T // tm),
        in_specs=[pl.BlockSpec((8, tm), lambda ph, i: (0, i))],
        out_specs=[pl.BlockSpec((8, tm), lambda ph, i: (0, i * ph)),
                   pl.BlockSpec((1, n_blk_pad), lambda ph, i: (0, 0)),
                   pl.BlockSpec((n_experts, HEAD_DIM), lambda ph, i: (0, 0)),
                   pl.BlockSpec((n_experts, HEAD_DIM), lambda ph, i: (0, 0))],
        out_shape=[jax.ShapeDtypeStruct((8, T), I32), jax.ShapeDtypeStruct((1, n_blk_pad), I32),
                   jax.ShapeDtypeStruct((n_experts, HEAD_DIM), I32), jax.ShapeDtypeStruct((n_experts, HEAD_DIM), I32)],
        scratch_shapes=[pltpu.VMEM((n_experts, HEAD_DIM), F32), pltpu.VMEM((n_experts, HEAD_DIM), F32)],
        compiler_params=_cparams(("arbitrary", "arbitrary")),
        name="route",
    )(top_e)


def _slab_copy(src_ref, src_tok, dst_ref, dst_tok, sem):
    src = src_ref.at[pl.ds(pl.multiple_of(src_tok * SLAB, SLAB), SLAB), :]
    dst = dst_ref.at[pl.ds(pl.multiple_of(dst_tok * SLAB, SLAB), SLAB), :]
    return pltpu.make_async_copy(src, dst, sem)


def _wait_slabs(slabs_hbm_ref, n_tokens, sem):
    assert n_tokens % WAIT_TOKENS == 0
    chunk = slabs_hbm_ref.at[pl.ds(0, WAIT_TOKENS * SLAB), :]
    for _ in range(n_tokens // WAIT_TOKENS):
        pltpu.make_async_copy(chunk, chunk, sem).wait()


def _dispatch_kernel(pad_ref, dest_ref, h_ref, xpad_ref, zero_ref, sem, zsem):
    tm = h_ref.shape[0] // SLAB

    @pl.when(pl.program_id(0) == 0)
    def _():
        zero_ref[...] = jnp.zeros(zero_ref.shape, zero_ref.dtype)

        def per_expert(e, _):
            lo, hi = pad_ref[0, e], pad_ref[1, e]

            def issue(r, _):
                _slab_copy(zero_ref, 0, xpad_ref, r, zsem).start()
                return 0
            lax.fori_loop(lo, hi, issue, 0)

            def drain(r, _):
                _slab_copy(zero_ref, 0, xpad_ref, 0, zsem).wait()
                return 0
            lax.fori_loop(lo, hi, drain, 0)
            return 0
        lax.fori_loop(0, pad_ref.shape[1], per_expert, 0)

    def issue(t, _):
        for k in range(TOP_K):
            _slab_copy(h_ref, t, xpad_ref, dest_ref[t * TOP_K + k], sem).start(priority=k % 2)
        return 0
    lax.fori_loop(0, tm, issue, 0, unroll=2)

    _wait_slabs(xpad_ref, TOP_K * tm, sem)


def _dispatch(pads, dest, h2, n_rows, tm):
    T, D = h2.shape[0] // SLAB, h2.shape[1]
    n_rows = n_rows * SLAB
    tm_rows = tm * SLAB
    grid_spec = pltpu.PrefetchScalarGridSpec(
        num_scalar_prefetch=1,
        grid=(T // tm,),
        in_specs=[pl.BlockSpec((tm * TOP_K,), lambda i, pads: (i,), memory_space=pltpu.SMEM),
                  pl.BlockSpec((tm_rows, D), lambda i, pads: (i, 0))],
        out_specs=pl.BlockSpec(memory_space=pl.ANY),
        scratch_shapes=[pltpu.VMEM((8, D), h2.dtype), pltpu.SemaphoreType.DMA(()), pltpu.SemaphoreType.DMA(())])
    return pl.pallas_call(
        _dispatch_kernel,
        grid_spec=grid_spec,
        out_shape=jax.ShapeDtypeStruct((n_rows, D), h2.dtype),
        compiler_params=_cparams(("arbitrary",)),
        name="dispatch",
    )(pads, dest, h2)


def _next_expert(blk_e, n_experts):
    nxt = jnp.min(jnp.where(blk_e[None, :] > blk_e[:, None], blk_e[None, :], n_experts), axis=1)
    return jnp.where(nxt < n_experts, nxt, -1).astype(I32)


def _stream_expert_weights(be_ref, nxt_ref, w_hbm, col_offsets, wbuf_ref, wb_refs, sem, cnt_ref, n_experts):
    n, m = pl.program_id(0), pl.program_id(1)
    tn = wb_refs[0].shape[1]
    e = be_ref[m]
    live = e < n_experts
    first = live & ((m == 0) | (be_ref[jnp.maximum(m - 1, 0)] != e))

    def copies(ex, col_tile, slot):
        c0 = pl.multiple_of(col_tile * tn, tn)
        return [pltpu.make_async_copy(w_hbm.at[ex, :, pl.ds(off + c0, tn)], wbuf_ref.at[slot, g], sem.at[slot, g])
                for g, off in enumerate(col_offsets)]

    @pl.when((n == 0) & (m == 0))
    def _():
        cnt_ref[0] = 0
        for c in copies(e, 0, 0):
            c.start()

    @pl.when(first)
    def _():
        slot = cnt_ref[0] % 2
        for c in copies(e, n, slot):
            c.wait()
        nxt = nxt_ref[m]

        @pl.when(nxt >= 0)
        def _():
            for c in copies(nxt, n, 1 - slot):
                c.start()

        @pl.when((nxt < 0) & (n + 1 < pl.num_programs(0)))
        def _():
            for c in copies(be_ref[0], n + 1, 1 - slot):
                c.start()

        for g, wb_ref in enumerate(wb_refs):
            wb_ref[...] = wbuf_ref[slot, g].astype(BF16)
        cnt_ref[0] = cnt_ref[0] + 1
    return live


def _expert_up_kernel(be_ref, nxt_ref, x_ref, w_hbm, bg_ref, bl_ref, act_ref, wbuf_ref, wgb_ref, wlb_ref, sem, cnt_ref,
                      *, n_experts, d_ff):
    live = _stream_expert_weights(be_ref, nxt_ref, w_hbm, (0, d_ff), wbuf_ref, (wgb_ref, wlb_ref), sem, cnt_ref,
                                  n_experts)

    @pl.when(live)
    def _():
        x_lo, x_hi = _unpack_bf16_pairs(_load_token_slabs(x_ref, ROW_BLOCK, SLAB))
        x = jnp.concatenate([x_lo.astype(BF16), x_hi.astype(BF16)], axis=1)
        gate =jnp.minimum(_dot(x, wgb_ref[...]) + bg_ref[...], SWIGLU_LIMIT)
        lin = jnp.clip(_dot(x, wlb_ref[...]) + bl_ref[...], -SWIGLU_LIMIT, SWIGLU_LIMIT)
        act_ref[...] = (gate * _sigmoid(SWIGLU_ALPHA * gate) * (lin + 1.0)).astype(BF16)

    @pl.when(jnp.logical_not(live))
    def _():
        act_ref[...] = jnp.zeros(act_ref.shape, BF16)


def _expert_up(blk_e, nxt_e, x_pad, w_gu, b_gu, n_blk, tn):
    P = x_pad.shape[0] // SLAB
    E, D = w_gu.shape[0], w_gu.shape[1]
    F = w_gu.shape[2] // 2
    nf = F // tn
    ex = lambda be, m: jnp.minimum(be[m], E - 1)
    grid_spec = pltpu.PrefetchScalarGridSpec(
        num_scalar_prefetch=2,
        grid=(nf, n_blk),
        in_specs=[pl.BlockSpec((ROW_BLOCK * _slab_rows(D // 2), HEAD_DIM), lambda n, m, be, nx: (m, 0)),
                  pl.BlockSpec(memory_space=pl.ANY),
                  pl.BlockSpec((None, 1, tn), lambda n, m, be, nx: (ex(be, m), 0, n)),
                  pl.BlockSpec((None, 1, tn), lambda n, m, be, nx: (ex(be, m), 0, nf + n))],
        out_specs=pl.BlockSpec((ROW_BLOCK, tn), lambda n, m, be, nx: (m, n)),
        scratch_shapes=[pltpu.VMEM((2, 2, D, tn), F32), pltpu.VMEM((D, tn), BF16), pltpu.VMEM((D, tn), BF16),
                        pltpu.SemaphoreType.DMA((2, 2)), pltpu.SMEM((1,), I32)])
    return pl.pallas_call(
        functools.partial(_expert_up_kernel, n_experts=E, d_ff=F),
        grid_spec=grid_spec,
        out_shape=jax.ShapeDtypeStruct((P, F), BF16),
        compiler_params=_cparams(("arbitrary", "arbitrary")),
        name="expert_up",
    )(blk_e, nxt_e, x_pad, w_gu, b_gu, b_gu)


def _expert_down_kernel(be_ref, nxt_ref, a_ref, w_hbm, b_ref, y_ref, wbuf_ref, wb_ref, sem, cnt_ref, *, n_experts):
    live = _stream_expert_weights(be_ref, nxt_ref, w_hbm, (0,), wbuf_ref, (wb_ref,), sem, cnt_ref, n_experts)

    @pl.when(live)
    def _():
        _store_token_slabs(y_ref, _pack_bf16_pairs(_dot(a_ref[...], wb_ref[...]) + b_ref[...]))

    @pl.when(jnp.logical_not(live))
    def _():
        y_ref[...] = jnp.zeros(y_ref.shape, y_ref.dtype)


def _expert_down(blk_e, nxt_e, act, w_d, b_d, n_blk):
    P, F = act.shape
    E, _, D = w_d.shape
    tn = D
    ex = lambda be, m: jnp.minimum(be[m], E - 1)
    grid_spec = pltpu.PrefetchScalarGridSpec(
        num_scalar_prefetch=2,
        grid=(1, n_blk),
        in_specs=[pl.BlockSpec((ROW_BLOCK, F), lambda n, m, be, nx: (m, 0)),
                  pl.BlockSpec(memory_space=pl.ANY),
                  pl.BlockSpec((None, 1, tn), lambda n, m, be, nx: (ex(be, m), 0, n))],
        out_specs=pl.BlockSpec((ROW_BLOCK * _slab_rows(D // 2), HEAD_DIM), lambda n, m, be, nx: (m, 0)),
        scratch_shapes=[pltpu.VMEM((2, 1, F, tn), F32), pltpu.VMEM((F, tn), BF16),
                        pltpu.SemaphoreType.DMA((2, 1)), pltpu.SMEM((1,), I32)])
    return pl.pallas_call(
        functools.partial(_expert_down_kernel, n_experts=E),
        grid_spec=grid_spec,
        out_shape=jax.ShapeDtypeStruct((P * SLAB, HEAD_DIM), jnp.uint32),
        compiler_params=_cparams(("arbitrary", "arbitrary")),
        name="expert_down",
    )(blk_e, nxt_e, act, w_d, b_d)


def _combine_kernel(dest_ref, dest_next_ref, x1_ref, tw_ref, g_ref, ypad_ref, o_ref, ybuf_ref, sem):
    tm = x1_ref.shape[0]
    i = pl.program_id(0)
    slot = i % 2

    def gather(d_ref, s):
        def issue(t, _):
            for k in range(TOP_K):
                _slab_copy(ypad_ref, d_ref[t * TOP_K + k], ybuf_ref.at[s, k], t, sem.at[s]).start(priority=k % 2)
            return 0
        lax.fori_loop(0, tm, issue, 0, unroll=2)

    @pl.when(i == 0)
    def _():
        gather(dest_ref, 0)

    @pl.when(i + 1 < pl.num_programs(0))
    def _():
        gather(dest_next_ref, 1 - slot)

    _wait_slabs(ypad_ref, TOP_K * tm, sem.at[slot])

    tw = tw_ref[...]
    half = SLAB * HEAD_DIM
    acc_lo = x1_ref[:, :half]
    acc_hi = x1_ref[:, half:]
    for k in range(TOP_K):
        y_lo, y_hi = _unpack_bf16_pairs(_load_token_slabs(ybuf_ref.at[slot, k], tm, SLAB))
        acc_lo = acc_lo + tw[:, k:k + 1] * y_lo
        acc_hi = acc_hi + tw[:, k:k + 1] * y_hi
    acc = jnp.concatenate([acc_lo, acc_hi], axis=1)
    o_ref[...] = acc * lax.rsqrt(jnp.mean(acc * acc, axis=-1, keepdims=True) + NORM_EPS) * g_ref[...]


def _combine(dest, x1, top_w, g, y_pad, tm):
    T, D = x1.shape
    n = T // tm
    return pl.pallas_call(
        _combine_kernel,
        grid=(n,),
        in_specs=[pl.BlockSpec((tm * TOP_K,), lambda i: (i,), memory_space=pltpu.SMEM),
                  pl.BlockSpec((tm * TOP_K,), lambda i: (jnp.minimum(i + 1, n - 1),), memory_space=pltpu.SMEM),
                  pl.BlockSpec((tm, D), lambda i: (i, 0)),
                  pl.BlockSpec((tm, HEAD_DIM), lambda i: (i, 0)),
                  pl.BlockSpec((1, D), lambda i: (0, 0)),
                  pl.BlockSpec(memory_space=pl.ANY)],
        out_specs=pl.BlockSpec((tm, D), lambda i: (i, 0)),
        out_shape=jax.ShapeDtypeStruct((T, D), F32),
        scratch_shapes=[pltpu.VMEM((2, TOP_K, tm * _slab_rows(D // 2), HEAD_DIM), y_pad.dtype),
                        pltpu.SemaphoreType.DMA((2,))],
        compiler_params=_cparams(("arbitrary",)),
        name="combine",
    )(dest, dest, x1, top_w, g, y_pad)


def _rope_tables(pos):
    inv_freq = ROPE_THETA ** (-jnp.arange(0, ROPE_DIM, 2, dtype=F32) / ROPE_DIM)
    ang = pos.astype(F32)[..., None] * inv_freq
    cos, sin = jnp.cos(ang), jnp.sin(ang)
    rest = HEAD_DIM - ROPE_DIM
    cos_t = jnp.concatenate([cos, cos, jnp.ones(cos.shape[:-1] + (rest,), F32)], axis=-1)
    sin_t = jnp.concatenate([-sin, sin, jnp.zeros(sin.shape[:-1] + (rest,), F32)], axis=-1)
    return cos_t, sin_t


def _cmp_to_slc_t(seq, nc_pad):
    n_cmp = (seq - CMP_BLOCK) // CMP_STRIDE + 1
    n_slc = seq // SEL_BLOCK
    cmp_start = np.arange(n_cmp) * CMP_STRIDE
    slc_start = np.arange(n_slc) * SEL_BLOCK
    overlap = np.clip(np.minimum(cmp_start[:, None] + CMP_BLOCK, slc_start[None, :] + SEL_BLOCK)
                      - np.maximum(cmp_start[:, None], slc_start[None, :]), 0, None)
    out = np.zeros((n_slc, nc_pad), np.float32)
    out[:, :n_cmp] = (overlap / CMP_BLOCK).T
    return jnp.asarray(out, BF16)


def _regroup_w_in(w):
    D = w.shape[0]
    o_kv = NSA_WIDTH
    o_gn = o_kv + 6 * KV_WIDTH
    o_b = o_gn + 3 * NSA_HEADS
    o_gm = o_b + 3 * MOBA_WIDTH
    kv = lambda s: w[:, o_kv + s * KV_WIDTH:o_kv + (s + 1) * KV_WIDTH]
    mb = lambda s: w[:, o_b + s * MOBA_WIDTH:o_b + (s + 1) * MOBA_WIDTH]
    cols = [w[:, :NSA_WIDTH], kv(2), kv(4), mb(0), mb(1), kv(0), kv(1), kv(3), kv(5), mb(2), w[:, o_gm:o_gm + 2 * D]]
    w_main = jnp.concatenate(cols, axis=1).astype(BF16)
    wg = w[:, o_gn:o_gn + 3 * NSA_HEADS]
    wg = wg.reshape(D, 3, NSA_GROUPS, NSA_REP).transpose(2, 1, 3, 0).reshape(NSA_GROUPS, 3 * NSA_REP, D)
    wg = jnp.pad(wg, ((0, 0), (0, GATE_ROWS - 3 * NSA_REP), (0, 0))).reshape(NSA_GROUPS * GATE_ROWS, D)
    return w_main, wg.astype(BF16)


def kernel(x, positions, g_attn_norm, w_in, pe_cmp_k, w_cmp_k1, w_cmp_k2, pe_cmp_v, w_cmp_v1, w_cmp_v2, w_proj_nsa, w_proj_moba, w_out, g_ffn_norm, w_router, b_router, w_gate_up, b_gate_up, w_down, b_down, g_final_norm):
    B, S, D = x.shape
    T = B * S
    E = w_router.shape[-1]
    F = w_down.shape[-2]
    depth = w_in.shape[0]
    assert depth == 1, "the final norm is fused into the single layer's combine step"
    assert S % MOBA_BLOCK == 0 and S >= WINDOW + NSA_TQ and (2 * D) % COL_TILE == 0 and T % 512 == 0
    nc = S // CMP_STRIDE
    tm_big = min(1024, T)

    cos_t, sin_t = _rope_tables(positions)
    cmp_end = np.arange(nc) * CMP_STRIDE + CMP_BLOCK - 1
    cmp_end = np.minimum(cmp_end, S - 1)
    cos_c, sin_c = cos_t[:, cmp_end], sin_t[:, cmp_end]
    c2st = _cmp_to_slc_t(S, nc)

    x2 = x.reshape(T, D)
    for layer in range(depth):
        w_main, w_gates_t = _regroup_w_in(w_in[layer])
        h, gates_t = _norm_gates(x2, g_attn_norm[layer].reshape(1, D), w_gates_t, tm_big)
        proj = _in_proj(h, w_main, cos_t.reshape(T, HEAD_DIM), sin_t.reshape(T, HEAD_DIM), min(2048, T))
        proj3 = proj.reshape(B, S, proj.shape[1])

        xc = proj3[:, :, OFF_KCMP:OFF_KCMP + 2 * KV_WIDTH].reshape(B, nc, CMP_STRIDE * 2 * KV_WIDTH)
        kc, vc = _compress(xc, pe_cmp_k[layer], w_cmp_k1[layer].astype(BF16), w_cmp_k2[layer].astype(BF16),
                           pe_cmp_v[layer], w_cmp_v1[layer].astype(BF16), w_cmp_v2[layer].astype(BF16), cos_c, sin_c)
        o_a = _nsa_attention(proj3, kc, vc, gates_t, c2st)
        o_b = _moba_attention(proj3)

        mix = _merge_proj(o_a.reshape(T, NSA_WIDTH), o_b.reshape(T, MOBA_WIDTH),
                          w_proj_nsa[layer].astype(BF16), w_proj_moba[layer].astype(BF16), proj, tm_big)
        wr_t = w_router[layer].T
        wr_hi = wr_t.astype(BF16)
        wr_lo = (wr_t - wr_hi.astype(F32)).astype(BF16)
        x1, h2, top_e, top_w = _out_router(x2, mix, w_out[layer].astype(BF16), g_ffn_norm[layer].reshape(1, D),
                                           wr_hi, wr_lo, b_router[layer].reshape(E, 1), 512)

        n_blk = (T * TOP_K) // ROW_BLOCK + E
        n_blk_pad = -(-n_blk // HEAD_DIM) * HEAD_DIM
        dest, blk_e, pad_lo, pad_hi = _route(top_e, E, n_blk_pad, 512)
        pads = jnp.stack([pad_lo[:, 0], pad_hi[:, 0]])
        dest = dest[:TOP_K].T.reshape(T * TOP_K)
        x_pad = _dispatch(pads, dest, h2, n_blk * ROW_BLOCK, 512)
        blk_e = blk_e.reshape(n_blk_pad)
        nxt_e = _next_expert(blk_e, E)
        act = _expert_up(blk_e, nxt_e, x_pad, w_gate_up[layer], b_gate_up[layer].reshape(E, 1, 2 * F), n_blk,
                         min(1024, F))
        y_pad = _expert_down(blk_e, nxt_e, act, w_down[layer], b_down[layer].reshape(E, 1, D), n_blk)
        x2 = _combine(dest, x1, top_w, g_final_norm.reshape(1, D), y_pad, 256)
    return x2.reshape(B, S, D)
```

```python
import functools

import jax
import jax.numpy as jnp
import numpy as np
from jax import lax
from jax.experimental import pallas as pl
from jax.experimental.pallas import tpu as pltpu

F32 = jnp.float32
BF16 = jnp.bfloat16
I32 = jnp.int32

HEAD_DIM = 128
ROPE_DIM = HEAD_DIM // 4
ROPE_HALF = ROPE_DIM // 2
ROPE_THETA = 500000.0
NORM_EPS = 1e-5
NEG_INF = -1e30
REMOVED = -3e38
SEL_FORCED = 1e9

NSA_HEADS = 8
NSA_GROUPS = 2
NSA_REP = NSA_HEADS // NSA_GROUPS
CMP_BLOCK = 32
CMP_STRIDE = 16
CMP_HIDDEN = 256
SEL_BLOCK = 64
SEL_TOPN = 16
WINDOW = 512
MOBA_HEADS = 8
MOBA_BLOCK = 256
MOBA_TOPK = 3
TOP_K = 4
SWIGLU_LIMIT = 7.0
SWIGLU_ALPHA = 1.702
ROW_BLOCK = 256

NSA_WIDTH = NSA_HEADS * HEAD_DIM
MOBA_WIDTH = MOBA_HEADS * HEAD_DIM
KV_WIDTH = NSA_GROUPS * HEAD_DIM
SCALE = HEAD_DIM ** -0.5
Q_SCALE = SCALE * 1.4426950408889634

COL_TILE = 512
OFF_QA = 0
OFF_KSLC = NSA_WIDTH
OFF_KWIN = OFF_KSLC + KV_WIDTH
OFF_QB = OFF_KWIN + KV_WIDTH
OFF_KB = OFF_QB + MOBA_WIDTH
ROPE_COLS = OFF_KB + MOBA_WIDTH
OFF_KCMP = ROPE_COLS
OFF_VCMP = OFF_KCMP + KV_WIDTH
OFF_VSLC = OFF_VCMP + KV_WIDTH
OFF_VWIN = OFF_VSLC + KV_WIDTH
OFF_VB = OFF_VWIN + KV_WIDTH
PLAIN_END = OFF_VB + MOBA_WIDTH
N_ROPE_TILES = ROPE_COLS // COL_TILE
N_PLAIN_TILES = (PLAIN_END - ROPE_COLS) // COL_TILE
Q_TILES = tuple(range(OFF_QA // COL_TILE, OFF_KSLC // COL_TILE)) + tuple(range(OFF_QB // COL_TILE, OFF_KB // COL_TILE))

VMEM_LIMIT = 56 * 1024 * 1024
NSA_TQ = 128
SEL_KT = 512
GATE_ROWS = 16
IN_PROJ_CHUNK = 256
ROW_CHUNK = 256
NSA_SUB = 4
MOBA_HEADS_PER_STEP = 4
MOBA_BLOCKS_PER_ITER = 4
SLAB = 8
WAIT_TOKENS = 256


def _cparams(sem):
    return pltpu.CompilerParams(dimension_semantics=sem, vmem_limit_bytes=VMEM_LIMIT)


def _sigmoid(z):
    return 1.0 / (1.0 + jnp.exp(-z))


def _dot(a, b):
    return jnp.dot(a, b, preferred_element_type=F32)


def _dot_nt(a, b):
    return lax.dot_general(a, b, (((1,), (1,)), ((), ())), preferred_element_type=F32)


def _split_hi_lo(v):
    hi = v.astype(BF16)
    lo = (v - hi.astype(F32)).astype(BF16)
    return hi, lo


def _pack_bf16_pairs(v):
    half = v.shape[1] // 2
    lo = pltpu.bitcast(v[:, :half].astype(BF16).astype(F32), jnp.uint32)
    hi = pltpu.bitcast(v[:, half:].astype(BF16).astype(F32), jnp.uint32)
    return lax.shift_right_logical(lo, jnp.uint32(16)) | hi


def _unpack_bf16_pairs(w):
    lo = pltpu.bitcast(lax.shift_left(w, jnp.uint32(16)), F32)
    hi = pltpu.bitcast(w & jnp.uint32(0xFFFF0000), F32)
    return lo, hi


def _slab_rows(width_words):
    assert width_words == SLAB * HEAD_DIM, "a token's packed words must fill exactly one (8, 128) tile"
    return SLAB


def _store_token_slabs(ref, words):
    m, width = words.shape
    rows = _slab_rows(width)
    for s in range(rows):
        ref[pl.ds(s, m, stride=rows), :] = words[:, s * HEAD_DIM:(s + 1) * HEAD_DIM]


def _load_token_slabs(ref, m, rows):
    return jnp.concatenate([ref[pl.ds(s, m, stride=rows), :] for s in range(rows)], axis=1)


def _topk_rows(work, row_idx, n_rows, k):
    picks = []
    for _ in range(k):
        m = jnp.max(work, axis=0, keepdims=True)
        first = jnp.min(jnp.where(work == m, row_idx, n_rows), axis=0, keepdims=True)
        pick = row_idx == first
        picks.append((m, first, pick))
        work = jnp.where(pick, REMOVED, work)
    return picks


def _norm_gates_kernel(x_ref, g_ref, wgt_ref, h_ref, gt_ref):
    x = x_ref[...]
    h = x * lax.rsqrt(jnp.mean(x * x, axis=-1, keepdims=True) + NORM_EPS) * g_ref[...]
    hb = h.astype(BF16)
    h_ref[...] = hb
    gt_ref[...] = _sigmoid(_dot_nt(wgt_ref[...], hb))


def _norm_gates(x2, g, wgt, tm):
    T, D = x2.shape
    R = wgt.shape[0]
    return pl.pallas_call(
        _norm_gates_kernel,
        grid=(T // tm,),
        in_specs=[pl.BlockSpec((tm, D), lambda i: (i, 0)),
                  pl.BlockSpec((1, D), lambda i: (0, 0)),
                  pl.BlockSpec((R, D), lambda i: (0, 0))],
        out_specs=[pl.BlockSpec((tm, D), lambda i: (i, 0)),
                   pl.BlockSpec((R, tm), lambda i: (0, i))],
        out_shape=[jax.ShapeDtypeStruct((T, D), BF16), jax.ShapeDtypeStruct((R, T), F32)],
        compiler_params=_cparams(("arbitrary",)),
        name="norm_gates",
    )(x2, g, wgt)


def _rope_tile(xh, c, s, lane):
    rot = jnp.where(lane < ROPE_HALF, pltpu.roll(xh, HEAD_DIM - ROPE_HALF, 1), pltpu.roll(xh, ROPE_HALF, 1))
    return xh * c + rot * s


def _in_proj_kernel(h_ref, w_ref, c_ref, s_ref, o_ref):
    j = pl.program_id(1)
    tm = h_ref.shape[0]
    chunk = min(IN_PROJ_CHUNK, tm)

    def by_chunks(epilogue):
        for r0 in range(0, tm, chunk):
            rows = slice(r0, r0 + chunk)
            epilogue(rows, _dot(h_ref[rows, :], w_ref[...]))

    @pl.when(j < N_ROPE_TILES)
    def _():
        is_q = j == Q_TILES[0]
        for qt in Q_TILES[1:]:
            is_q = is_q | (j == qt)
        f = jnp.where(is_q, Q_SCALE, 1.0).astype(F32)
        lane = lax.broadcasted_iota(I32, (chunk, HEAD_DIM), 1)

        def rope(rows, acc):
            c = c_ref[rows, :] * f
            s = s_ref[rows, :] * f
            for hh in range(COL_TILE // HEAD_DIM):
                sl = slice(hh * HEAD_DIM, (hh + 1) * HEAD_DIM)
                o_ref[rows, sl] = _rope_tile(acc[:, sl], c, s, lane).astype(BF16)
        by_chunks(rope)

    @pl.when((j >= N_ROPE_TILES) & (j < N_ROPE_TILES + N_PLAIN_TILES))
    def _():
        def plain(rows, acc):
            o_ref[rows, :] = acc.astype(BF16)
        by_chunks(plain)

    @pl.when(j >= N_ROPE_TILES + N_PLAIN_TILES)
    def _():
        def gate(rows, acc):
            o_ref[rows, :] = _sigmoid(acc).astype(BF16)
        by_chunks(gate)


def _in_proj(h, w, cos_t, sin_t, tm):
    T, D = h.shape
    N = w.shape[1]
    return pl.pallas_call(
        _in_proj_kernel,
        grid=(T // tm, N // COL_TILE),
        in_specs=[pl.BlockSpec((tm, D), lambda i, j: (i, 0)),
                  pl.BlockSpec((D, COL_TILE), lambda i, j: (0, j)),
                  pl.BlockSpec((tm, HEAD_DIM), lambda i, j: (i, 0)),
                  pl.BlockSpec((tm, HEAD_DIM), lambda i, j: (i, 0))],
        out_specs=pl.BlockSpec((tm, COL_TILE), lambda i, j: (i, j)),
        out_shape=jax.ShapeDtypeStruct((T, N), BF16),
        compiler_params=_cparams(("arbitrary", "arbitrary")),
        name="in_proj",
    )(h, w, cos_t, sin_t)


def _gelu_tanh(x):
    return 0.5 * x * (1.0 + jnp.tanh(0.7978845608028654 * (x + 0.044715 * x * x * x)))


def _compress_kernel(x_ref, pek_ref, w1k_ref, w2k_ref, pev_ref, w1v_ref, w2v_ref, c_ref, s_ref, kc_ref, vc_ref,
                     xf_ref):
    n_slabs = xf_ref.shape[0]
    nc = x_ref.shape[0] // CMP_STRIDE
    half = CMP_STRIDE * HEAD_DIM
    for c in range(n_slabs):
        xf_ref[c] = x_ref[:, c * HEAD_DIM:(c + 1) * HEAD_DIM].astype(F32)
    for which, (pe_ref, w1_ref, w2_ref, out_ref) in enumerate(
            ((pek_ref, w1k_ref, w2k_ref, kc_ref), (pev_ref, w1v_ref, w2v_ref, vc_ref))):
        for g in range(NSA_GROUPS):
            acc_a = jnp.zeros((nc, CMP_HIDDEN), F32)
            acc_b = jnp.zeros((nc, CMP_HIDDEN), F32)
            for l in range(CMP_STRIDE):
                xl = xf_ref[which * NSA_GROUPS + g, pl.ds(l, nc, stride=CMP_STRIDE), :]
                xa =(xl + pe_ref[l:l + 1, :]).astype(BF16)
                xb = (xl + pe_ref[CMP_STRIDE + l:CMP_STRIDE + l + 1, :]).astype(BF16)
                acc_a = acc_a + _dot(xa, w1_ref[l * HEAD_DIM:(l + 1) * HEAD_DIM, :])
                acc_b = acc_b + _dot(xb, w1_ref[half + l * HEAD_DIM:half + (l + 1) * HEAD_DIM, :])
            hid = _gelu_tanh(acc_a + pltpu.roll(acc_b, nc - 1, 0))
            out = _dot(hid.astype(BF16), w2_ref[...])
            if which == 0:
                lane = lax.broadcasted_iota(I32, out.shape, 1)
                out = _rope_tile(out, c_ref[...], s_ref[...], lane)
            out_ref[g] = out.astype(BF16)


def _compress(proj3, pek, w1k, w2k, pev, w1v, w2v, cos_c, sin_c):
    B, S, _ = proj3.shape
    nc = S // CMP_STRIDE
    W = 2 * KV_WIDTH
    full = lambda a: pl.BlockSpec(a.shape, lambda b: (0,) * a.ndim)
    out_sds = jax.ShapeDtypeStruct((B, NSA_GROUPS, nc, HEAD_DIM), BF16)
    out_spec = pl.BlockSpec((None, NSA_GROUPS, nc, HEAD_DIM), lambda b: (b, 0, 0, 0))
    return pl.pallas_call(
        _compress_kernel,
        grid=(B,),
        in_specs=[pl.BlockSpec((None, S, W), lambda b: (b, 0, OFF_KCMP // W)),
                  full(pek), full(w1k), full(w2k), full(pev), full(w1v), full(w2v),
                  pl.BlockSpec((None, nc, HEAD_DIM), lambda b: (b, 0, 0)),
                  pl.BlockSpec((None, nc, HEAD_DIM), lambda b: (b, 0, 0))],
        out_specs=[out_spec, out_spec],
        out_shape=[out_sds, out_sds],
        scratch_shapes=[pltpu.VMEM((W // HEAD_DIM, S, HEAD_DIM), F32)],
        compiler_params=_cparams(("arbitrary",)),
        name="compress",
    )(proj3, pek, w1k, w2k, pev, w1v, w2v, cos_c, sin_c)


def _transpose_into(src_ref, col0, dst_ref, n_rows):
    def body(b, _):
        r0 = pl.multiple_of(b * HEAD_DIM, HEAD_DIM)
        blk = src_ref[pl.ds(r0, HEAD_DIM), col0:col0 + HEAD_DIM].astype(F32)
        dst_ref[:, pl.ds(r0, HEAD_DIM)] = blk.T.astype(BF16)
        return 0
    lax.fori_loop(0, n_rows // HEAD_DIM, body, 0)


def _topk_mask(score, k):
    n, L = score.shape
    sub = 8
    assert n % sub == 0
    groups = n // sub
    segs = [score[sub * g:sub * (g + 1), :] for g in range(groups)]
    sub_idx = lax.broadcasted_iota(I32, (sub, L), 0)
    later = [jnp.where(sub_idx > r, 1.0, 0.0) for r in range(sub)]
    ranks = [jnp.zeros((sub, L), F32) for _ in range(groups)]
    for i in range(n):
        gi, ri = divmod(i, sub)
        row = score[i:i + 1, :]
        for g in range(groups):
            if g > gi:
                inc = jnp.where(row >= segs[g], 1.0, 0.0)
            elif g < gi:
                inc = jnp.where(row > segs[g], 1.0, 0.0)
            else:
                inc = jnp.where(row > segs[g], 1.0, jnp.where(row == segs[g], later[ri], 0.0))
            ranks[g] = ranks[g] + inc
    return jnp.concatenate(ranks, axis=0) < k


def _bias_rows_to_cols(sel, lanes_out=HEAD_DIM):
    n, L = sel.shape
    bias_t = jnp.where(sel, 0.0, NEG_INF)
    if n < lanes_out:
        bias_t = jnp.concatenate([bias_t, jnp.zeros((lanes_out - n, L), F32)], axis=0)
    return bias_t.T.astype(BF16)


def _softmax_cols(s, live=None):
    m = jnp.max(s, axis=0, keepdims=True)
    p = jnp.exp2(s - m)
    inv = 1.0 / jnp.sum(p, axis=0, keepdims=True)
    if live is not None:
        inv = jnp.where(live, inv, 0.0)
    return p * inv


def _online_update(s, vt, carry):
    m, l, acc = carry
    m_new = jnp.maximum(m, jnp.max(s, axis=0, keepdims=True))
    alpha = jnp.exp2(m - m_new)
    p = jnp.exp2(s - m_new)
    l = alpha * l + jnp.sum(p, axis=0, keepdims=True)
    acc = alpha * acc + _dot(vt, p.astype(BF16))
    return m_new, l, acc


def _nsa_kernel(q_ref, kc_ref, vc_ref, ks_ref, vs_ref, kw_ref, vw_ref, gt_ref, c2st_ref, e_ref, o_ref,
                ksa_ref, vst_ref, vwt_ref, vct_ref, *, seq):
    tq = NSA_TQ
    U = NSA_SUB
    R = NSA_REP
    i = pl.program_id(2)
    base = i * (U * tq)
    nc = kc_ref.shape[0]
    n_slc = c2st_ref.shape[0]
    lanes = R * tq
    band = WINDOW + tq
    tile_r = lambda a: jnp.concatenate([a] * R, axis=1)
    rowi = lambda n: lax.broadcasted_iota(I32, (n, tq), 0)

    @pl.when(i == 0)
    def _():
        _transpose_into(vs_ref, 0, vst_ref, seq)
        _transpose_into(vw_ref, 0, vwt_ref, seq)
        _transpose_into(vc_ref, 0, vct_ref, nc)
        ksa_ref[:, 0:HEAD_DIM] = ks_ref[...]
        ksa_ref[:, HEAD_DIM:2 * HEAD_DIM] = e_ref[...]

    subs = range(U)
    heads = [[q_ref[u * tq:(u + 1) * tq, r * HEAD_DIM:(r + 1) * HEAD_DIM] for r in range(R)] for u in subs]
    q4 = [jnp.concatenate(heads[u], axis=0) for u in subs]
    tcols = [lambda n, q0=base + u * tq: q0 + lax.broadcasted_iota(I32, (n, tq), 1) for u in subs]
    starts = [pl.multiple_of(jnp.maximum(base + u * tq - WINDOW, 0), HEAD_DIM) for u in subs]

    s_c = [_dot_nt(kc_ref[...], q4[u]) for u in subs]
    s_w = [_dot_nt(kw_ref[pl.ds(starts[u], band), :], q4[u]) for u in subs]

    p_c = []
    for u in subs:
        bias_c = jnp.where(rowi(nc) * CMP_STRIDE + (CMP_BLOCK - 1) <= tcols[u](nc), 0.0, NEG_INF)
        live_c = tcols[u](1) >= CMP_BLOCK - 1
        p_c.append(_softmax_cols(s_c[u] + tile_r(bias_c), tile_r(live_c)))
    o_cmp = [_dot(vct_ref[...], p_c[u].astype(BF16)) for u in subs]

    imp = []
    for u in subs:
        p_sum = p_c[u][:, 0:tq]
        for r in range(1, R):
            p_sum = p_sum + p_c[u][:, r * tq:(r + 1) * tq]
        p_hi, p_lo = _split_hi_lo(p_sum)
        imp.append(_dot(c2st_ref[...], p_hi) + _dot(c2st_ref[...], p_lo))

    p_w = []
    for u in subs:
        dpos = tcols[u](band) - (starts[u] + rowi(band))
        bias_w = jnp.where((dpos >= 0) & (dpos < WINDOW), 0.0, NEG_INF)
        p_w.append(_softmax_cols(s_w[u] + tile_r(bias_w)))
    o_win = [_dot(vwt_ref[:, pl.ds(starts[u], band)], p_w[u].astype(BF16)) for u in subs]

    q4a = []
    for u in subs:
        jj = rowi(n_slc)
        cur = tcols[u](n_slc) // SEL_BLOCK
        valid = jj <= cur
        forced = (jj == 0) | (jj == cur) | (jj == cur - 1)
        score = jnp.where(forced, SEL_FORCED, jnp.where(valid, imp[u], -SEL_FORCED))
        sel = _topk_mask(score, min(SEL_TOPN, n_slc)) & valid
        bias = _bias_rows_to_cols(sel)
        q4a.append(jnp.concatenate([jnp.concatenate([h, bias], axis=1) for h in heads[u]], axis=0))

    def sel_body(kt, carry):
        k0 = pl.multiple_of(kt * SEL_KT, SEL_KT)
        s = [_dot_nt(ksa_ref[pl.ds(k0, SEL_KT), :], q4a[u]) for u in range(U)]
        return tuple(_online_update(s[u], vst_ref[:, pl.ds(k0, SEL_KT)], carry[u]) for u in range(U))

    n_full = base // SEL_KT
    init = (jnp.full((1, lanes), NEG_INF, F32), jnp.zeros((1, lanes), F32), jnp.zeros((HEAD_DIM, lanes), F32))
    carry = lax.fori_loop(0, n_full, sel_body, (init,) * U)
    k0 = pl.multiple_of(n_full * SEL_KT, SEL_KT)
    gt = gt_ref[...]
    assert U * tq == SEL_KT
    n_diag = [(u + 1) * tq for u in subs]
    s_d = [_dot_nt(ksa_ref[pl.ds(k0, n_diag[u]), :], q4a[u]) for u in subs]
    for u in subs:
        causal = jnp.where(k0 + rowi(n_diag[u]) <= tcols[u](n_diag[u]), 0.0, NEG_INF)
        _, l_s, acc_s = _online_update(s_d[u] + tile_r(causal), vst_ref[:, pl.ds(k0, n_diag[u])], carry[u])
        o_slc = acc_s * (1.0 / l_s)

        g = gt[:, u * tq:(u + 1) * tq]
        for r in range(R):
            sl = slice(r * tq, (r + 1) * tq)
            o_r = (g[r:r + 1, :] * o_cmp[u][:, sl] + g[R + r:R + r + 1, :] * o_slc[:, sl]
                   + g[2 * R + r:2 * R + r + 1, :] * o_win[u][:, sl])
            o_ref[u * tq:(u + 1) * tq, r * HEAD_DIM:(r + 1) * HEAD_DIM] = o_r.T.astype(BF16)


def _block_onehot(seq, block):
    assert seq // block <= HEAD_DIM
    e = np.zeros((seq, HEAD_DIM), np.float32)
    e[np.arange(seq), np.arange(seq) // block] = 1.0
    return jnp.asarray(e, BF16)


def _nsa_attention(proj3, kc, vc, gates_t, c2st):
    B, S, _ = proj3.shape
    G = NSA_GROUPS
    nc = kc.shape[2]
    tq = NSA_TQ * NSA_SUB
    assert S % tq == 0 and SEL_KT % tq == 0
    nq = S // tq
    hb = HEAD_DIM
    gw = NSA_REP * HEAD_DIM
    e_sel = _block_onehot(S, SEL_BLOCK)
    seq_spec = lambda off: pl.BlockSpec((None, S, hb), lambda b, g, i: (b, 0, off // hb + g))
    cmp_spec = pl.BlockSpec((None, None, nc, hb), lambda b, g, i: (b, g, 0, 0))
    return pl.pallas_call(
        functools.partial(_nsa_kernel, seq=S),
        grid=(B, G, nq),
        in_specs=[pl.BlockSpec((None, tq, gw), lambda b, g, i: (b, i, OFF_QA // gw + g)),
                  cmp_spec, cmp_spec,
                  seq_spec(OFF_KSLC), seq_spec(OFF_VSLC), seq_spec(OFF_KWIN), seq_spec(OFF_VWIN),
                  pl.BlockSpec((GATE_ROWS, tq), lambda b, g, i: (g, b * nq + i)),
                  pl.BlockSpec(c2st.shape, lambda b, g, i: (0, 0)),
                  pl.BlockSpec(e_sel.shape, lambda b, g, i: (0, 0))],
        out_specs=pl.BlockSpec((None, tq, gw), lambda b, g, i: (b, i, g)),
        out_shape=jax.ShapeDtypeStruct((B, S, NSA_WIDTH), BF16),
        scratch_shapes=[pltpu.VMEM((S, 2 * hb), BF16), pltpu.VMEM((hb, S), BF16), pltpu.VMEM((hb, S), BF16),
                        pltpu.VMEM((hb, nc), BF16)],
        compiler_params=_cparams(("arbitrary", "arbitrary", "arbitrary")),
        name="nsa_attn",
    )(proj3, kc, vc, proj3, proj3, proj3, proj3, gates_t, c2st, e_sel)


def _moba_kernel(q_ref, k_ref, v_ref, e_ref, o_ref, ka_ref, vt_ref, kmh_ref, kml_ref, *, seq):
    blk = MOBA_BLOCK
    HP = MOBA_HEADS_PER_STEP
    i = pl.program_id(2)
    nb = seq // blk
    nbp = kmh_ref.shape[1]
    hsl = lambda h: slice(h * HEAD_DIM, (h + 1) * HEAD_DIM)

    @pl.when(i == 0)
    def _():
        for h in range(HP):
            _transpose_into(v_ref, h * HEAD_DIM, vt_ref.at[h], seq)
            ka_ref[h, :, 0:HEAD_DIM] = k_ref[:, hsl(h)]
            ka_ref[h, :, HEAD_DIM:2 * HEAD_DIM] = e_ref[...]
            means = [jnp.sum(k_ref[n * blk:(n + 1) * blk, hsl(h)].astype(F32), axis=0, keepdims=True) * (1.0 / blk)
                     for n in range(nb)]
            if nbp > nb:
                means.append(jnp.zeros((nbp - nb, HEAD_DIM), F32))
            hi, lo = _split_hi_lo(jnp.concatenate(means, axis=0))
            kmh_ref[h] = hi
            kml_ref[h] = lo

    own0 = pl.multiple_of(i * blk, blk)
    past = lax.broadcasted_iota(I32, (nbp, blk), 0) < i
    causal = jnp.where(lax.broadcasted_iota(I32, (blk, blk), 0) <= lax.broadcasted_iota(I32, (blk, blk), 1), 0.0, NEG_INF)
    hs = range(HP)
    q = [q_ref[:, hsl(h)] for h in hs]
    gate = [_dot_nt(kmh_ref[h], q[h]) + _dot_nt(kml_ref[h], q[h]) for h in hs]
    s_own = [_dot_nt(k_ref[pl.ds(own0, blk), hsl(h)], q[h]) for h in hs]
    m_own = [jnp.max(s_own[h] + causal, axis=0, keepdims=True) for h in hs]
    p_own = [jnp.exp2(s_own[h] + causal - m_own[h]) for h in hs]
    carry = [(m_own[h], jnp.sum(p_own[h], axis=0, keepdims=True),
              _dot(vt_ref[h, :, pl.ds(own0, blk)], p_own[h].astype(BF16))) for h in hs]
    qa = []
    for h in hs:
        sel = _topk_mask(jnp.where(past, gate[h], -SEL_FORCED), min(MOBA_TOPK, nb)) & past
        qa.append(jnp.concatenate([q[h], _bias_rows_to_cols(sel)], axis=1))

    per_iter = MOBA_BLOCKS_PER_ITER
    kt = per_iter * blk

    def body(j, carry):
        k0 = pl.multiple_of(j * kt, kt)
        s = [_dot_nt(ka_ref[h, pl.ds(k0, kt), :], qa[h]) for h in range(HP)]
        return tuple(_online_update(s[h], vt_ref[h, :, pl.ds(k0, kt)], carry[h]) for h in range(HP))

    carry = lax.fori_loop(0, (i + per_iter - 1) // per_iter, body, tuple(carry))
    for h in range(HP):
        _, l, acc = carry[h]
        o_ref[:, hsl(h)] = (acc * (1.0 / l)).T.astype(BF16)


def _moba_attention(proj3):
    B, S, _ = proj3.shape
    HP = MOBA_HEADS_PER_STEP
    blk = MOBA_BLOCK
    hb = HEAD_DIM
    assert (S // blk) % MOBA_BLOCKS_PER_ITER == 0 and MOBA_HEADS % HP == 0
    nbp = max(8, -(-(S // blk) // 8) * 8)
    e_blk = _block_onehot(S, blk)
    wide = HP * hb
    seq_spec = lambda off: pl.BlockSpec((None, S, wide), lambda b, h, i: (b, 0, off // wide + h))
    return pl.pallas_call(
        functools.partial(_moba_kernel, seq=S),
        grid=(B, MOBA_HEADS // HP, S // blk),
        in_specs=[pl.BlockSpec((None, blk, wide), lambda b, h, i: (b, i, OFF_QB // wide + h)),
                  seq_spec(OFF_KB), seq_spec(OFF_VB),
                  pl.BlockSpec(e_blk.shape, lambda b, h, i: (0, 0))],
        out_specs=pl.BlockSpec((None, blk, wide), lambda b, h, i: (b, i, h)),
        out_shape=jax.ShapeDtypeStruct((B, S, MOBA_WIDTH), BF16),
        scratch_shapes=[pltpu.VMEM((HP, S, 2 * hb), BF16), pltpu.VMEM((HP, hb, S), BF16),
                        pltpu.VMEM((HP, nbp, hb), BF16), pltpu.VMEM((HP, nbp, hb), BF16)],
        compiler_params=_cparams(("arbitrary", "arbitrary", "arbitrary")),
        name="moba_attn",
    )(proj3, proj3, proj3, e_blk)


def _merge_proj_kernel(oa_ref, ob_ref, wa_ref, wb_ref, ga_ref, gb_ref, o_ref):
    tm = oa_ref.shape[0]
    chunk = min(ROW_CHUNK, tm)
    for r0 in range(0, tm, chunk):
        rows = slice(r0, r0 + chunk)
        a = _dot(oa_ref[rows, :], wa_ref[...])
        b = _dot(ob_ref[rows, :], wb_ref[...])
        o_ref[rows, :] = (ga_ref[rows, :].astype(F32) * a + gb_ref[rows, :].astype(F32) * b).astype(BF16)


def _merge_proj(oa, ob, wa, wb, proj, tm):
    T = oa.shape[0]
    D = wa.shape[1]
    tn = min(COL_TILE, D)
    g0 = PLAIN_END // tn
    return pl.pallas_call(
        _merge_proj_kernel,
        grid=(T // tm, D // tn),
        in_specs=[pl.BlockSpec((tm, NSA_WIDTH), lambda i, j: (i, 0)),
                  pl.BlockSpec((tm, MOBA_WIDTH), lambda i, j: (i, 0)),
                  pl.BlockSpec((NSA_WIDTH, tn), lambda i, j: (0, j)),
                  pl.BlockSpec((MOBA_WIDTH, tn), lambda i, j: (0, j)),
                  pl.BlockSpec((tm, tn), lambda i, j: (i, g0 + j)),
                  pl.BlockSpec((tm, tn), lambda i, j: (i, g0 + D // tn + j))],
        out_specs=pl.BlockSpec((tm, tn), lambda i, j: (i, j)),
        out_shape=jax.ShapeDtypeStruct((T, D), BF16),
        compiler_params=_cparams(("arbitrary", "arbitrary")),
        name="merge_proj",
    )(oa, ob, wa, wb, proj, proj)


def _out_router_kernel(x_ref, mix_ref, wo_ref, g_ref, wrh_ref, wrl_ref, br_ref, x1_ref, h2_ref, te_ref, tw_ref):
    tm = x_ref.shape[0]
    E = wrh_ref.shape[0]
    x1 = x_ref[...] + _dot(mix_ref[...], wo_ref[...])
    x1_ref[...] = x1
    h2 = x1 * lax.rsqrt(jnp.mean(x1 * x1, axis=-1, keepdims=True) + NORM_EPS) * g_ref[...]
    _store_token_slabs(h2_ref, _pack_bf16_pairs(h2))
    h_hi, h_lo = _split_hi_lo(h2)
    logits = (_dot_nt(wrh_ref[...], h_hi) + _dot_nt(wrh_ref[...], h_lo) + _dot_nt(wrl_ref[...], h_hi)
              + br_ref[...])
    e_idx = lax.broadcasted_iota(I32, (E, tm), 0)
    picks = _topk_rows(logits, e_idx, E, TOP_K)
    vals = [p[0] for p in picks]
    exps = [jnp.exp(v - vals[0]) for v in vals]
    inv = 1.0 / (exps[0] + exps[1] + exps[2] + exps[3])
    te_ref[...] = jnp.concatenate([p[1] for p in picks] + [jnp.zeros((8 - TOP_K, tm), I32)], axis=0)
    w_t = jnp.concatenate([e * inv for e in exps] + [jnp.zeros((HEAD_DIM - TOP_K, tm), F32)], axis=0)
    tw_ref[...] = w_t.T


def _out_router(x2, mix, wo, g, wrh, wrl, br, tm):
    T, D = x2.shape
    E = wrh.shape[0]
    slab = _slab_rows(D // 2)
    full = lambda a: pl.BlockSpec(a.shape, lambda i: (0,) * a.ndim)
    return pl.pallas_call(
        _out_router_kernel,
        grid=(T // tm,),
        in_specs=[pl.BlockSpec((tm, D), lambda i: (i, 0)), pl.BlockSpec((tm, D), lambda i: (i, 0)),
                  full(wo), full(g), full(wrh), full(wrl), full(br)],
        out_specs=[pl.BlockSpec((tm, D), lambda i: (i, 0)), pl.BlockSpec((tm * slab, HEAD_DIM), lambda i: (i, 0)),
                   pl.BlockSpec((8, tm), lambda i: (0, i)), pl.BlockSpec((tm, HEAD_DIM), lambda i: (i, 0))],
        out_shape=[jax.ShapeDtypeStruct((T, D), F32), jax.ShapeDtypeStruct((T * slab, HEAD_DIM), jnp.uint32),
                   jax.ShapeDtypeStruct((8, T), I32), jax.ShapeDtypeStruct((T, HEAD_DIM), F32)],
        compiler_params=_cparams(("arbitrary",)),
        name="out_router",
    )(x2, mix, wo, g, wrh, wrl, br)


def _route_kernel(te_ref, dest_ref, blke_ref, padlo_ref, padhi_ref, run_ref, pstart_ref):
    ph = pl.program_id(0)
    i = pl.program_id(1)
    E = run_ref.shape[0]
    tm = te_ref.shape[1]
    nbp = blke_ref.shape[1]
    te = te_ref[...]
    e_idx = lax.broadcasted_iota(I32, (E, tm), 0)
    ohs = [te[k:k + 1, :] == e_idx for k in range(TOP_K)]
    oh = jnp.where(ohs[0] | ohs[1] | ohs[2] | ohs[3], 1.0, 0.0)
    tile_cnt = jnp.sum(oh, axis=1, keepdims=True)

    @pl.when((ph == 0) & (i == 0))
    def _():
        run_ref[...] = jnp.zeros(run_ref.shape, F32)

    @pl.when(ph == 0)
    def _():
        run_ref[...] = run_ref[...] + tile_cnt
        dest_ref[...] = jnp.zeros(dest_ref.shape, I32)

    @pl.when((ph == 1) & (i == 0))
    def _():
        counts = run_ref[...]
        padded = jnp.floor((counts + (ROW_BLOCK - 1)) * (1.0 / ROW_BLOCK)) * ROW_BLOCK
        row = lax.broadcasted_iota(I32, counts.shape, 0)
        incl = padded
        sh = 1
        while sh < E:
            incl = incl + jnp.where(row >= sh, pltpu.roll(incl, sh, 0), 0.0)
            sh *= 2
        pstart_ref[...] = incl - padded
        padlo_ref[...] = (incl - padded + counts).astype(I32)
        padhi_ref[...] = incl.astype(I32)
        blk_start = (lax.broadcasted_iota(I32, (E, nbp), 1) * ROW_BLOCK).astype(F32)
        blke_ref[...] = jnp.sum(jnp.where(incl[:, 0:1] <= blk_start, 1, 0), axis=0, keepdims=True).astype(I32)
        run_ref[...] = jnp.zeros(run_ref.shape, F32)

    @pl.when(ph == 1)
    def _():
        upper = jnp.where(lax.broadcasted_iota(I32, (tm, tm), 0) < lax.broadcasted_iota(I32, (tm, tm), 1), 1.0, 0.0)
        before = _dot(oh.astype(BF16), upper.astype(BF16))
        val = before + pstart_ref[:, 0:1] + run_ref[:, 0:1]
        rows = [jnp.sum(jnp.where(ohs[k], val, 0.0), axis=0, keepdims=True) for k in range(TOP_K)]
        dest_ref[...] = jnp.concatenate(rows + [jnp.zeros((8 - TOP_K, tm), F32)], axis=0).astype(I32)
        run_ref[...] = run_ref[...] + tile_cnt


def _route(top_e, n_experts, n_blk_pad, tm):
    T = top_e.shape[1]
    return pl.pallas_call(
        _route_kernel,
        grid=(2, T // tm),
        in_specs=[pl.BlockSpec((8, tm), lambda ph, i: (0, i))],
        out_specs=[pl.BlockSpec((8, tm), lambda ph, i: (0, i * ph)),
                   pl.BlockSpec((1, n_blk_pad), lambda ph, i: (0, 0)),
                   pl.BlockSpec((n_experts, HEAD_DIM), lambda ph, i: (0, 0)),
                   pl.BlockSpec((n_experts, HEAD_DIM), lambda ph, i: (0, 0))],
        out_shape=[jax.ShapeDtypeStruct((8, T), I32), jax.ShapeDtypeStruct((1, n_blk_pad), I32),
                   jax.ShapeDtypeStruct((n_experts, HEAD_DIM), I32), jax.ShapeDtypeStruct((n_experts, HEAD_DIM), I32)],
        scratch_shapes=[pltpu.VMEM((n_experts, HEAD_DIM), F32), pltpu.VMEM((n_experts, HEAD_DIM), F32)],
        compiler_params=_cparams(("arbitrary", "arbitrary")),
        name="route",
    )(top_e)


def _slab_copy(src_ref, src_tok, dst_ref, dst_tok, sem):
    src = src_ref.at[pl.ds(pl.multiple_of(src_tok * SLAB, SLAB), SLAB), :]
    dst = dst_ref.at[pl.ds(pl.multiple_of(dst_tok * SLAB, SLAB), SLAB), :]
    return pltpu.make_async_copy(src, dst, sem)


def _wait_slabs(slabs_hbm_ref, n_tokens, sem):
    assert n_tokens % WAIT_TOKENS == 0
    chunk = slabs_hbm_ref.at[pl.ds(0, WAIT_TOKENS * SLAB), :]
    for _ in range(n_tokens // WAIT_TOKENS):
        pltpu.make_async_copy(chunk, chunk, sem).wait()


def _dispatch_kernel(pad_ref, dest_ref, h_ref, xpad_ref, zero_ref, sem, zsem):
    tm = h_ref.shape[0] // SLAB

    @pl.when(pl.program_id(0) == 0)
    def _():
        zero_ref[...] = jnp.zeros(zero_ref.shape, zero_ref.dtype)

        def per_expert(e, _):
            lo, hi = pad_ref[0, e], pad_ref[1, e]

            def issue(r, _):
                _slab_copy(zero_ref, 0, xpad_ref, r, zsem).start()
                return 0
            lax.fori_loop(lo, hi, issue, 0)

            def drain(r, _):
                _slab_copy(zero_ref, 0, xpad_ref, 0, zsem).wait()
                return 0
            lax.fori_loop(lo, hi, drain, 0)
            return 0
        lax.fori_loop(0, pad_ref.shape[1], per_expert, 0)

    def issue(t, _):
        for k in range(TOP_K):
            _slab_copy(h_ref, t, xpad_ref, dest_ref[t * TOP_K + k], sem).start(priority=k % 2)
        return 0
    lax.fori_loop(0, tm, issue, 0, unroll=2)

    _wait_slabs(xpad_ref, TOP_K * tm, sem)


def _dispatch(pads, dest, h2, n_rows, tm):
    T, D = h2.shape[0] // SLAB, h2.shape[1]
    n_rows = n_rows * SLAB
    tm_rows = tm * SLAB
    grid_spec = pltpu.PrefetchScalarGridSpec(
        num_scalar_prefetch=1,
        grid=(T // tm,),
        in_specs=[pl.BlockSpec((tm * TOP_K,), lambda i, pads: (i,), memory_space=pltpu.SMEM),
                  pl.BlockSpec((tm_rows, D), lambda i, pads: (i, 0))],
        out_specs=pl.BlockSpec(memory_space=pl.ANY),
        scratch_shapes=[pltpu.VMEM((8, D), h2.dtype), pltpu.SemaphoreType.DMA(()), pltpu.SemaphoreType.DMA(())])
    return pl.pallas_call(
        _dispatch_kernel,
        grid_spec=grid_spec,
        out_shape=jax.ShapeDtypeStruct((n_rows, D), h2.dtype),
        compiler_params=_cparams(("arbitrary",)),
        name="dispatch",
    )(pads, dest, h2)


def _next_expert(blk_e, n_experts):
    nxt = jnp.min(jnp.where(blk_e[None, :] > blk_e[:, None], blk_e[None, :], n_experts), axis=1)
    return jnp.where(nxt < n_experts, nxt, -1).astype(I32)


def _stream_expert_weights(be_ref, nxt_ref, w_hbm, col_offsets, wbuf_ref, wb_refs, sem, cnt_ref, n_experts):
    n, m = pl.program_id(0), pl.program_id(1)
    tn = wb_refs[0].shape[1]
    e = be_ref[m]
    live = e < n_experts
    first = live & ((m == 0) | (be_ref[jnp.maximum(m - 1, 0)] != e))

    def copies(ex, col_tile, slot):
        c0 = pl.multiple_of(col_tile * tn, tn)
        return [pltpu.make_async_copy(w_hbm.at[ex, :, pl.ds(off + c0, tn)], wbuf_ref.at[slot, g], sem.at[slot, g])
                for g, off in enumerate(col_offsets)]

    @pl.when((n == 0) & (m == 0))
    def _():
        cnt_ref[0] = 0
        for c in copies(e, 0, 0):
            c.start()

    @pl.when(first)
    def _():
        slot = cnt_ref[0] % 2
        for c in copies(e, n, slot):
            c.wait()
        nxt = nxt_ref[m]

        @pl.when(nxt >= 0)
        def _():
            for c in copies(nxt, n, 1 - slot):
                c.start()

        @pl.when((nxt < 0) & (n + 1 < pl.num_programs(0)))
        def _():
            for c in copies(be_ref[0], n + 1, 1 - slot):
                c.start()

        for g, wb_ref in enumerate(wb_refs):
            wb_ref[...] = wbuf_ref[slot, g].astype(BF16)
        cnt_ref[0] = cnt_ref[0] + 1
    return live


def _expert_up_kernel(be_ref, nxt_ref, x_ref, w_hbm, bg_ref, bl_ref, act_ref, wbuf_ref, wgb_ref, wlb_ref, sem, cnt_ref,
                      *, n_experts, d_ff):
    live = _stream_expert_weights(be_ref, nxt_ref, w_hbm, (0, d_ff), wbuf_ref, (wgb_ref, wlb_ref), sem, cnt_ref,
                                  n_experts)

    @pl.when(live)
    def _():
        x_lo, x_hi = _unpack_bf16_pairs(_load_token_slabs(x_ref, ROW_BLOCK, SLAB))
        x = jnp.concatenate([x_lo.astype(BF16), x_hi.astype(BF16)], axis=1)
        gate =jnp.minimum(_dot(x, wgb_ref[...]) + bg_ref[...], SWIGLU_LIMIT)
        lin = jnp.clip(_dot(x, wlb_ref[...]) + bl_ref[...], -SWIGLU_LIMIT, SWIGLU_LIMIT)
        act_ref[...] = (gate * _sigmoid(SWIGLU_ALPHA * gate) * (lin + 1.0)).astype(BF16)

    @pl.when(jnp.logical_not(live))
    def _():
        act_ref[...] = jnp.zeros(act_ref.shape, BF16)


def _expert_up(blk_e, nxt_e, x_pad, w_gu, b_gu, n_blk, tn):
    P = x_pad.shape[0] // SLAB
    E, D = w_gu.shape[0], w_gu.shape[1]
    F = w_gu.shape[2] // 2
    nf = F // tn
    ex = lambda be, m: jnp.minimum(be[m], E - 1)
    grid_spec = pltpu.PrefetchScalarGridSpec(
        num_scalar_prefetch=2,
        grid=(nf, n_blk),
        in_specs=[pl.BlockSpec((ROW_BLOCK * _slab_rows(D // 2), HEAD_DIM), lambda n, m, be, nx: (m, 0)),
                  pl.BlockSpec(memory_space=pl.ANY),
                  pl.BlockSpec((None, 1, tn), lambda n, m, be, nx: (ex(be, m), 0, n)),
                  pl.BlockSpec((None, 1, tn), lambda n, m, be, nx: (ex(be, m), 0, nf + n))],
        out_specs=pl.BlockSpec((ROW_BLOCK, tn), lambda n, m, be, nx: (m, n)),
        scratch_shapes=[pltpu.VMEM((2, 2, D, tn), F32), pltpu.VMEM((D, tn), BF16), pltpu.VMEM((D, tn), BF16),
                        pltpu.SemaphoreType.DMA((2, 2)), pltpu.SMEM((1,), I32)])
    return pl.pallas_call(
        functools.partial(_expert_up_kernel, n_experts=E, d_ff=F),
        grid_spec=grid_spec,
        out_shape=jax.ShapeDtypeStruct((P, F), BF16),
        compiler_params=_cparams(("arbitrary", "arbitrary")),
        name="expert_up",
    )(blk_e, nxt_e, x_pad, w_gu, b_gu, b_gu)


def _expert_down_kernel(be_ref, nxt_ref, a_ref, w_hbm, b_ref, y_ref, wbuf_ref, wb_ref, sem, cnt_ref, *, n_experts):
    live = _stream_expert_weights(be_ref, nxt_ref, w_hbm, (0,), wbuf_ref, (wb_ref,), sem, cnt_ref, n_experts)

    @pl.when(live)
    def _():
        _store_token_slabs(y_ref, _pack_bf16_pairs(_dot(a_ref[...], wb_ref[...]) + b_ref[...]))

    @pl.when(jnp.logical_not(live))
    def _():
        y_ref[...] = jnp.zeros(y_ref.shape, y_ref.dtype)


def _expert_down(blk_e, nxt_e, act, w_d, b_d, n_blk):
    P, F = act.shape
    E, _, D = w_d.shape
    tn = D
    ex = lambda be, m: jnp.minimum(be[m], E - 1)
    grid_spec = pltpu.PrefetchScalarGridSpec(
        num_scalar_prefetch=2,
        grid=(1, n_blk),
        in_specs=[pl.BlockSpec((ROW_BLOCK, F), lambda n, m, be, nx: (m, 0)),
                  pl.BlockSpec(memory_space=pl.ANY),
                  pl.BlockSpec((None, 1, tn), lambda n, m, be, nx: (ex(be, m), 0, n))],
        out_specs=pl.BlockSpec((ROW_BLOCK * _slab_rows(D // 2), HEAD_DIM), lambda n, m, be, nx: (m, 0)),
        scratch_shapes=[pltpu.VMEM((2, 1, F, tn), F32), pltpu.VMEM((F, tn), BF16),
                        pltpu.SemaphoreType.DMA((2, 1)), pltpu.SMEM((1,), I32)])
    return pl.pallas_call(
        functools.partial(_expert_down_kernel, n_experts=E),
        grid_spec=grid_spec,
        out_shape=jax.ShapeDtypeStruct((P * SLAB, HEAD_DIM), jnp.uint32),
        compiler_params=_cparams(("arbitrary", "arbitrary")),
        name="expert_down",
    )(blk_e, nxt_e, act, w_d, b_d)


def _combine_kernel(dest_ref, dest_next_ref, x1_ref, tw_ref, g_ref, ypad_ref, o_ref, ybuf_ref, sem):
    tm = x1_ref.shape[0]
    i = pl.program_id(0)
    slot = i % 2

    def gather(d_ref, s):
        def issue(t, _):
            for k in range(TOP_K):
                _slab_copy(ypad_ref, d_ref[t * TOP_K + k], ybuf_ref.at[s, k], t, sem.at[s]).start(priority=k % 2)
            return 0
        lax.fori_loop(0, tm, issue, 0, unroll=2)

    @pl.when(i == 0)
    def _():
        gather(dest_ref, 0)

    @pl.when(i + 1 < pl.num_programs(0))
    def _():
        gather(dest_next_ref, 1 - slot)

    _wait_slabs(ypad_ref, TOP_K * tm, sem.at[slot])

    tw = tw_ref[...]
    half = SLAB * HEAD_DIM
    acc_lo = x1_ref[:, :half]
    acc_hi = x1_ref[:, half:]
    for k in range(TOP_K):
        y_lo, y_hi = _unpack_bf16_pairs(_load_token_slabs(ybuf_ref.at[slot, k], tm, SLAB))
        acc_lo = acc_lo + tw[:, k:k + 1] * y_lo
        acc_hi = acc_hi + tw[:, k:k + 1] * y_hi
    acc = jnp.concatenate([acc_lo, acc_hi], axis=1)
    o_ref[...] = acc * lax.rsqrt(jnp.mean(acc * acc, axis=-1, keepdims=True) + NORM_EPS) * g_ref[...]


def _combine(dest, x1, top_w, g, y_pad, tm):
    T, D = x1.shape
    n = T // tm
    return pl.pallas_call(
        _combine_kernel,
        grid=(n,),
        in_specs=[pl.BlockSpec((tm * TOP_K,), lambda i: (i,), memory_space=pltpu.SMEM),
                  pl.BlockSpec((tm * TOP_K,), lambda i: (jnp.minimum(i + 1, n - 1),), memory_space=pltpu.SMEM),
                  pl.BlockSpec((tm, D), lambda i: (i, 0)),
                  pl.BlockSpec((tm, HEAD_DIM), lambda i: (i, 0)),
                  pl.BlockSpec((1, D), lambda i: (0, 0)),
                  pl.BlockSpec(memory_space=pl.ANY)],
        out_specs=pl.BlockSpec((tm, D), lambda i: (i, 0)),
        out_shape=jax.ShapeDtypeStruct((T, D), F32),
        scratch_shapes=[pltpu.VMEM((2, TOP_K, tm * _slab_rows(D // 2), HEAD_DIM), y_pad.dtype),
                        pltpu.SemaphoreType.DMA((2,))],
        compiler_params=_cparams(("arbitrary",)),
        name="combine",
    )(dest, dest, x1, top_w, g, y_pad)


def _rope_tables(pos):
    inv_freq = ROPE_THETA ** (-jnp.arange(0, ROPE_DIM, 2, dtype=F32) / ROPE_DIM)
    ang = pos.astype(F32)[..., None] * inv_freq
    cos, sin = jnp.cos(ang), jnp.sin(ang)
    rest = HEAD_DIM - ROPE_DIM
    cos_t = jnp.concatenate([cos, cos, jnp.ones(cos.shape[:-1] + (rest,), F32)], axis=-1)
    sin_t = jnp.concatenate([-sin, sin, jnp.zeros(sin.shape[:-1] + (rest,), F32)], axis=-1)
    return cos_t, sin_t


def _cmp_to_slc_t(seq, nc_pad):
    n_cmp = (seq - CMP_BLOCK) // CMP_STRIDE + 1
    n_slc = seq // SEL_BLOCK
    cmp_start = np.arange(n_cmp) * CMP_STRIDE
    slc_start = np.arange(n_slc) * SEL_BLOCK
    overlap = np.clip(np.minimum(cmp_start[:, None] + CMP_BLOCK, slc_start[None, :] + SEL_BLOCK)
                      - np.maximum(cmp_start[:, None], slc_start[None, :]), 0, None)
    out = np.zeros((n_slc, nc_pad), np.float32)
    out[:, :n_cmp] = (overlap / CMP_BLOCK).T
    return jnp.asarray(out, BF16)


def _regroup_w_in(w):
    D = w.shape[0]
    o_kv = NSA_WIDTH
    o_gn = o_kv + 6 * KV_WIDTH
    o_b = o_gn + 3 * NSA_HEADS
    o_gm = o_b + 3 * MOBA_WIDTH
    kv = lambda s: w[:, o_kv + s * KV_WIDTH:o_kv + (s + 1) * KV_WIDTH]
    mb = lambda s: w[:, o_b + s * MOBA_WIDTH:o_b + (s + 1) * MOBA_WIDTH]
    cols = [w[:, :NSA_WIDTH], kv(2), kv(4), mb(0), mb(1), kv(0), kv(1), kv(3), kv(5), mb(2), w[:, o_gm:o_gm + 2 * D]]
    w_main = jnp.concatenate(cols, axis=1).astype(BF16)
    wg = w[:, o_gn:o_gn + 3 * NSA_HEADS]
    wg = wg.reshape(D, 3, NSA_GROUPS, NSA_REP).transpose(2, 1, 3, 0).reshape(NSA_GROUPS, 3 * NSA_REP, D)
    wg = jnp.pad(wg, ((0, 0), (0, GATE_ROWS - 3 * NSA_REP), (0, 0))).reshape(NSA_GROUPS * GATE_ROWS, D)
    return w_main, wg.astype(BF16)


def kernel(x, positions, g_attn_norm, w_in, pe_cmp_k, w_cmp_k1, w_cmp_k2, pe_cmp_v, w_cmp_v1, w_cmp_v2, w_proj_nsa, w_proj_moba, w_out, g_ffn_norm, w_router, b_router, w_gate_up, b_gate_up, w_down, b_down, g_final_norm):
    B, S, D = x.shape
    T = B * S
    E = w_router.shape[-1]
    F = w_down.shape[-2]
    depth = w_in.shape[0]
    assert depth == 1, "the final norm is fused into the single layer's combine step"
    assert S % MOBA_BLOCK == 0 and S >= WINDOW + NSA_TQ and (2 * D) % COL_TILE == 0 and T % 512 == 0
    nc = S // CMP_STRIDE
    tm_big = min(1024, T)

    cos_t, sin_t = _rope_tables(positions)
    cmp_end = np.arange(nc) * CMP_STRIDE + CMP_BLOCK - 1
    cmp_end = np.minimum(cmp_end, S - 1)
    cos_c, sin_c = cos_t[:, cmp_end], sin_t[:, cmp_end]
    c2st = _cmp_to_slc_t(S, nc)

    x2 = x.reshape(T, D)
    for layer in range(depth):
        w_main, w_gates_t = _regroup_w_in(w_in[layer])
        h, gates_t = _norm_gates(x2, g_attn_norm[layer].reshape(1, D), w_gates_t, tm_big)
        proj = _in_proj(h, w_main, cos_t.reshape(T, HEAD_DIM), sin_t.reshape(T, HEAD_DIM), min(2048, T))
        proj3 = proj.reshape(B, S, proj.shape[1])

        kc, vc = _compress(proj3, pe_cmp_k[layer], w_cmp_k1[layer].astype(BF16), w_cmp_k2[layer].astype(BF16),
                           pe_cmp_v[layer], w_cmp_v1[layer].astype(BF16), w_cmp_v2[layer].astype(BF16), cos_c, sin_c)
        o_a = _nsa_attention(proj3, kc, vc, gates_t, c2st)
        o_b = _moba_attention(proj3)

        mix = _merge_proj(o_a.reshape(T, NSA_WIDTH), o_b.reshape(T, MOBA_WIDTH),
                          w_proj_nsa[layer].astype(BF16), w_proj_moba[layer].astype(BF16), proj, tm_big)
        wr_t = w_router[layer].T
        wr_hi = wr_t.astype(BF16)
        wr_lo = (wr_t - wr_hi.astype(F32)).astype(BF16)
        x1, h2, top_e, top_w = _out_router(x2, mix, w_out[layer].astype(BF16), g_ffn_norm[layer].reshape(1, D),
                                           wr_hi, wr_lo, b_router[layer].reshape(E, 1), 512)

        n_blk = (T * TOP_K) // ROW_BLOCK + E
        n_blk_pad = -(-n_blk // HEAD_DIM) * HEAD_DIM
        dest, blk_e, pad_lo, pad_hi = _route(top_e, E, n_blk_pad, 512)
        pads = jnp.stack([pad_lo[:, 0], pad_hi[:, 0]])
        dest = dest[:TOP_K].T.reshape(T * TOP_K)
        x_pad = _dispatch(pads, dest, h2, n_blk * ROW_BLOCK, 512)
        blk_e = blk_e.reshape(n_blk_pad)
        nxt_e = _next_expert(blk_e, E)
        act = _expert_up(blk_e, nxt_e, x_pad, w_gate_up[layer], b_gate_up[layer].reshape(E, 1, 2 * F), n_blk,
                         min(1024, F))
        y_pad = _expert_down(blk_e, nxt_e, act, w_down[layer], b_down[layer].reshape(E, 1, D), n_blk)
        x2 = _combine(dest, x1, top_w, g_final_norm.reshape(1, D), y_pad, 256)
    return x2.reshape(B, S, D)
```

```python
import functools

import jax
import jax.numpy as jnp
import numpy as np
from jax import lax
from jax.experimental import pallas as pl
from jax.experimental.pallas import tpu as pltpu

F32 = jnp.float32
BF16 = jnp.bfloat16
I32 = jnp.int32

HEAD_DIM = 128
ROPE_DIM = HEAD_DIM // 4
ROPE_HALF = ROPE_DIM // 2
ROPE_THETA = 500000.0
NORM_EPS = 1e-5
NEG_INF = -1e30
REMOVED = -3e38
SEL_FORCED = 1e9

NSA_HEADS = 8
NSA_GROUPS = 2
NSA_REP = NSA_HEADS // NSA_GROUPS
CMP_BLOCK = 32
CMP_STRIDE = 16
CMP_HIDDEN = 256
SEL_BLOCK = 64
SEL_TOPN = 16
WINDOW = 512
MOBA_HEADS = 8
MOBA_BLOCK = 256
MOBA_TOPK = 3
TOP_K = 4
SWIGLU_LIMIT = 7.0
SWIGLU_ALPHA = 1.702
ROW_BLOCK = 256

NSA_WIDTH = NSA_HEADS * HEAD_DIM
MOBA_WIDTH = MOBA_HEADS * HEAD_DIM
KV_WIDTH = NSA_GROUPS * HEAD_DIM
SCALE = HEAD_DIM ** -0.5
Q_SCALE = SCALE * 1.4426950408889634

COL_TILE = 512
OFF_QA = 0
OFF_KSLC = NSA_WIDTH
OFF_KWIN = OFF_KSLC + KV_WIDTH
OFF_QB = OFF_KWIN + KV_WIDTH
OFF_KB = OFF_QB + MOBA_WIDTH
ROPE_COLS = OFF_KB + MOBA_WIDTH
OFF_KCMP = ROPE_COLS
OFF_VCMP = OFF_KCMP + KV_WIDTH
OFF_VSLC = OFF_VCMP + KV_WIDTH
OFF_VWIN = OFF_VSLC + KV_WIDTH
OFF_VB = OFF_VWIN + KV_WIDTH
PLAIN_END = OFF_VB + MOBA_WIDTH
N_ROPE_TILES = ROPE_COLS // COL_TILE
N_PLAIN_TILES = (PLAIN_END - ROPE_COLS) // COL_TILE
Q_TILES = tuple(range(OFF_QA // COL_TILE, OFF_KSLC // COL_TILE)) + tuple(range(OFF_QB // COL_TILE, OFF_KB // COL_TILE))

VMEM_LIMIT = 56 * 1024 * 1024
NSA_TQ = 128
SEL_KT = 512
GATE_ROWS = 16
IN_PROJ_CHUNK = 256
ROW_CHUNK = 256
NSA_SUB = 4
MOBA_HEADS_PER_STEP = 4
MOBA_BLOCKS_PER_ITER = 4
SLAB = 8
WAIT_TOKENS = 256


def _cparams(sem):
    return pltpu.CompilerParams(dimension_semantics=sem, vmem_limit_bytes=VMEM_LIMIT)


def _sigmoid(z):
    return 1.0 / (1.0 + jnp.exp(-z))


def _dot(a, b):
    return jnp.dot(a, b, preferred_element_type=F32)


def _dot_nt(a, b):
    return lax.dot_general(a, b, (((1,), (1,)), ((), ())), preferred_element_type=F32)


def _split_hi_lo(v):
    hi = v.astype(BF16)
    lo = (v - hi.astype(F32)).astype(BF16)
    return hi, lo


def _pack_bf16_pairs(v):
    half = v.shape[1] // 2
    lo = pltpu.bitcast(v[:, :half].astype(BF16).astype(F32), jnp.uint32)
    hi = pltpu.bitcast(v[:, half:].astype(BF16).astype(F32), jnp.uint32)
    return lax.shift_right_logical(lo, jnp.uint32(16)) | hi


def _unpack_bf16_pairs(w):
    lo = pltpu.bitcast(lax.shift_left(w, jnp.uint32(16)), F32)
    hi = pltpu.bitcast(w & jnp.uint32(0xFFFF0000), F32)
    return lo, hi


def _slab_rows(width_words):
    assert width_words == SLAB * HEAD_DIM, "a token's packed words must fill exactly one (8, 128) tile"
    return SLAB


def _store_token_slabs(ref, words):
    m, width = words.shape
    rows = _slab_rows(width)
    for s in range(rows):
        ref[pl.ds(s, m, stride=rows), :] = words[:, s * HEAD_DIM:(s + 1) * HEAD_DIM]


def _load_token_slabs(ref, m, rows):
    return jnp.concatenate([ref[pl.ds(s, m, stride=rows), :] for s in range(rows)], axis=1)


def _topk_rows(work, row_idx, n_rows, k):
    picks = []
    for _ in range(k):
        m = jnp.max(work, axis=0, keepdims=True)
        first = jnp.min(jnp.where(work == m, row_idx, n_rows), axis=0, keepdims=True)
        pick = row_idx == first
        picks.append((m, first, pick))
        work = jnp.where(pick, REMOVED, work)
    return picks


def _norm_gates_kernel(x_ref, g_ref, wgt_ref, h_ref, gt_ref):
    x = x_ref[...]
    h = x * lax.rsqrt(jnp.mean(x * x, axis=-1, keepdims=True) + NORM_EPS) * g_ref[...]
    hb = h.astype(BF16)
    h_ref[...] = hb
    gt_ref[...] = _sigmoid(_dot_nt(wgt_ref[...], hb))


def _norm_gates(x2, g, wgt, tm):
    T, D = x2.shape
    R = wgt.shape[0]
    return pl.pallas_call(
        _norm_gates_kernel,
        grid=(T // tm,),
        in_specs=[pl.BlockSpec((tm, D), lambda i: (i, 0)),
                  pl.BlockSpec((1, D), lambda i: (0, 0)),
                  pl.BlockSpec((R, D), lambda i: (0, 0))],
        out_specs=[pl.BlockSpec((tm, D), lambda i: (i, 0)),
                   pl.BlockSpec((R, tm), lambda i: (0, i))],
        out_shape=[jax.ShapeDtypeStruct((T, D), BF16), jax.ShapeDtypeStruct((R, T), F32)],
        compiler_params=_cparams(("arbitrary",)),
        name="norm_gates",
    )(x2, g, wgt)


def _rope_tile(xh, c, s, lane):
    rot = jnp.where(lane < ROPE_HALF, pltpu.roll(xh, HEAD_DIM - ROPE_HALF, 1), pltpu.roll(xh, ROPE_HALF, 1))
    return xh * c + rot * s


def _in_proj_kernel(h_ref, w_ref, c_ref, s_ref, o_ref):
    j = pl.program_id(1)
    tm = h_ref.shape[0]
    chunk = min(IN_PROJ_CHUNK, tm)

    def by_chunks(epilogue):
        for r0 in range(0, tm, chunk):
            rows = slice(r0, r0 + chunk)
            epilogue(rows, _dot(h_ref[rows, :], w_ref[...]))

    @pl.when(j < N_ROPE_TILES)
    def _():
        is_q = j == Q_TILES[0]
        for qt in Q_TILES[1:]:
            is_q = is_q | (j == qt)
        f = jnp.where(is_q, Q_SCALE, 1.0).astype(F32)
        lane = lax.broadcasted_iota(I32, (chunk, HEAD_DIM), 1)

        def rope(rows, acc):
            c = c_ref[rows, :] * f
            s = s_ref[rows, :] * f
            for hh in range(COL_TILE // HEAD_DIM):
                sl = slice(hh * HEAD_DIM, (hh + 1) * HEAD_DIM)
                o_ref[rows, sl] = _rope_tile(acc[:, sl], c, s, lane).astype(BF16)
        by_chunks(rope)

    @pl.when((j >= N_ROPE_TILES) & (j < N_ROPE_TILES + N_PLAIN_TILES))
    def _():
        def plain(rows, acc):
            o_ref[rows, :] = acc.astype(BF16)
        by_chunks(plain)

    @pl.when(j >= N_ROPE_TILES + N_PLAIN_TILES)
    def _():
        def gate(rows, acc):
            o_ref[rows, :] = _sigmoid(acc).astype(BF16)
        by_chunks(gate)


def _in_proj(h, w, cos_t, sin_t, tm):
    T, D = h.shape
    N = w.shape[1]
    return pl.pallas_call(
        _in_proj_kernel,
        grid=(T // tm, N // COL_TILE),
        in_specs=[pl.BlockSpec((tm, D), lambda i, j: (i, 0)),
                  pl.BlockSpec((D, COL_TILE), lambda i, j: (0, j)),
                  pl.BlockSpec((tm, HEAD_DIM), lambda i, j: (i, 0)),
                  pl.BlockSpec((tm, HEAD_DIM), lambda i, j: (i, 0))],
        out_specs=pl.BlockSpec((tm, COL_TILE), lambda i, j: (i, j)),
        out_shape=jax.ShapeDtypeStruct((T, N), BF16),
        compiler_params=_cparams(("arbitrary", "arbitrary")),
        name="in_proj",
    )(h, w, cos_t, sin_t)


def _gelu_tanh(x):
    return 0.5 * x * (1.0 + jnp.tanh(0.7978845608028654 * (x + 0.044715 * x * x * x)))


def _compress_kernel(x_ref, pek_ref, w1k_ref, w2k_ref, pev_ref, w1v_ref, w2v_ref, c_ref, s_ref, kc_ref, vc_ref,
                     xf_ref):
    n_slabs = xf_ref.shape[0]
    nc = x_ref.shape[0] // CMP_STRIDE
    half = CMP_STRIDE * HEAD_DIM
    for c in range(n_slabs):
        xf_ref[c] = x_ref[:, c * HEAD_DIM:(c + 1) * HEAD_DIM].astype(F32)
    for which, (pe_ref, w1_ref, w2_ref, out_ref) in enumerate(
            ((pek_ref, w1k_ref, w2k_ref, kc_ref), (pev_ref, w1v_ref, w2v_ref, vc_ref))):
        for g in range(NSA_GROUPS):
            acc_a = jnp.zeros((nc, CMP_HIDDEN), F32)
            acc_b = jnp.zeros((nc, CMP_HIDDEN), F32)
            for l in range(CMP_STRIDE):
                xl = xf_ref[which * NSA_GROUPS + g, pl.ds(l, nc, stride=CMP_STRIDE), :]
                xa =(xl + pe_ref[l:l + 1, :]).astype(BF16)
                xb = (xl + pe_ref[CMP_STRIDE + l:CMP_STRIDE + l + 1, :]).astype(BF16)
                acc_a = acc_a + _dot(xa, w1_ref[l * HEAD_DIM:(l + 1) * HEAD_DIM, :])
                acc_b = acc_b + _dot(xb, w1_ref[half + l * HEAD_DIM:half + (l + 1) * HEAD_DIM, :])
            hid = _gelu_tanh(acc_a + pltpu.roll(acc_b, nc - 1, 0))
            out = _dot(hid.astype(BF16), w2_ref[...])
            if which == 0:
                lane = lax.broadcasted_iota(I32, out.shape, 1)
                out = _rope_tile(out, c_ref[...], s_ref[...], lane)
            out_ref[g] = out.astype(BF16)


def _compress(proj3, pek, w1k, w2k, pev, w1v, w2v, cos_c, sin_c):
    B, S, _ = proj3.shape
    nc = S // CMP_STRIDE
    W = 2 * KV_WIDTH
    full = lambda a: pl.BlockSpec(a.shape, lambda b: (0,) * a.ndim)
    out_sds = jax.ShapeDtypeStruct((B, NSA_GROUPS, nc, HEAD_DIM), BF16)
    out_spec = pl.BlockSpec((None, NSA_GROUPS, nc, HEAD_DIM), lambda b: (b, 0, 0, 0))
    return pl.pallas_call(
        _compress_kernel,
        grid=(B,),
        in_specs=[pl.BlockSpec((None, S, W), lambda b: (b, 0, OFF_KCMP // W)),
                  full(pek), full(w1k), full(w2k), full(pev), full(w1v), full(w2v),
                  pl.BlockSpec((None, nc, HEAD_DIM), lambda b: (b, 0, 0)),
                  pl.BlockSpec((None, nc, HEAD_DIM), lambda b: (b, 0, 0))],
        out_specs=[out_spec, out_spec],
        out_shape=[out_sds, out_sds],
        scratch_shapes=[pltpu.VMEM((W // HEAD_DIM, S, HEAD_DIM), F32)],
        compiler_params=_cparams(("arbitrary",)),
        name="compress",
    )(proj3, pek, w1k, w2k, pev, w1v, w2v, cos_c, sin_c)


def _transpose_into(src_ref, col0, dst_ref, n_rows):
    def body(b, _):
        r0 = pl.multiple_of(b * HEAD_DIM, HEAD_DIM)
        blk = src_ref[pl.ds(r0, HEAD_DIM), col0:col0 + HEAD_DIM].astype(F32)
        dst_ref[:, pl.ds(r0, HEAD_DIM)] = blk.T.astype(BF16)
        return 0
    lax.fori_loop(0, n_rows // HEAD_DIM, body, 0)


def _topk_mask(score, k):
    n, L = score.shape
    sub = 8
    assert n % sub == 0
    groups = n // sub
    segs = [score[sub * g:sub * (g + 1), :] for g in range(groups)]
    sub_idx = lax.broadcasted_iota(I32, (sub, L), 0)
    later = [jnp.where(sub_idx > r, 1.0, 0.0) for r in range(sub)]
    ranks = [jnp.zeros((sub, L), F32) for _ in range(groups)]
    for i in range(n):
        gi, ri = divmod(i, sub)
        row = score[i:i + 1, :]
        for g in range(groups):
            if g > gi:
                inc = jnp.where(row >= segs[g], 1.0, 0.0)
            elif g < gi:
                inc = jnp.where(row > segs[g], 1.0, 0.0)
            else:
                inc = jnp.where(row > segs[g], 1.0, jnp.where(row == segs[g], later[ri], 0.0))
            ranks[g] = ranks[g] + inc
    return jnp.concatenate(ranks, axis=0) < k


def _bias_rows_to_cols(sel, lanes_out=HEAD_DIM):
    n, L = sel.shape
    bias_t = jnp.where(sel, 0.0, NEG_INF)
    if n < lanes_out:
        bias_t = jnp.concatenate([bias_t, jnp.zeros((lanes_out - n, L), F32)], axis=0)
    return bias_t.T.astype(BF16)


def _softmax_cols(s, live=None):
    m = jnp.max(s, axis=0, keepdims=True)
    p = jnp.exp2(s - m)
    inv = 1.0 / jnp.sum(p, axis=0, keepdims=True)
    if live is not None:
        inv = jnp.where(live, inv, 0.0)
    return p * inv


def _online_update(s, vt, carry):
    m, l, acc = carry
    m_new = jnp.maximum(m, jnp.max(s, axis=0, keepdims=True))
    alpha = jnp.exp2(m - m_new)
    p = jnp.exp2(s - m_new)
    l = alpha * l + jnp.sum(p, axis=0, keepdims=True)
    acc = alpha * acc + _dot(vt, p.astype(BF16))
    return m_new, l, acc


def _nsa_kernel(q_ref, kc_ref, vc_ref, ks_ref, vs_ref, kw_ref, vw_ref, gt_ref, c2st_ref, e_ref, o_ref,
                ksa_ref, vst_ref, vwt_ref, vct_ref, *, seq):
    tq = NSA_TQ
    U = NSA_SUB
    R = NSA_REP
    i = pl.program_id(2)
    base = i * (U * tq)
    nc = kc_ref.shape[0]
    n_slc = c2st_ref.shape[0]
    lanes = R * tq
    band = WINDOW + tq
    tile_r = lambda a: jnp.concatenate([a] * R, axis=1)
    rowi = lambda n: lax.broadcasted_iota(I32, (n, tq), 0)

    @pl.when(i == 0)
    def _():
        _transpose_into(vs_ref, 0, vst_ref, seq)
        _transpose_into(vw_ref, 0, vwt_ref, seq)
        _transpose_into(vc_ref, 0, vct_ref, nc)
        ksa_ref[:, 0:HEAD_DIM] = ks_ref[...]
        ksa_ref[:, HEAD_DIM:2 * HEAD_DIM] = e_ref[...]

    subs = range(U)
    heads = [[q_ref[u * tq:(u + 1) * tq, r * HEAD_DIM:(r + 1) * HEAD_DIM] for r in range(R)] for u in subs]
    q4 = [jnp.concatenate(heads[u], axis=0) for u in subs]
    tcols = [lambda n, q0=base + u * tq: q0 + lax.broadcasted_iota(I32, (n, tq), 1) for u in subs]
    starts = [pl.multiple_of(jnp.maximum(base + u * tq - WINDOW, 0), HEAD_DIM) for u in subs]

    s_c = [_dot_nt(kc_ref[...], q4[u]) for u in subs]
    s_w = [_dot_nt(kw_ref[pl.ds(starts[u], band), :], q4[u]) for u in subs]

    p_c = []
    for u in subs:
        bias_c = jnp.where(rowi(nc) * CMP_STRIDE + (CMP_BLOCK - 1) <= tcols[u](nc), 0.0, NEG_INF)
        live_c = tcols[u](1) >= CMP_BLOCK - 1
        p_c.append(_softmax_cols(s_c[u] + tile_r(bias_c), tile_r(live_c)))
    o_cmp = [_dot(vct_ref[...], p_c[u].astype(BF16)) for u in subs]

    imp = []
    for u in subs:
        p_sum = p_c[u][:, 0:tq]
        for r in range(1, R):
            p_sum = p_sum + p_c[u][:, r * tq:(r + 1) * tq]
        p_hi, p_lo = _split_hi_lo(p_sum)
        imp.append(_dot(c2st_ref[...], p_hi) + _dot(c2st_ref[...], p_lo))

    p_w = []
    for u in subs:
        dpos = tcols[u](band) - (starts[u] + rowi(band))
        bias_w = jnp.where((dpos >= 0) & (dpos < WINDOW), 0.0, NEG_INF)
        p_w.append(_softmax_cols(s_w[u] + tile_r(bias_w)))
    o_win = [_dot(vwt_ref[:, pl.ds(starts[u], band)], p_w[u].astype(BF16)) for u in subs]

    q4a = []
    for u in subs:
        jj = rowi(n_slc)
        cur = tcols[u](n_slc) // SEL_BLOCK
        valid = jj <= cur
        forced = (jj == 0) | (jj == cur) | (jj == cur - 1)
        score = jnp.where(forced, SEL_FORCED, jnp.where(valid, imp[u], -SEL_FORCED))
        sel = _topk_mask(score, min(SEL_TOPN, n_slc)) & valid
        bias = _bias_rows_to_cols(sel)
        q4a.append(jnp.concatenate([jnp.concatenate([h, bias], axis=1) for h in heads[u]], axis=0))

    def sel_body(kt, carry):
        k0 = pl.multiple_of(kt * SEL_KT, SEL_KT)
        s = [_dot_nt(ksa_ref[pl.ds(k0, SEL_KT), :], q4a[u]) for u in range(U)]
        return tuple(_online_update(s[u], vst_ref[:, pl.ds(k0, SEL_KT)], carry[u]) for u in range(U))

    n_full = base // SEL_KT
    init = (jnp.full((1, lanes), NEG_INF, F32), jnp.zeros((1, lanes), F32), jnp.zeros((HEAD_DIM, lanes), F32))
    carry = lax.fori_loop(0, n_full, sel_body, (init,) * U)
    k0 = pl.multiple_of(n_full * SEL_KT, SEL_KT)
    gt = gt_ref[...]
    assert U * tq == SEL_KT
    n_diag = [(u + 1) * tq for u in subs]
    s_d = [_dot_nt(ksa_ref[pl.ds(k0, n_diag[u]), :], q4a[u]) for u in subs]
    for u in subs:
        causal = jnp.where(k0 + rowi(n_diag[u]) <= tcols[u](n_diag[u]), 0.0, NEG_INF)
        _, l_s, acc_s = _online_update(s_d[u] + tile_r(causal), vst_ref[:, pl.ds(k0, n_diag[u])], carry[u])
        o_slc = acc_s * (1.0 / l_s)

        g = gt[:, u * tq:(u + 1) * tq]
        for r in range(R):
            sl = slice(r * tq, (r + 1) * tq)
            o_r = (g[r:r + 1, :] * o_cmp[u][:, sl] + g[R + r:R + r + 1, :] * o_slc[:, sl]
                   + g[2 * R + r:2 * R + r + 1, :] * o_win[u][:, sl])
            o_ref[u * tq:(u + 1) * tq, r * HEAD_DIM:(r + 1) * HEAD_DIM] = o_r.T.astype(BF16)


def _block_onehot(seq, block):
    assert seq // block <= HEAD_DIM
    e = np.zeros((seq, HEAD_DIM), np.float32)
    e[np.arange(seq), np.arange(seq) // block] = 1.0
    return jnp.asarray(e, BF16)


def _nsa_attention(proj3, kc, vc, gates_t, c2st):
    B, S, _ = proj3.shape
    G = NSA_GROUPS
    nc = kc.shape[2]
    tq = NSA_TQ * NSA_SUB
    assert S % tq == 0 and SEL_KT % tq == 0
    nq = S // tq
    hb = HEAD_DIM
    gw = NSA_REP * HEAD_DIM
    e_sel = _block_onehot(S, SEL_BLOCK)
    seq_spec = lambda off: pl.BlockSpec((None, S, hb), lambda b, g, i: (b, 0, off // hb + g))
    cmp_spec = pl.BlockSpec((None, None, nc, hb), lambda b, g, i: (b, g, 0, 0))
    return pl.pallas_call(
        functools.partial(_nsa_kernel, seq=S),
        grid=(B, G, nq),
        in_specs=[pl.BlockSpec((None, tq, gw), lambda b, g, i: (b, i, OFF_QA // gw + g)),
                  cmp_spec, cmp_spec,
                  seq_spec(OFF_KSLC), seq_spec(OFF_VSLC), seq_spec(OFF_KWIN), seq_spec(OFF_VWIN),
                  pl.BlockSpec((GATE_ROWS, tq), lambda b, g, i: (g, b * nq + i)),
                  pl.BlockSpec(c2st.shape, lambda b, g, i: (0, 0)),
                  pl.BlockSpec(e_sel.shape, lambda b, g, i: (0, 0))],
        out_specs=pl.BlockSpec((None, tq, gw), lambda b, g, i: (b, i, g)),
        out_shape=jax.ShapeDtypeStruct((B, S, NSA_WIDTH), BF16),
        scratch_shapes=[pltpu.VMEM((S, 2 * hb), BF16), pltpu.VMEM((hb, S), BF16), pltpu.VMEM((hb, S), BF16),
                        pltpu.VMEM((hb, nc), BF16)],
        compiler_params=_cparams(("arbitrary", "arbitrary", "arbitrary")),
        name="nsa_attn",
    )(proj3, kc, vc, proj3, proj3, proj3, proj3, gates_t, c2st, e_sel)


def _moba_kernel(q_ref, k_ref, v_ref, e_ref, o_ref, ka_ref, vt_ref, kmh_ref, kml_ref, *, seq):
    blk = MOBA_BLOCK
    HP = MOBA_HEADS_PER_STEP
    i = pl.program_id(2)
    nb = seq // blk
    nbp = kmh_ref.shape[1]
    hsl = lambda h: slice(h * HEAD_DIM, (h + 1) * HEAD_DIM)

    @pl.when(i == 0)
    def _():
        for h in range(HP):
            _transpose_into(v_ref, h * HEAD_DIM, vt_ref.at[h], seq)
            ka_ref[h, :, 0:HEAD_DIM] = k_ref[:, hsl(h)]
            ka_ref[h, :, HEAD_DIM:2 * HEAD_DIM] = e_ref[...]
            means = [jnp.sum(k_ref[n * blk:(n + 1) * blk, hsl(h)].astype(F32), axis=0, keepdims=True) * (1.0 / blk)
                     for n in range(nb)]
            if nbp > nb:
                means.append(jnp.zeros((nbp - nb, HEAD_DIM), F32))
            hi, lo = _split_hi_lo(jnp.concatenate(means, axis=0))
            kmh_ref[h] = hi
            kml_ref[h] = lo

    own0 = pl.multiple_of(i * blk, blk)
    past = lax.broadcasted_iota(I32, (nbp, blk), 0) < i
    causal = jnp.where(lax.broadcasted_iota(I32, (blk, blk), 0) <= lax.broadcasted_iota(I32, (blk, blk), 1), 0.0, NEG_INF)
    hs = range(HP)
    q = [q_ref[:, hsl(h)] for h in hs]
    gate = [_dot_nt(kmh_ref[h], q[h]) + _dot_nt(kml_ref[h], q[h]) for h in hs]
    s_own = [_dot_nt(k_ref[pl.ds(own0, blk), hsl(h)], q[h]) for h in hs]
    m_own = [jnp.max(s_own[h] + causal, axis=0, keepdims=True) for h in hs]
    p_own = [jnp.exp2(s_own[h] + causal - m_own[h]) for h in hs]
    carry = [(m_own[h], jnp.sum(p_own[h], axis=0, keepdims=True),
              _dot(vt_ref[h, :, pl.ds(own0, blk)], p_own[h].astype(BF16))) for h in hs]
    qa = []
    for h in hs:
        sel = _topk_mask(jnp.where(past, gate[h], -SEL_FORCED), min(MOBA_TOPK, nb)) & past
        qa.append(jnp.concatenate([q[h], _bias_rows_to_cols(sel)], axis=1))

    per_iter = MOBA_BLOCKS_PER_ITER
    kt = per_iter * blk

    def body(j, carry):
        k0 = pl.multiple_of(j * kt, kt)
        s = [_dot_nt(ka_ref[h, pl.ds(k0, kt), :], qa[h]) for h in range(HP)]
        return tuple(_online_update(s[h], vt_ref[h, :, pl.ds(k0, kt)], carry[h]) for h in range(HP))

    carry = lax.fori_loop(0, (i + per_iter - 1) // per_iter, body, tuple(carry))
    for h in range(HP):
        _, l, acc = carry[h]
        o_ref[:, hsl(h)] = (acc * (1.0 / l)).T.astype(BF16)


def _moba_attention(proj3):
    B, S, _ = proj3.shape
    HP = MOBA_HEADS_PER_STEP
    blk = MOBA_BLOCK
    hb = HEAD_DIM
    assert (S // blk) % MOBA_BLOCKS_PER_ITER == 0 and MOBA_HEADS % HP == 0
    nbp = max(8, -(-(S // blk) // 8) * 8)
    e_blk = _block_onehot(S, blk)
    wide = HP * hb
    seq_spec = lambda off: pl.BlockSpec((None, S, wide), lambda b, h, i: (b, 0, off // wide + h))
    return pl.pallas_call(
        functools.partial(_moba_kernel, seq=S),
        grid=(B, MOBA_HEADS // HP, S // blk),
        in_specs=[pl.BlockSpec((None, blk, wide), lambda b, h, i: (b, i, OFF_QB // wide + h)),
                  seq_spec(OFF_KB), seq_spec(OFF_VB),
                  pl.BlockSpec(e_blk.shape, lambda b, h, i: (0, 0))],
        out_specs=pl.BlockSpec((None, blk, wide), lambda b, h, i: (b, i, h)),
        out_shape=jax.ShapeDtypeStruct((B, S, MOBA_WIDTH), BF16),
        scratch_shapes=[pltpu.VMEM((HP, S, 2 * hb), BF16), pltpu.VMEM((HP, hb, S), BF16),
                        pltpu.VMEM((HP, nbp, hb), BF16), pltpu.VMEM((HP, nbp, hb), BF16)],
        compiler_params=_cparams(("arbitrary", "arbitrary", "arbitrary")),
        name="moba_attn",
    )(proj3, proj3, proj3, e_blk)


def _merge_proj_kernel(oa_ref, ob_ref, wa_ref, wb_ref, ga_ref, gb_ref, o_ref):
    tm = oa_ref.shape[0]
    chunk = min(ROW_CHUNK, tm)
    for r0 in range(0, tm, chunk):
        rows = slice(r0, r0 + chunk)
        a = _dot(oa_ref[rows, :], wa_ref[...])
        b = _dot(ob_ref[rows, :], wb_ref[...])
        o_ref[rows, :] = (ga_ref[rows, :].astype(F32) * a + gb_ref[rows, :].astype(F32) * b).astype(BF16)


def _merge_proj(oa, ob, wa, wb, proj, tm):
    T = oa.shape[0]
    D = wa.shape[1]
    tn = min(COL_TILE, D)
    g0 = PLAIN_END // tn
    return pl.pallas_call(
        _merge_proj_kernel,
        grid=(T // tm, D // tn),
        in_specs=[pl.BlockSpec((tm, NSA_WIDTH), lambda i, j: (i, 0)),
                  pl.BlockSpec((tm, MOBA_WIDTH), lambda i, j: (i, 0)),
                  pl.BlockSpec((NSA_WIDTH, tn), lambda i, j: (0, j)),
                  pl.BlockSpec((MOBA_WIDTH, tn), lambda i, j: (0, j)),
                  pl.BlockSpec((tm, tn), lambda i, j: (i, g0 + j)),
                  pl.BlockSpec((tm, tn), lambda i, j: (i, g0 + D // tn + j))],
        out_specs=pl.BlockSpec((tm, tn), lambda i, j: (i, j)),
        out_shape=jax.ShapeDtypeStruct((T, D), BF16),
        compiler_params=_cparams(("arbitrary", "arbitrary")),
        name="merge_proj",
    )(oa, ob, wa, wb, proj, proj)


def _out_router_kernel(x_ref, mix_ref, wo_ref, g_ref, wrh_ref, wrl_ref, br_ref, x1_ref, h2_ref, te_ref, tw_ref):
    tm = x_ref.shape[0]
    E = wrh_ref.shape[0]
    x1 = x_ref[...] + _dot(mix_ref[...], wo_ref[...])
    x1_ref[...] = x1
    h2 = x1 * lax.rsqrt(jnp.mean(x1 * x1, axis=-1, keepdims=True) + NORM_EPS) * g_ref[...]
    _store_token_slabs(h2_ref, _pack_bf16_pairs(h2))
    h_hi, h_lo = _split_hi_lo(h2)
    logits = (_dot_nt(wrh_ref[...], h_hi) + _dot_nt(wrh_ref[...], h_lo) + _dot_nt(wrl_ref[...], h_hi)
              + br_ref[...])
    e_idx = lax.broadcasted_iota(I32, (E, tm), 0)
    picks = _topk_rows(logits, e_idx, E, TOP_K)
    vals = [p[0] for p in picks]
    exps = [jnp.exp(v - vals[0]) for v in vals]
    inv = 1.0 / (exps[0] + exps[1] + exps[2] + exps[3])
    te_ref[...] = jnp.concatenate([p[1] for p in picks] + [jnp.zeros((8 - TOP_K, tm), I32)], axis=0)
    w_t = jnp.concatenate([e * inv for e in exps] + [jnp.zeros((HEAD_DIM - TOP_K, tm), F32)], axis=0)
    tw_ref[...] = w_t.T


def _out_router(x2, mix, wo, g, wrh, wrl, br, tm):
    T, D = x2.shape
    E = wrh.shape[0]
    slab = _slab_rows(D // 2)
    full = lambda a: pl.BlockSpec(a.shape, lambda i: (0,) * a.ndim)
    return pl.pallas_call(
        _out_router_kernel,
        grid=(T // tm,),
        in_specs=[pl.BlockSpec((tm, D), lambda i: (i, 0)), pl.BlockSpec((tm, D), lambda i: (i, 0)),
                  full(wo), full(g), full(wrh), full(wrl), full(br)],
        out_specs=[pl.BlockSpec((tm, D), lambda i: (i, 0)), pl.BlockSpec((tm * slab, HEAD_DIM), lambda i: (i, 0)),
                   pl.BlockSpec((8, tm), lambda i: (0, i)), pl.BlockSpec((tm, HEAD_DIM), lambda i: (i, 0))],
        out_shape=[jax.ShapeDtypeStruct((T, D), F32), jax.ShapeDtypeStruct((T * slab, HEAD_DIM), jnp.uint32),
                   jax.ShapeDtypeStruct((8, T), I32), jax.ShapeDtypeStruct((T, HEAD_DIM), F32)],
        compiler_params=_cparams(("arbitrary",)),
        name="out_router",
    )(x2, mix, wo, g, wrh, wrl, br)


def _route_kernel(te_ref, dest_ref, blke_ref, padlo_ref, padhi_ref, run_ref, pstart_ref):
    ph = pl.program_id(0)
    i = pl.program_id(1)
    E = run_ref.shape[0]
    tm = te_ref.shape[1]
    nbp = blke_ref.shape[1]
    te = te_ref[...]
    e_idx = lax.broadcasted_iota(I32, (E, tm), 0)
    ohs = [te[k:k + 1, :] == e_idx for k in range(TOP_K)]
    oh = jnp.where(ohs[0] | ohs[1] | ohs[2] | ohs[3], 1.0, 0.0)
    tile_cnt = jnp.sum(oh, axis=1, keepdims=True)

    @pl.when((ph == 0) & (i == 0))
    def _():
        run_ref[...] = jnp.zeros(run_ref.shape, F32)

    @pl.when(ph == 0)
    def _():
        run_ref[...] = run_ref[...] + tile_cnt
        dest_ref[...] = jnp.zeros(dest_ref.shape, I32)

    @pl.when((ph == 1) & (i == 0))
    def _():
        counts = run_ref[...]
        padded = jnp.floor((counts + (ROW_BLOCK - 1)) * (1.0 / ROW_BLOCK)) * ROW_BLOCK
        row = lax.broadcasted_iota(I32, counts.shape, 0)
        incl = padded
        sh = 1
        while sh < E:
            incl = incl + jnp.where(row >= sh, pltpu.roll(incl, sh, 0), 0.0)
            sh *= 2
        pstart_ref[...] = incl - padded
        padlo_ref[...] = (incl - padded + counts).astype(I32)
        padhi_ref[...] = incl.astype(I32)
        blk_start = (lax.broadcasted_iota(I32, (E, nbp), 1) * ROW_BLOCK).astype(F32)
        blke_ref[...] = jnp.sum(jnp.where(incl[:, 0:1] <= blk_start, 1, 0), axis=0, keepdims=True).astype(I32)
        run_ref[...] = jnp.zeros(run_ref.shape, F32)

    @pl.when(ph == 1)
    def _():
        upper = jnp.where(lax.broadcasted_iota(I32, (tm, tm), 0) < lax.broadcasted_iota(I32, (tm, tm), 1), 1.0, 0.0)
        before = _dot(oh.astype(BF16), upper.astype(BF16))
        val = before + pstart_ref[:, 0:1] + run_ref[:, 0:1]
        rows = [jnp.sum(jnp.where(ohs[k], val, 0.0), axis=0, keepdims=True) for k in range(TOP_K)]
        dest_ref[...] = jnp.concatenate(rows + [jnp.zeros((8 - TOP_K, tm), F32)], axis=0).astype(I32)
        run_ref[...] = run_ref[...] + tile_cnt


def _route(top_e, n_experts, n_blk_pad, tm):
    T = top_e.shape[1]
    return pl.pallas_call(
        _route_kernel,
        grid=(2, T // tm),
        in_specs=[pl.BlockSpec((8, tm), lambda ph, i: (0, i))],
        out_specs=[pl.BlockSpec((8, tm), lambda ph, i: (0, i * ph)),
                   pl.BlockSpec((1, n_blk_pad), lambda ph, i: (0, 0)),
                   pl.BlockSpec((n_experts, HEAD_DIM), lambda ph, i: (0, 0)),
                   pl.BlockSpec((n_experts, HEAD_DIM), lambda ph, i: (0, 0))],
        out_shape=[jax.ShapeDtypeStruct((8, T), I32), jax.ShapeDtypeStruct((1, n_blk_pad), I32),
                   jax.ShapeDtypeStruct((n_experts, HEAD_DIM), I32), jax.ShapeDtypeStruct((n_experts, HEAD_DIM), I32)],
        scratch_shapes=[pltpu.VMEM((n_experts, HEAD_DIM), F32), pltpu.VMEM((n_experts, HEAD_DIM), F32)],
        compiler_params=_cparams(("arbitrary", "arbitrary")),
        name="route",
    )(top_e)


def _slab_copy(src_ref, src_tok, dst_ref, dst_tok, sem):
    src = src_ref.at[pl.ds(pl.multiple_of(src_tok * SLAB, SLAB), SLAB), :]
    dst = dst_ref.at[pl.ds(pl.multiple_of(dst_tok * SLAB, SLAB), SLAB), :]
    return pltpu.make_async_copy(src, dst, sem)


def _wait_slabs(slabs_hbm_ref, n_tokens, sem):
    assert n_tokens % WAIT_TOKENS == 0
    chunk = slabs_hbm_ref.at[pl.ds(0, WAIT_TOKENS * SLAB), :]
    for _ in range(n_tokens // WAIT_TOKENS):
        pltpu.make_async_copy(chunk, chunk, sem).wait()


def _dispatch_kernel(pad_ref, dest_ref, h_ref, xpad_ref, zero_ref, sem, zsem):
    tm = h_ref.shape[0] // SLAB

    @pl.when(pl.program_id(0) == 0)
    def _():
        zero_ref[...] = jnp.zeros(zero_ref.shape, zero_ref.dtype)

        def per_expert(e, _):
            lo, hi = pad_ref[0, e], pad_ref[1, e]

            def issue(r, _):
                _slab_copy(zero_ref, 0, xpad_ref, r, zsem).start()
                return 0
            lax.fori_loop(lo, hi, issue, 0)

            def drain(r, _):
                _slab_copy(zero_ref, 0, xpad_ref, 0, zsem).wait()
                return 0
            lax.fori_loop(lo, hi, drain, 0)
            return 0
        lax.fori_loop(0, pad_ref.shape[1], per_expert, 0)

    def issue(t, _):
        for k in range(TOP_K):
            _slab_copy(h_ref, t, xpad_ref, dest_ref[t * TOP_K + k], sem).start(priority=k % 2)
        return 0
    lax.fori_loop(0, tm, issue, 0, unroll=2)

    _wait_slabs(xpad_ref, TOP_K * tm, sem)


def _dispatch(pads, dest, h2, n_rows, tm):
    T, D = h2.shape[0] // SLAB, h2.shape[1]
    n_rows = n_rows * SLAB
    tm_rows = tm * SLAB
    grid_spec = pltpu.PrefetchScalarGridSpec(
        num_scalar_prefetch=1,
        grid=(T // tm,),
        in_specs=[pl.BlockSpec((tm * TOP_K,), lambda i, pads: (i,), memory_space=pltpu.SMEM),
                  pl.BlockSpec((tm_rows, D), lambda i, pads: (i, 0))],
        out_specs=pl.BlockSpec(memory_space=pl.ANY),
        scratch_shapes=[pltpu.VMEM((8, D), h2.dtype), pltpu.SemaphoreType.DMA(()), pltpu.SemaphoreType.DMA(())])
    return pl.pallas_call(
        _dispatch_kernel,
        grid_spec=grid_spec,
        out_shape=jax.ShapeDtypeStruct((n_rows, D), h2.dtype),
        compiler_params=_cparams(("arbitrary",)),
        name="dispatch",
    )(pads, dest, h2)


def _next_expert(blk_e, n_experts):
    nxt = jnp.min(jnp.where(blk_e[None, :] > blk_e[:, None], blk_e[None, :], n_experts), axis=1)
    return jnp.where(nxt < n_experts, nxt, -1).astype(I32)


def _stream_expert_weights(be_ref, nxt_ref, w_hbm, col_offsets, wbuf_ref, wb_refs, sem, cnt_ref, n_experts):
    n, m = pl.program_id(0), pl.program_id(1)
    tn = wb_refs[0].shape[1]
    e = be_ref[m]
    live = e < n_experts
    first = live & ((m == 0) | (be_ref[jnp.maximum(m - 1, 0)] != e))

    def copies(ex, col_tile, slot):
        c0 = pl.multiple_of(col_tile * tn, tn)
        return [pltpu.make_async_copy(w_hbm.at[ex, :, pl.ds(off + c0, tn)], wbuf_ref.at[slot, g], sem.at[slot, g])
                for g, off in enumerate(col_offsets)]

    @pl.when((n == 0) & (m == 0))
    def _():
        cnt_ref[0] = 0
        for c in copies(e, 0, 0):
            c.start()

    @pl.when(first)
    def _():
        slot = cnt_ref[0] % 2
        for c in copies(e, n, slot):
            c.wait()
        nxt = nxt_ref[m]

        @pl.when(nxt >= 0)
        def _():
            for c in copies(nxt, n, 1 - slot):
                c.start()

        @pl.when((nxt < 0) & (n + 1 < pl.num_programs(0)))
        def _():
            for c in copies(be_ref[0], n + 1, 1 - slot):
                c.start()

        for g, wb_ref in enumerate(wb_refs):
            wb_ref[...] = wbuf_ref[slot, g].astype(BF16)
        cnt_ref[0] = cnt_ref[0] + 1
    return live


def _expert_up_kernel(be_ref, nxt_ref, x_ref, w_hbm, bg_ref, bl_ref, act_ref, wbuf_ref, wgb_ref, wlb_ref, sem, cnt_ref,
                      *, n_experts, d_ff):
    live = _stream_expert_weights(be_ref, nxt_ref, w_hbm, (0, d_ff), wbuf_ref, (wgb_ref, wlb_ref), sem, cnt_ref,
                                  n_experts)

    @pl.when(live)
    def _():
        x_lo, x_hi = _unpack_bf16_pairs(_load_token_slabs(x_ref, ROW_BLOCK, SLAB))
        x = jnp.concatenate([x_lo.astype(BF16), x_hi.astype(BF16)], axis=1)
        gate =jnp.minimum(_dot(x, wgb_ref[...]) + bg_ref[...], SWIGLU_LIMIT)
        lin = jnp.clip(_dot(x, wlb_ref[...]) + bl_ref[...], -SWIGLU_LIMIT, SWIGLU_LIMIT)
        act_ref[...] = (gate * _sigmoid(SWIGLU_ALPHA * gate) * (lin + 1.0)).astype(BF16)

    @pl.when(jnp.logical_not(live))
    def _():
        act_ref[...] = jnp.zeros(act_ref.shape, BF16)


def _expert_up(blk_e, nxt_e, x_pad, w_gu, b_gu, n_blk, tn):
    P = x_pad.shape[0] // SLAB
    E, D = w_gu.shape[0], w_gu.shape[1]
    F = w_gu.shape[2] // 2
    nf = F // tn
    ex = lambda be, m: jnp.minimum(be[m], E - 1)
    grid_spec = pltpu.PrefetchScalarGridSpec(
        num_scalar_prefetch=2,
        grid=(nf, n_blk),
        in_specs=[pl.BlockSpec((ROW_BLOCK * _slab_rows(D // 2), HEAD_DIM), lambda n, m, be, nx: (m, 0)),
                  pl.BlockSpec(memory_space=pl.ANY),
                  pl.BlockSpec((None, 1, tn), lambda n, m, be, nx: (ex(be, m), 0, n)),
                  pl.BlockSpec((None, 1, tn), lambda n, m, be, nx: (ex(be, m), 0, nf + n))],
        out_specs=pl.BlockSpec((ROW_BLOCK, tn), lambda n, m, be, nx: (m, n)),
        scratch_shapes=[pltpu.VMEM((2, 2, D, tn), F32), pltpu.VMEM((D, tn), BF16), pltpu.VMEM((D, tn), BF16),
                        pltpu.SemaphoreType.DMA((2, 2)), pltpu.SMEM((1,), I32)])
    return pl.pallas_call(
        functools.partial(_expert_up_kernel, n_experts=E, d_ff=F),
        grid_spec=grid_spec,
        out_shape=jax.ShapeDtypeStruct((P, F), BF16),
        compiler_params=_cparams(("arbitrary", "arbitrary")),
        name="expert_up",
    )(blk_e, nxt_e, x_pad, w_gu, b_gu, b_gu)


def _expert_down_kernel(be_ref, nxt_ref, a_ref, w_hbm, b_ref, y_ref, wbuf_ref, wb_ref, sem, cnt_ref, *, n_experts):
    live = _stream_expert_weights(be_ref, nxt_ref, w_hbm, (0,), wbuf_ref, (wb_ref,), sem, cnt_ref, n_experts)

    @pl.when(live)
    def _():
        _store_token_slabs(y_ref, _pack_bf16_pairs(_dot(a_ref[...], wb_ref[...]) + b_ref[...]))

    @pl.when(jnp.logical_not(live))
    def _():
        y_ref[...] = jnp.zeros(y_ref.shape, y_ref.dtype)


def _expert_down(blk_e, nxt_e, act, w_d, b_d, n_blk):
    P, F = act.shape
    E, _, D = w_d.shape
    tn = D
    ex = lambda be, m: jnp.minimum(be[m], E - 1)
    grid_spec = pltpu.PrefetchScalarGridSpec(
        num_scalar_prefetch=2,
        grid=(1, n_blk),
        in_specs=[pl.BlockSpec((ROW_BLOCK, F), lambda n, m, be, nx: (m, 0)),
                  pl.BlockSpec(memory_space=pl.ANY),
                  pl.BlockSpec((None, 1, tn), lambda n, m, be, nx: (ex(be, m), 0, n))],
        out_specs=pl.BlockSpec((ROW_BLOCK * _slab_rows(D // 2), HEAD_DIM), lambda n, m, be, nx: (m, 0)),
        scratch_shapes=[pltpu.VMEM((2, 1, F, tn), F32), pltpu.VMEM((F, tn), BF16),
                        pltpu.SemaphoreType.DMA((2, 1)), pltpu.SMEM((1,), I32)])
    return pl.pallas_call(
        functools.partial(_expert_down_kernel, n_experts=E),
        grid_spec=grid_spec,
        out_shape=jax.ShapeDtypeStruct((P * SLAB, HEAD_DIM), jnp.uint32),
        compiler_params=_cparams(("arbitrary", "arbitrary")),
        name="expert_down",
    )(blk_e, nxt_e, act, w_d, b_d)


def _combine_kernel(dest_ref, dest_next_ref, x1_ref, tw_ref, g_ref, ypad_ref, o_ref, ybuf_ref, sem):
    tm = x1_ref.shape[0]
    i = pl.program_id(0)
    slot = i % 2

    def gather(d_ref, s):
        def issue(t, _):
            for k in range(TOP_K):
                _slab_copy(ypad_ref, d_ref[t * TOP_K + k], ybuf_ref.at[s, k], t, sem.at[s]).start(priority=k % 2)
            return 0
        lax.fori_loop(0, tm, issue, 0, unroll=2)

    @pl.when(i == 0)
    def _():
        gather(dest_ref, 0)

    @pl.when(i + 1 < pl.num_programs(0))
    def _():
        gather(dest_next_ref, 1 - slot)

    _wait_slabs(ypad_ref, TOP_K * tm, sem.at[slot])

    tw = tw_ref[...]
    half = SLAB * HEAD_DIM
    acc_lo = x1_ref[:, :half]
    acc_hi = x1_ref[:, half:]
    for k in range(TOP_K):
        y_lo, y_hi = _unpack_bf16_pairs(_load_token_slabs(ybuf_ref.at[slot, k], tm, SLAB))
        acc_lo = acc_lo + tw[:, k:k + 1] * y_lo
        acc_hi = acc_hi + tw[:, k:k + 1] * y_hi
    acc = jnp.concatenate([acc_lo, acc_hi], axis=1)
    o_ref[...] = acc * lax.rsqrt(jnp.mean(acc * acc, axis=-1, keepdims=True) + NORM_EPS) * g_ref[...]


def _combine(dest, x1, top_w, g, y_pad, tm):
    T, D = x1.shape
    n = T // tm
    return pl.pallas_call(
        _combine_kernel,
        grid=(n,),
        in_specs=[pl.BlockSpec((tm * TOP_K,), lambda i: (i,), memory_space=pltpu.SMEM),
                  pl.BlockSpec((tm * TOP_K,), lambda i: (jnp.minimum(i + 1, n - 1),), memory_space=pltpu.SMEM),
                  pl.BlockSpec((tm, D), lambda i: (i, 0)),
                  pl.BlockSpec((tm, HEAD_DIM), lambda i: (i, 0)),
                  pl.BlockSpec((1, D), lambda i: (0, 0)),
                  pl.BlockSpec(memory_space=pl.ANY)],
        out_specs=pl.BlockSpec((tm, D), lambda i: (i, 0)),
        out_shape=jax.ShapeDtypeStruct((T, D), F32),
        scratch_shapes=[pltpu.VMEM((2, TOP_K, tm * _slab_rows(D // 2), HEAD_DIM), y_pad.dtype),
                        pltpu.SemaphoreType.DMA((2,))],
        compiler_params=_cparams(("arbitrary",)),
        name="combine",
    )(dest, dest, x1, top_w, g, y_pad)


def _rope_tables(pos):
    inv_freq = ROPE_THETA ** (-jnp.arange(0, ROPE_DIM, 2, dtype=F32) / ROPE_DIM)
    ang = pos.astype(F32)[..., None] * inv_freq
    cos, sin = jnp.cos(ang), jnp.sin(ang)
    rest = HEAD_DIM - ROPE_DIM
    cos_t = jnp.concatenate([cos, cos, jnp.ones(cos.shape[:-1] + (rest,), F32)], axis=-1)
    sin_t = jnp.concatenate([-sin, sin, jnp.zeros(sin.shape[:-1] + (rest,), F32)], axis=-1)
    return cos_t, sin_t


def _cmp_to_slc_t(seq, nc_pad):
    n_cmp = (seq - CMP_BLOCK) // CMP_STRIDE + 1
    n_slc = seq // SEL_BLOCK
    cmp_start = np.arange(n_cmp) * CMP_STRIDE
    slc_start = np.arange(n_slc) * SEL_BLOCK
    overlap = np.clip(np.minimum(cmp_start[:, None] + CMP_BLOCK, slc_start[None, :] + SEL_BLOCK)
                      - np.maximum(cmp_start[:, None], slc_start[None, :]), 0, None)
    out = np.zeros((n_slc, nc_pad), np.float32)
    out[:, :n_cmp] = (overlap / CMP_BLOCK).T
    return jnp.asarray(out, BF16)


def _regroup_w_in(w):
    D = w.shape[0]
    o_kv = NSA_WIDTH
    o_gn = o_kv + 6 * KV_WIDTH
    o_b = o_gn + 3 * NSA_HEADS
    o_gm = o_b + 3 * MOBA_WIDTH
    kv = lambda s: w[:, o_kv + s * KV_WIDTH:o_kv + (s + 1) * KV_WIDTH]
    mb = lambda s: w[:, o_b + s * MOBA_WIDTH:o_b + (s + 1) * MOBA_WIDTH]
    cols = [w[:, :NSA_WIDTH], kv(2), kv(4), mb(0), mb(1), kv(0), kv(1), kv(3), kv(5), mb(2), w[:, o_gm:o_gm + 2 * D]]
    w_main = jnp.concatenate(cols, axis=1).astype(BF16)
    wg = w[:, o_gn:o_gn + 3 * NSA_HEADS]
    wg = wg.reshape(D, 3, NSA_GROUPS, NSA_REP).transpose(2, 1, 3, 0).reshape(NSA_GROUPS, 3 * NSA_REP, D)
    wg = jnp.pad(wg, ((0, 0), (0, GATE_ROWS - 3 * NSA_REP), (0, 0))).reshape(NSA_GROUPS * GATE_ROWS, D)
    return w_main, wg.astype(BF16)


def kernel(x, positions, g_attn_norm, w_in, pe_cmp_k, w_cmp_k1, w_cmp_k2, pe_cmp_v, w_cmp_v1, w_cmp_v2, w_proj_nsa, w_proj_moba, w_out, g_ffn_norm, w_router, b_router, w_gate_up, b_gate_up, w_down, b_down, g_final_norm):
    B, S, D = x.shape
    T = B * S
    E = w_router.shape[-1]
    F = w_down.shape[-2]
    depth = w_in.shape[0]
    assert depth == 1, "the final norm is fused into the single layer's combine step"
    assert S % MOBA_BLOCK == 0 and S >= WINDOW + NSA_TQ and (2 * D) % COL_TILE == 0 and T % 512 == 0
    nc = S // CMP_STRIDE
    tm_big = min(1024, T)

    cos_t, sin_t = _rope_tables(positions)
    at_block_end = lambda t: jnp.concatenate([t[:, CMP_BLOCK - 1::CMP_STRIDE], t[:, S - 1:]], axis=1)
    cos_c, sin_c = at_block_end(cos_t), at_block_end(sin_t)
    assert cos_c.shape[1] == nc
    c2st = _cmp_to_slc_t(S, nc)

    x2 = x.reshape(T, D)
    for layer in range(depth):
        w_main, w_gates_t = _regroup_w_in(w_in[layer])
        h, gates_t = _norm_gates(x2, g_attn_norm[layer].reshape(1, D), w_gates_t, tm_big)
        proj = _in_proj(h, w_main, cos_t.reshape(T, HEAD_DIM), sin_t.reshape(T, HEAD_DIM), min(2048, T))
        proj3 = proj.reshape(B, S, proj.shape[1])

        kc, vc = _compress(proj3, pe_cmp_k[layer], w_cmp_k1[layer].astype(BF16), w_cmp_k2[layer].astype(BF16),
                           pe_cmp_v[layer], w_cmp_v1[layer].astype(BF16), w_cmp_v2[layer].astype(BF16), cos_c, sin_c)
        o_a = _nsa_attention(proj3, kc, vc, gates_t, c2st)
        o_b = _moba_attention(proj3)

        mix = _merge_proj(o_a.reshape(T, NSA_WIDTH), o_b.reshape(T, MOBA_WIDTH),
                          w_proj_nsa[layer].astype(BF16), w_proj_moba[layer].astype(BF16), proj, tm_big)
        wr_t = w_router[layer].T
        wr_hi = wr_t.astype(BF16)
        wr_lo = (wr_t - wr_hi.astype(F32)).astype(BF16)
        x1, h2, top_e, top_w = _out_router(x2, mix, w_out[layer].astype(BF16), g_ffn_norm[layer].reshape(1, D),
                                           wr_hi, wr_lo, b_router[layer].reshape(E, 1), 512)

        n_blk = (T * TOP_K) // ROW_BLOCK + E
        n_blk_pad = -(-n_blk // HEAD_DIM) * HEAD_DIM
        dest, blk_e, pad_lo, pad_hi = _route(top_e, E, n_blk_pad, 512)
        pads = jnp.stack([pad_lo[:, 0], pad_hi[:, 0]])
        dest = dest[:TOP_K].T.reshape(T * TOP_K)
        x_pad = _dispatch(pads, dest, h2, n_blk * ROW_BLOCK, 512)
        blk_e = blk_e.reshape(n_blk_pad)
        nxt_e = _next_expert(blk_e, E)
        act = _expert_up(blk_e, nxt_e, x_pad, w_gate_up[layer], b_gate_up[layer].reshape(E, 1, 2 * F), n_blk,
                         min(1024, F))
        y_pad = _expert_down(blk_e, nxt_e, act, w_down[layer], b_down[layer].reshape(E, 1, D), n_blk)
        x2 = _combine(dest, x1, top_w, g_final_norm.reshape(1, D), y_pad, 256)
    return x2.reshape(B, S, D)
```
